```python
import math
import functools
import jax
import jax.numpy as jnp
from jax import lax
import numpy as np

D_MODEL = 1024
BATCH = 32
SEQ = 256
DEPTH = 2
DEC_BATCH = 2
DEC_SEQ = 1024
PAST_LEN = 512

GRID_W = 64
HEAD_DIM = 64
MIX_WIDTH = D_MODEL
GROUP_WIDTH = MIX_WIDTH // 4
H_SWA = GROUP_WIDTH // HEAD_DIM
KV_SWA = H_SWA // 2
WINDOW = 128
BLOCK = 128
H_GQA = GROUP_WIDTH // HEAD_DIM
KV_GQA = H_GQA // 2
HY_CH = GROUP_WIDTH
HY_ORDER = 2
HY_BANDS = 16
HY_EMB = 2 * HY_BANDS + 1
HY_FH = 64
HY_TARGET = 1e-2
HY_FAST = 0.3
HY_SLOW = 1.5
H_GLA = 4
DV_GLA = GROUP_WIDTH // H_GLA
DK_GLA = DV_GLA // 2
GLA_RANK = 16
GLA_CHUNK = 64
GLA_NORM = 16.0
D_FF = 2816
N_MOD = 9
ROPE_THETA = 10000.0
EPS = 1e-6
IN_SIZES = (H_SWA * HEAD_DIM, KV_SWA * HEAD_DIM, KV_SWA * HEAD_DIM,
            3 * HY_CH,
            H_GQA * HEAD_DIM, KV_GQA * HEAD_DIM, KV_GQA * HEAD_DIM,
            H_GLA * DK_GLA, H_GLA * DK_GLA, H_GLA * DV_GLA, 2 * GLA_RANK, H_GLA * DV_GLA)
IN_WIDTH = sum(IN_SIZES)

kernel_name = 'hymba_style_diffusion_prefix_step'


def rms_norm(x, g):
    xf = x.astype(jnp.float32)
    y = xf * lax.rsqrt(jnp.mean(jnp.square(xf), axis=-1, keepdims=True) + EPS)
    return (y * g.astype(jnp.float32)).astype(x.dtype)


def modulation(cond, w_mod, b_mod):
    m = jax.nn.silu(cond) @ w_mod + b_mod
    return jnp.split(m[..., None, :], N_MOD, axis=-1)


def modulate(x, shift, scale):
    return x * (1 + scale) + shift


def swiglu(x, w_in, w_out):
    a, b = jnp.split(x @ w_in, 2, axis=-1)
    return (jax.nn.silu(a) * b) @ w_out


def grid_rope(L):
    rows = L // GRID_W
    r = jnp.repeat(jnp.arange(rows, dtype=jnp.float32), GRID_W)
    col = jnp.tile(jnp.arange(GRID_W, dtype=jnp.float32), rows)
    nf = HEAD_DIM // 4
    inv = ROPE_THETA ** (-jnp.arange(nf, dtype=jnp.float32) / nf)
    ang = jnp.concatenate([r[:, None] * inv, col[:, None] * inv], axis=-1)
    return jnp.cos(ang), jnp.sin(ang)


def apply_rope(x, cos, sin):
    xf = x.astype(jnp.float32)
    x1, x2 = xf[..., :HEAD_DIM // 2], xf[..., HEAD_DIM // 2:]
    c, s = cos[:, None, :], sin[:, None, :]
    return jnp.concatenate([x1 * c - x2 * s, x2 * c + x1 * s], axis=-1).astype(x.dtype)


def dense_attention(q, k, v, sink):
    B, Lq, H, hd = q.shape
    KV = k.shape[2]
    G = H // KV
    nb = Lq // BLOCK
    kf = k.astype(jnp.float32)
    vf = v.astype(jnp.float32)
    qb = q.astype(jnp.float32).reshape(B, nb, BLOCK, KV, G, hd).transpose(1, 0, 2, 3, 4, 5) * (hd ** -0.5)

    def block(qblk):
        s = jnp.einsum('bqkgd,bskd->bkgqs', qblk, kf)
        if sink is not None:
            sk = jnp.broadcast_to(sink.astype(jnp.float32).reshape(1, KV, G, 1, 1), s.shape[:-1] + (1,))
            s = jnp.concatenate([s, sk], axis=-1)
        p = jax.nn.softmax(s, axis=-1)
        if sink is not None:
            p = p[..., :-1]
        return jnp.einsum('bkgqs,bskd->bqkgd', p, vf)

    o = lax.map(block, qb)
    return o.transpose(1, 0, 2, 3, 4, 5).reshape(B, Lq, H, hd).astype(q.dtype)


def window_attention(q, k, v, k_ctx, v_ctx, sink):
    B, L, H, hd = q.shape
    KV = k.shape[2]
    G = H // KV
    nb = L // BLOCK
    f32 = jnp.float32
    qb = q.astype(f32).reshape(B, nb, BLOCK, KV, G, hd) * (hd ** -0.5)

    def band(x):
        xb = x.astype(f32).reshape(B, nb, BLOCK, KV, hd)
        z = jnp.zeros_like(xb[:, :1])
        prev = jnp.concatenate([z, xb[:, :-1]], axis=1)
        nxt = jnp.concatenate([xb[:, 1:], z], axis=1)
        return jnp.concatenate([prev, xb, nxt], axis=2)

    kb, vb = band(k), band(v)
    qpos = jnp.arange(nb)[:, None] * BLOCK + jnp.arange(BLOCK)[None, :]
    kpos = (jnp.arange(nb)[:, None] - 1) * BLOCK + jnp.arange(3 * BLOCK)[None, :]
    valid = ((jnp.abs(qpos[:, :, None] - kpos[:, None, :]) <= WINDOW)
             & (kpos[:, None, :] >= 0) & (kpos[:, None, :] < L))
    s_loc = jnp.where(valid, jnp.einsum('bnqkgd,bnskd->bkgnqs', qb, kb), -1e30)
    s_ctx = jnp.einsum('bnqkgd,bckd->bkgnqc', qb, k_ctx.astype(f32))
    s_sink = jnp.broadcast_to(sink.astype(f32).reshape(1, KV, G, 1, 1, 1), s_loc.shape[:-1] + (1,))
    p = jax.nn.softmax(jnp.concatenate([s_loc, s_ctx, s_sink], axis=-1), axis=-1)
    nloc = 3 * BLOCK
    nctx = k_ctx.shape[1]
    o = (jnp.einsum('bkgnqs,bnskd->bnqkgd', p[..., :nloc], vb)
         + jnp.einsum('bkgnqc,bckd->bnqkgd', p[..., nloc:nloc + nctx], v_ctx.astype(f32)))
    return o.reshape(B, L, H, hd).astype(q.dtype)


def short_conv(u, w, b):
    up = jnp.pad(u, ((0, 0), (1, 1), (0, 0)))
    return up[:, :-2] * w[0] + up[:, 1:-1] * w[1] + up[:, 2:] * w[2] + b


def hyena_kernels(L, lp):
    f32 = jnp.float32
    t = jnp.linspace(0.0, 1.0, L, dtype=f32)[:, None]
    w = (2.0 * math.pi / L) * jnp.arange(L, dtype=f32)[:, None]
    bands = jnp.linspace(1e-4, HY_BANDS - 1, HY_BANDS, dtype=f32)[None, :]
    z = jnp.concatenate([t, jnp.cos(bands * w), -jnp.sin(bands * w)], axis=-1)
    freq = lp['hy_freq'].astype(f32)
    h = jnp.sin(freq[0] * (z @ lp['hy_w1'].astype(f32) + lp['hy_b1'].astype(f32)))
    h = jnp.sin(freq[1] * (h @ lp['hy_w2'].astype(f32) + lp['hy_b2'].astype(f32)))
    h = (h @ lp['hy_w3'].astype(f32)).reshape(L, 2, HY_ORDER, HY_CH)
    deltas = jnp.linspace(math.log(HY_TARGET) / HY_SLOW, math.log(HY_TARGET) / HY_FAST, HY_CH, dtype=f32)
    h = h * jnp.exp(-t * jnp.abs(deltas))[:, None, None, :]
    k2 = jnp.concatenate([h[:, 0], jnp.zeros((1, HY_ORDER, HY_CH), f32), h[:0:-1, 1]], axis=0)
    return jnp.fft.rfft(k2, axis=0)


def long_conv(u, kf, bias):
    L = u.shape[1]
    U = jnp.fft.rfft(u, n=2 * L, axis=1)
    y = jnp.fft.irfft(U * kf[None], n=2 * L, axis=1)[:, :L]
    return y + u * bias


def hyena_mixer(z, lp):
    L = z.shape[1]
    z = short_conv(z, lp['hy_conv_w'], lp['hy_conv_b']).astype(jnp.float32)
    v, x1, x2 = jnp.split(z, 3, axis=-1)
    kf = hyena_kernels(L, lp)
    bias = lp['hy_bias'].astype(jnp.float32)
    y = x1 * long_conv(v, kf[:, 0], bias[0])
    y = x2 * long_conv(y, kf[:, 1], bias[1])
    return y


def gla_scan(q, k, v, g, s0):
    B, L, H, DK = q.shape
    DV = v.shape[-1]
    C = GLA_CHUNK
    n = L // C
    q, k, v, g = (x.astype(jnp.float32).reshape(B, n, C, H, x.shape[-1]) for x in (q, k, v, g))
    b = jnp.cumsum(g, axis=2)
    b_last = b[:, :, -1:]
    q_t = q * jnp.exp(b) * (DK ** -0.5)
    k_t = k * jnp.exp(-b)
    k_e = k * jnp.exp(b_last - b)
    mask = jnp.tril(jnp.ones((C, C), dtype=bool))
    att = jnp.where(mask, jnp.einsum('bnihd,bnjhd->bnhij', q_t, k_t), 0.0)
    o_intra = jnp.einsum('bnhij,bnjhv->bnihv', att, v)
    contrib = jnp.einsum('bnjhd,bnjhv->nbhdv', k_e, v)
    decay = jnp.exp(b_last[:, :, 0]).transpose(1, 0, 2, 3)

    def step(S, inp):
        dec, add = inp
        return dec[..., None] * S + add, S

    s_final, s_prev = lax.scan(step, s0.astype(jnp.float32), (decay, contrib))
    o_inter = jnp.einsum('bnihd,nbhdv->bnihv', q_t, s_prev)
    return (o_intra + o_inter).reshape(B, L, H, DV), s_final


def gla_bidir(q, k, v, gl, lp, s0_fwd, s0_bwd):
    B, L = q.shape[:2]
    logits = jnp.einsum('blzr,zrd->blzd', gl, lp['gla_gate_w']) + lp['gla_gate_b']
    g = (jax.nn.log_sigmoid(logits.astype(jnp.float32)) / GLA_NORM).reshape(B, L, 2, H_GLA, DK_GLA)
    o_f, s_f = gla_scan(q, k, v, g[:, :, 0], s0_fwd)
    flip = lambda x: x[:, ::-1]
    o_b, s_b = gla_scan(flip(q), flip(k), flip(v), flip(g[:, :, 1]), s0_bwd)
    return o_f + flip(o_b), jnp.stack([s_f, s_b], axis=1)


def project(u, lp):
    B, L, _ = u.shape
    idx = np.cumsum(IN_SIZES)[:-1].tolist()
    qa, ka, va, hy, qc, kc, vc, qd, kd, vd, gd, rd = jnp.split(u @ lp['w_in'], idx, axis=-1)
    heads = lambda t, h, d: t.reshape(B, L, h, d)
    qkg = lp['qk_norm_g']
    return (heads(qa, H_SWA, HEAD_DIM), heads(ka, KV_SWA, HEAD_DIM), heads(va, KV_SWA, HEAD_DIM), hy,
            rms_norm(heads(qc, H_GQA, HEAD_DIM), qkg[0]), rms_norm(heads(kc, KV_GQA, HEAD_DIM), qkg[1]),
            heads(vc, KV_GQA, HEAD_DIM),
            heads(qd, H_GLA, DK_GLA), heads(kd, H_GLA, DK_GLA), heads(vd, H_GLA, DV_GLA),
            gd.reshape(B, L, 2, GLA_RANK), rd)


def merge_groups(oa, ob, oc, od, rd, lp):
    B, L = oa.shape[:2]
    ga, gb, gc, gd = jnp.split(lp['mix_g'], 4)
    dt = rd.dtype
    ya = rms_norm(oa.reshape(B, L, GROUP_WIDTH), ga)
    yb = rms_norm(ob.astype(dt), gb)
    yc = rms_norm(oc.reshape(B, L, GROUP_WIDTH), gc)
    yd = rms_norm(od.astype(dt), gd.reshape(H_GLA, DV_GLA)).reshape(B, L, GROUP_WIDTH) * jax.nn.silu(rd)
    return jnp.concatenate([ya, yb.astype(dt), yc, yd], axis=-1) @ lp['w_out']


def mixer_context(u, lp):
    B = u.shape[0]
    qa, ka, va, hy, qc, kc, vc, qd, kd, vd, gd, rd = project(u, lp)
    oa = dense_attention(qa, ka, va, lp['swa_sink'])
    ob = hyena_mixer(hy, lp)
    oc = dense_attention(qc, kc, vc, None)
    s0 = jnp.zeros((B, H_GLA, DK_GLA, DV_GLA), jnp.float32)
    od, st = gla_bidir(qd, kd, vd, gd, lp, s0, s0)
    return merge_groups(oa, ob, oc, od, rd, lp), (ka, va, kc, vc, st.astype(u.dtype))


def mixer_latent(u, lp, ck_a, cv_a, ck_c, cv_c, st):
    L = u.shape[1]
    cos, sin = grid_rope(L)
    qa, ka, va, hy, qc, kc, vc, qd, kd, vd, gd, rd = project(u, lp)
    oa = window_attention(apply_rope(qa, cos, sin), apply_rope(ka, cos, sin), va, ck_a, cv_a, lp['swa_sink'])
    ob = hyena_mixer(hy, lp)
    k_all = jnp.concatenate([apply_rope(kc, cos, sin), ck_c.astype(kc.dtype)], axis=1)
    v_all = jnp.concatenate([vc, cv_c.astype(vc.dtype)], axis=1)
    oc = dense_attention(apply_rope(qc, cos, sin), k_all, v_all, None)
    od, _ = gla_bidir(qd, kd, vd, gd, lp, st[:, 0], st[:, 1])
    return merge_groups(oa, ob, oc, od, rd, lp), ()


def trunk_layer(x, cond, lp, mixer):
    sh1, sc1, g1, sh2, sc2, g2, sh3, sc3, g3 = modulation(cond, lp['w_mod'], lp['b_mod'])
    ng = lp['norm_g']
    h = x + 0.5 * g1 * swiglu(modulate(rms_norm(x, ng[0]), sh1, sc1), lp['ffn_w_in'][0], lp['ffn_w_out'][0])
    mix, ctx_tensors = mixer(modulate(rms_norm(h, ng[1]), sh2, sc2))
    h = h + g2 * mix
    h = h + 0.5 * g3 * swiglu(modulate(rms_norm(h, ng[2]), sh3, sc3), lp['ffn_w_in'][1], lp['ffn_w_out'][1])
    return h, ctx_tensors


def setup_inputs(seed: int = 0) -> dict:
    key = jax.random.key(seed)
    ks = jax.random.split(key, 32)
    f32 = jnp.float32
    nrm = lambda i, shape, scale: jax.random.normal(ks[i], shape, f32) * scale
    D = D_MODEL
    return {
        'x_prompt': nrm(0, (BATCH, SEQ, D), 1.0),
        'x_sample': nrm(1, (DEC_BATCH, DEC_SEQ, D), 1.0),
        'cache_swa_k': nrm(2, (DEC_BATCH, DEPTH, PAST_LEN, KV_SWA, HEAD_DIM), 1.0),
        'cache_swa_v': nrm(3, (DEC_BATCH, DEPTH, PAST_LEN, KV_SWA, HEAD_DIM), 1.0),
        'cache_gqa_k': nrm(4, (DEC_BATCH, DEPTH, PAST_LEN, KV_GQA, HEAD_DIM), 1.0),
        'cache_gqa_v': nrm(5, (DEC_BATCH, DEPTH, PAST_LEN, KV_GQA, HEAD_DIM), 1.0),
        'state_gla': nrm(6, (DEC_BATCH, DEPTH, 2, H_GLA, DK_GLA, DV_GLA), 0.3),
        'c': nrm(7, (DEC_BATCH, D), 1.0),
        'c_ctx': nrm(8, (D,), 1.0),
        'w_mod': nrm(9, (DEPTH, D, N_MOD * D), D ** -0.5),
        'b_mod': nrm(10, (DEPTH, N_MOD * D), 0.02),
        'norm_g': 1.0 + nrm(11, (DEPTH, 3, D), 0.02),
        'ffn_w_in': nrm(12, (DEPTH, 2, D, 2 * D_FF), D ** -0.5),
        'ffn_w_out': nrm(13, (DEPTH, 2, D_FF, D), D_FF ** -0.5),
        'w_in': nrm(14, (DEPTH, D, IN_WIDTH), D ** -0.5),
        'w_out': nrm(15, (DEPTH, MIX_WIDTH, D), MIX_WIDTH ** -0.5),
        'mix_g': 1.0 + nrm(16, (DEPTH, MIX_WIDTH), 0.02),
        'swa_sink': nrm(17, (DEPTH, H_SWA), 0.5),
        'qk_norm_g': 1.0 + nrm(18, (DEPTH, 2, HEAD_DIM), 0.02),
        'hy_conv_w': nrm(19, (DEPTH, 3, 3 * HY_CH), 3 ** -0.5),
        'hy_conv_b': nrm(20, (DEPTH, 3 * HY_CH), 0.02),
        'hy_w1': nrm(21, (DEPTH, HY_EMB, HY_FH), HY_EMB ** -0.5),
        'hy_b1': nrm(22, (DEPTH, HY_FH), 0.02),
        'hy_w2': nrm(23, (DEPTH, HY_FH, HY_FH), HY_FH ** -0.5),
        'hy_b2': nrm(24, (DEPTH, HY_FH), 0.02),
        'hy_w3': nrm(25, (DEPTH, HY_FH, 2 * HY_ORDER * HY_CH), HY_FH ** -0.5),
        'hy_freq': 1.0 + nrm(26, (DEPTH, 2, HY_FH), 0.02),
        'hy_bias': nrm(27, (DEPTH, HY_ORDER, HY_CH), 0.1),
        'gla_gate_w': nrm(28, (DEPTH, 2, GLA_RANK, H_GLA * DK_GLA), GLA_RANK ** -0.5),
        'gla_gate_b': nrm(29, (DEPTH, 2, H_GLA * DK_GLA), 0.02),
        'final_g': 1.0 + nrm(30, (D,), 0.02),
    }


def reference(x_prompt, x_sample, cache_swa_k, cache_swa_v, cache_gqa_k, cache_gqa_v, state_gla, c,
              c_ctx, w_mod, b_mod, norm_g, ffn_w_in, ffn_w_out, w_in, w_out, mix_g, swa_sink, qk_norm_g,
              hy_conv_w, hy_conv_b, hy_w1, hy_b1, hy_w2, hy_b2, hy_w3, hy_freq, hy_bias,
              gla_gate_w, gla_gate_b, final_g):
    def layer_params(l):
        return {'w_mod': w_mod[l], 'b_mod': b_mod[l], 'norm_g': norm_g[l], 'ffn_w_in': ffn_w_in[l],
                'ffn_w_out': ffn_w_out[l], 'w_in': w_in[l], 'w_out': w_out[l], 'mix_g': mix_g[l],
                'swa_sink': swa_sink[l], 'qk_norm_g': qk_norm_g[l], 'hy_conv_w': hy_conv_w[l],
                'hy_conv_b': hy_conv_b[l], 'hy_w1': hy_w1[l], 'hy_b1': hy_b1[l], 'hy_w2': hy_w2[l],
                'hy_b2': hy_b2[l], 'hy_w3': hy_w3[l], 'hy_freq': hy_freq[l], 'hy_bias': hy_bias[l],
                'gla_gate_w': gla_gate_w[l], 'gla_gate_b': gla_gate_b[l]}

    hp = x_prompt
    ks_a, vs_a, ks_c, vs_c, sts = [], [], [], [], []
    for l in range(DEPTH):
        lp = layer_params(l)
        hp, (ka, va, kc, vc, st) = trunk_layer(hp, c_ctx, lp, functools.partial(mixer_context, lp=lp))
        ks_a.append(ka)
        vs_a.append(va)
        ks_c.append(kc)
        vs_c.append(vc)
        sts.append(st)
    y_prompt = rms_norm(hp, final_g)
    new_swa_k = jnp.stack(ks_a, axis=1)
    new_swa_v = jnp.stack(vs_a, axis=1)
    new_gqa_k = jnp.stack(ks_c, axis=1)
    new_gqa_v = jnp.stack(vs_c, axis=1)
    new_state_gla = jnp.stack(sts, axis=1)

    hs = x_sample
    for l in range(DEPTH):
        lp = layer_params(l)
        mixer = functools.partial(mixer_latent, lp=lp, ck_a=cache_swa_k[:, l], cv_a=cache_swa_v[:, l],
                                  ck_c=cache_gqa_k[:, l], cv_c=cache_gqa_v[:, l], st=state_gla[:, l])
        hs, _ = trunk_layer(hs, c, lp, mixer)
    y_sample = rms_norm(hs, final_g)
    return (y_prompt, y_sample, new_swa_k, new_swa_v, new_gqa_k, new_gqa_v, new_state_gla)
```

```python
import functools
import math

import numpy as np
import jax
import jax.numpy as jnp
from jax import lax
from jax.experimental import pallas as pl
from jax.experimental.pallas import tpu as pltpu

F32 = jnp.float32
BF16 = jnp.bfloat16

D_MODEL = 1024
BATCH = 32
SEQ = 256
DEPTH = 2
DEC_BATCH = 2
DEC_SEQ = 1024
PAST_LEN = 512
GRID_W = 64
HEAD_DIM = 64
GROUP_WIDTH = 256
N_HEADS = 4
N_KV = 2
WINDOW = 128
HY_CH = 256
HY_ORDER = 2
HY_BANDS = 16
HY_EMB = 2 * HY_BANDS + 1
HY_FH = 64
HY_TARGET = 1e-2
HY_FAST = 0.3
HY_SLOW = 1.5
H_GLA = 4
DV_GLA = 64
DK_GLA = 32
GLA_RANK = 16
GLA_CHUNK = 64
GLA_NORM = 16.0
D_FF = 2816
N_MOD = 9
ROPE_THETA = 10000.0
EPS = 1e-6

N_CTX = BATCH * SEQ
N_LAT = DEC_BATCH * DEC_SEQ
N_TOK = N_CTX + N_LAT
MOD_ROWS = 8

COL_HY, COL_QA, COL_KA, COL_VA = 0, 768, 1024, 1152
COL_QC, COL_KC, COL_VC = 1280, 1536, 1664
COL_QD, COL_KD, COL_VD, COL_RD, COL_GD = 1792, 1920, 2048, 2304, 2560
PROJ_WIDTH = 2688
_REF_COLS = dict(qa=0, ka=256, va=384, hy=512, qc=1280, kc=1536, vc=1664, qd=1792, kd=1920, vd=2048,
                 gd=2304, rd=2336)

TM_FFN = 1024
TK_FFN = 256
TM_PROJ = 512
TQ_ATTN = 256


def _bdot(a, b):
    return jnp.dot(a.astype(BF16), b.astype(BF16), preferred_element_type=F32)


def _bdot_nt(a, b):
    return lax.dot_general(a.astype(BF16), b.astype(BF16), (((1,), (1,)), ((), ())),
                           preferred_element_type=F32)


def _bdot_tn(a, b):
    return lax.dot_general(a.astype(BF16), b.astype(BF16), (((0,), (0,)), ((), ())),
                           preferred_element_type=F32)


def _fdot(a, b):
    return jnp.dot(a, b, preferred_element_type=F32, precision=lax.Precision.HIGHEST)


def _rms(x, g):
    return x * lax.rsqrt(jnp.mean(x * x, axis=-1, keepdims=True) + EPS) * g


def _silu(x):
    return x * jax.nn.sigmoid(x)


def _mod_row(i, tm):
    n_ctx_tiles = N_CTX // tm
    per_batch = DEC_SEQ // tm
    return jnp.where(i < n_ctx_tiles, 0, 1 + (i - n_ctx_tiles) // per_batch)


def _mod_spec(tm, j, grid_rank):
    if grid_rank == 1:
        return pl.BlockSpec((None, None, 1, D_MODEL), lambda i: (_mod_row(i, tm), j, 0, 0))
    return pl.BlockSpec((None, None, 1, D_MODEL), lambda i, k: (_mod_row(i, tm), j, 0, 0))


def _mod_kernel(cond_ref, w_ref, b_ref, o_ref):
    o_ref[...] = _bdot(_silu(cond_ref[...]), w_ref[...]) + b_ref[...]


def _modulation(cond, w_mod, b_mod):
    tn = D_MODEL
    return pl.pallas_call(
        _mod_kernel,
        grid=(DEPTH, N_MOD * D_MODEL // tn),
        in_specs=[
            pl.BlockSpec((MOD_ROWS, D_MODEL), lambda l, j: (0, 0)),
            pl.BlockSpec((None, D_MODEL, tn), lambda l, j: (l, 0, j)),
            pl.BlockSpec((None, 1, tn), lambda l, j: (l, 0, j)),
        ],
        out_specs=pl.BlockSpec((None, MOD_ROWS, tn), lambda l, j: (l, 0, j)),
        out_shape=jax.ShapeDtypeStruct((DEPTH, MOD_ROWS, N_MOD * D_MODEL), F32),
        name="modulation",
    )(cond, w_mod, b_mod.reshape(DEPTH, 1, N_MOD * D_MODEL))


def _ffn_kernel(x_ref, sh_ref, sc_ref, gt_ref, ng_ref, wa_ref, wb_ref, wo_ref, o_ref, xm_ref, acc_ref):
    k = pl.program_id(1)

    @pl.when(k == 0)
    def _():
        xn = _rms(x_ref[...], ng_ref[...])
        xm_ref[...] = (xn * (1.0 + sc_ref[...]) + sh_ref[...]).astype(BF16)
        acc_ref[...] = jnp.zeros_like(acc_ref)

    xm = xm_ref[...]
    a = jnp.dot(xm, wa_ref[...], preferred_element_type=F32)
    b = jnp.dot(xm, wb_ref[...], preferred_element_type=F32)
    acc_ref[...] += _bdot(_silu(a) * b, wo_ref[...])

    @pl.when(k == pl.num_programs(1) - 1)
    def _():
        o_ref[...] = x_ref[...] + 0.5 * gt_ref[...] * acc_ref[...]


def _ffn(h, mod, j0, ng, w_in, w_out):
    tm, tk = TM_FFN, TK_FFN
    nk = D_FF // tk
    return pl.pallas_call(
        _ffn_kernel,
        grid=(N_TOK // tm, nk),
        in_specs=[
            pl.BlockSpec((tm, D_MODEL), lambda i, k: (i, 0)),
            _mod_spec(tm, j0, 2), _mod_spec(tm, j0 + 1, 2), _mod_spec(tm, j0 + 2, 2),
            pl.BlockSpec((1, D_MODEL), lambda i, k: (0, 0)),
            pl.BlockSpec((D_MODEL, tk), lambda i, k: (0, k)),
            pl.BlockSpec((D_MODEL, tk), lambda i, k: (0, k + nk)),
            pl.BlockSpec((tk, D_MODEL), lambda i, k: (k, 0)),
        ],
        out_specs=pl.BlockSpec((tm, D_MODEL), lambda i, k: (i, 0)),
        out_shape=jax.ShapeDtypeStruct((N_TOK, D_MODEL), F32),
        scratch_shapes=[pltpu.VMEM((tm, D_MODEL), BF16), pltpu.VMEM((tm, D_MODEL), F32)],
        compiler_params=pltpu.CompilerParams(dimension_semantics=("parallel", "arbitrary")),
        name="ffn",
    )(h, mod, mod, mod, ng.reshape(1, D_MODEL), w_in, w_in, w_out)


def _proj_kernel(x_ref, sh_ref, sc_ref, ng_ref, w_ref, o_ref):
    xn = _rms(x_ref[...], ng_ref[...])
    o_ref[...] = _bdot(xn * (1.0 + sc_ref[...]) + sh_ref[...], w_ref[...])


def _in_proj(h, mod, ng, w):
    tm = TM_PROJ
    return pl.pallas_call(
        _proj_kernel,
        grid=(N_TOK // tm,),
        in_specs=[
            pl.BlockSpec((tm, D_MODEL), lambda i: (i, 0)),
            _mod_spec(tm, 3, 1), _mod_spec(tm, 4, 1),
            pl.BlockSpec((1, D_MODEL), lambda i: (0, 0)),
            pl.BlockSpec((D_MODEL, PROJ_WIDTH), lambda i: (0, 0)),
        ],
        out_specs=pl.BlockSpec((tm, PROJ_WIDTH), lambda i: (i, 0)),
        out_shape=jax.ShapeDtypeStruct((N_TOK, PROJ_WIDTH), F32),
        compiler_params=pltpu.CompilerParams(dimension_semantics=("parallel",)),
        name="in_proj",
    )(h, mod, mod, ng.reshape(1, D_MODEL), w)


def _rope_tables(length):
    rows = length // GRID_W
    r = np.repeat(np.arange(rows, dtype=np.float32), GRID_W)
    col = np.tile(np.arange(GRID_W, dtype=np.float32), rows)
    nf = HEAD_DIM // 4
    inv = (np.float32(ROPE_THETA) ** (-np.arange(nf, dtype=np.float32) / nf)).astype(np.float32)
    ang = np.concatenate([r[:, None] * inv, col[:, None] * inv], axis=-1).astype(np.float32)
    return np.cos(ang).astype(np.float32), np.sin(ang).astype(np.float32)


def _rope(x, cos, sin):
    half = HEAD_DIM // 2
    x1, x2 = x[:, :half], x[:, half:]
    return jnp.concatenate([x1 * cos - x2 * sin, x2 * cos + x1 * sin], axis=-1)


def _attn_kernel(*refs, latent, window, has_sink, qk_norm):
    refs = list(refs)
    q_ref, k_ref, v_ref = refs[:3]
    pos = 3
    if latent:
        kc_ref, vc_ref, cq_ref, sq_ref, ck_ref, sk_ref = refs[pos:pos + 6]
        pos += 6
    if has_sink:
        sink_ref = refs[pos]
        pos += 1
    if qk_norm:
        qkg_ref = refs[pos]
        pos += 1
    o_ref = refs[pos]
    kn_ref = refs[pos + 1] if (qk_norm and not latent) else None

    tq = q_ref.shape[0]
    lk = k_ref.shape[0]
    if window:
        q0 = pl.program_id(1) * tq
        qpos = q0 + lax.broadcasted_iota(jnp.int32, (tq, lk), 0)
        kpos = lax.broadcasted_iota(jnp.int32, (tq, lk), 1)
        valid = jnp.abs(qpos - kpos) <= WINDOW

    for kh in range(N_KV):
        cols = slice(kh * HEAD_DIM, (kh + 1) * HEAD_DIM)
        k_h = k_ref[:, cols]
        v_h = v_ref[:, cols]
        if qk_norm:
            k_h = _rms(k_h, qkg_ref[1:2, :])
            if kn_ref is not None:
                kn_ref[:, cols] = k_h
        if latent:
            k_h = _rope(k_h, ck_ref[...], sk_ref[...])
            kc_h = kc_ref[:, cols]
            vc_h = vc_ref[:, cols]
        for g in range(N_HEADS // N_KV):
            h = kh * (N_HEADS // N_KV) + g
            hcols = slice(h * HEAD_DIM, (h + 1) * HEAD_DIM)
            q_h = q_ref[:, hcols]
            if qk_norm:
                q_h = _rms(q_h, qkg_ref[0:1, :])
            if latent:
                q_h = _rope(q_h, cq_ref[...], sq_ref[...])
            q_h = q_h * (HEAD_DIM ** -0.5)
            s = _bdot_nt(q_h, k_h)
            if window:
                s = jnp.where(valid, s, -1e30)
            m = jnp.max(s, axis=-1, keepdims=True)
            if latent:
                s_c = _bdot_nt(q_h, kc_h)
                m = jnp.maximum(m, jnp.max(s_c, axis=-1, keepdims=True))
            if has_sink:
                sk = sink_ref[0:1, h:h + 1]
                m = jnp.maximum(m, sk)
            p = jnp.exp(s - m)
            den = jnp.sum(p, axis=-1, keepdims=True)
            o = _bdot(p, v_h)
            if latent:
                p_c = jnp.exp(s_c - m)
                den = den + jnp.sum(p_c, axis=-1, keepdims=True)
                o = o + _bdot(p_c, vc_h)
            if has_sink:
                den = den + jnp.exp(sk - m)
            o_ref[:, hcols] = o / den


def _attention(u, prev, *, latent, col_q, col_k, col_v, window=False, sink=None, qkg=None,
               cache_k=None, cache_v=None):
    tq = TQ_ATTN
    kvw = N_KV * HEAD_DIM
    qb, kb, vb = col_q // GROUP_WIDTH, col_k // kvw, col_v // kvw
    has_sink, qk_norm = sink is not None, qkg is not None
    args, in_specs = [u, u, u], None
    if latent:
        nq = DEC_SEQ // tq
        row0_q, row0_k = N_CTX // tq, N_CTX // DEC_SEQ
        grid = (DEC_BATCH, nq)
        in_specs = [
            pl.BlockSpec((tq, GROUP_WIDTH), lambda b, i: (row0_q + b * nq + i, qb)),
            pl.BlockSpec((DEC_SEQ, kvw), lambda b, i: (row0_k + b, kb)),
            pl.BlockSpec((DEC_SEQ, kvw), lambda b, i: (row0_k + b, vb)),
            pl.BlockSpec((None, PAST_LEN, kvw), lambda b, i: (b, 0, 0)),
            pl.BlockSpec((None, PAST_LEN, kvw), lambda b, i: (b, 0, 0)),
            pl.BlockSpec((tq, HEAD_DIM // 2), lambda b, i: (i, 0)),
            pl.BlockSpec((tq, HEAD_DIM // 2), lambda b, i: (i, 0)),
            pl.BlockSpec((DEC_SEQ, HEAD_DIM // 2), lambda b, i: (0, 0)),
            pl.BlockSpec((DEC_SEQ, HEAD_DIM // 2), lambda b, i: (0, 0)),
        ]
        cos, sin = _rope_tables(DEC_SEQ)
        args += [cache_k, cache_v, cos, sin, cos, sin]
        const = lambda b, i: (0, 0)
        out_specs = pl.BlockSpec((tq, GROUP_WIDTH), lambda b, i: (row0_q + b * nq + i, 0))
        sem = ("parallel", "arbitrary")
    else:
        grid = (BATCH,)
        in_specs = [
            pl.BlockSpec((SEQ, GROUP_WIDTH), lambda b: (b, qb)),
            pl.BlockSpec((SEQ, kvw), lambda b: (b, kb)),
            pl.BlockSpec((SEQ, kvw), lambda b: (b, vb)),
        ]
        const = lambda b: (0, 0)
        out_specs = pl.BlockSpec((SEQ, GROUP_WIDTH), lambda b: (b, 0))
        sem = ("parallel",)
    if has_sink:
        args.append(sink.reshape(1, N_HEADS))
        in_specs.append(pl.BlockSpec((1, N_HEADS), const))
    if qk_norm:
        args.append(qkg)
        in_specs.append(pl.BlockSpec((2, HEAD_DIM), const))
    out_shape = jax.ShapeDtypeStruct((N_TOK, GROUP_WIDTH), F32)
    aliases = {}
    if latent:
        aliases = {len(args): 0}
        args.append(prev)
        in_specs.append(pl.BlockSpec(memory_space=pl.ANY))
    elif qk_norm:
        out_shape = (out_shape, jax.ShapeDtypeStruct((N_CTX, kvw), F32))
        out_specs = (out_specs, pl.BlockSpec((SEQ, kvw), lambda b: (b, 0)))

    def body(*refs):
        if latent:
            refs = refs[:len(args) - 1] + refs[len(args):]
        _attn_kernel(*refs, latent=latent, window=window, has_sink=has_sink, qk_norm=qk_norm)

    return pl.pallas_call(
        body, grid=grid, in_specs=in_specs, out_specs=out_specs, out_shape=out_shape,
        input_output_aliases=aliases,
        compiler_params=pltpu.CompilerParams(dimension_semantics=sem),
        name="attn_latent" if latent else "attn_context",
    )(*args)


def _dft_matrices(length):
    n = length
    k = np.arange(n, dtype=np.int64)[:, None]
    s = np.arange(n, dtype=np.int64)[None, :]
    ang = np.pi * ((k * s) % (2 * n)).astype(np.float64) / n
    fwd_cos = np.cos(ang)
    fwd_sin = -np.sin(ang)
    fwd_sin[0, :] = 1.0 - 2.0 * (np.arange(n) % 2)
    fwd = np.concatenate([fwd_cos, fwd_sin], axis=0)
    wk = np.full((n,), 2.0)
    wk[0] = 1.0
    inv_cos = (np.cos(ang) * wk[:, None]).T / (2 * n)
    inv_sin = (-2.0 * np.sin(ang)).T / (2 * n)
    inv_sin[:, 0] = (1.0 - 2.0 * (np.arange(n) % 2)) / (2 * n)
    inv = np.concatenate([inv_cos, inv_sin], axis=1)
    return fwd.astype(np.float32), inv.astype(np.float32)


def _filter_features(length):
    t = np.linspace(0.0, 1.0, length, dtype=np.float32)[:, None]
    w = (np.float32(2.0 * math.pi / length) * np.arange(length, dtype=np.float32))[:, None]
    bands = np.linspace(1e-4, HY_BANDS - 1, HY_BANDS, dtype=np.float32)[None, :]
    z = np.concatenate([t, np.cos(bands * w), -np.sin(bands * w)], axis=-1).astype(np.float32)
    zp = np.zeros((length, 128), np.float32)
    zp[:, :HY_EMB] = z
    deltas = np.linspace(math.log(HY_TARGET) / HY_SLOW, math.log(HY_TARGET) / HY_FAST, HY_CH, dtype=np.float32)
    decay = np.exp(-t * np.abs(deltas)).astype(np.float32)
    return zp, decay


def _filter_kernel(z_ref, w1_ref, b1_ref, w2_ref, b2_ref, w3_ref, fr_ref, dec_ref, f_ref, kr_ref, ki_ref):
    n = z_ref.shape[0]
    h = jnp.sin(fr_ref[0:1, :] * (_fdot(z_ref[...], w1_ref[...]) + b1_ref[...]))
    h = jnp.sin(fr_ref[1:2, :] * (_fdot(h, w2_ref[...]) + b2_ref[...]))
    h = _fdot(h, w3_ref[...])
    dec = dec_ref[...]
    half = HY_ORDER * HY_CH
    pos = jnp.concatenate([h[:, j * HY_CH:(j + 1) * HY_CH] * dec for j in range(HY_ORDER)], axis=-1)
    neg = jnp.concatenate([h[:, half + j * HY_CH:half + (j + 1) * HY_CH] * dec for j in range(HY_ORDER)], axis=-1)
    row = lax.broadcasted_iota(jnp.int32, (n, half), 0)
    neg = jnp.where(row == 0, 0.0, neg)
    fwd = f_ref[...]
    a = jnp.dot(fwd, (pos + neg).astype(BF16), preferred_element_type=F32)
    b = jnp.dot(fwd, (pos - neg).astype(BF16), preferred_element_type=F32)
    kr_ref[...] = a[:n]
    ki_ref[...] = jnp.where(row == 0, a[n:], b[n:])


def _hyena_filters(length, fwd, w1p, b1, w2, b2, w3, freq):
    zp, decay = _filter_features(length)
    half = HY_ORDER * HY_CH
    return pl.pallas_call(
        _filter_kernel,
        out_shape=(jax.ShapeDtypeStruct((length, half), F32), jax.ShapeDtypeStruct((length, half), F32)),
        name="hyena_filters",
    )(zp, w1p, b1.reshape(1, HY_FH), w2, b2.reshape(1, HY_FH), w3, freq, decay, fwd)


def _hyena_kernel(z_ref, cw_ref, cb_ref, f_ref, g_ref, kr_ref, ki_ref, bias_ref, o_ref):
    n = z_ref.shape[0]
    z = z_ref[...]
    row = lax.broadcasted_iota(jnp.int32, z.shape, 0)
    prev = jnp.where(row == 0, 0.0, pltpu.roll(z, 1, 0))
    nxt = jnp.where(row == n - 1, 0.0, pltpu.roll(z, n - 1, 0))
    z = prev * cw_ref[0:1, :] + z * cw_ref[1:2, :] + nxt * cw_ref[2:3, :] + cb_ref[...]
    v, x1, x2 = z[:, :HY_CH], z[:, HY_CH:2 * HY_CH], z[:, 2 * HY_CH:]
    row0 = lax.broadcasted_iota(jnp.int32, (n, HY_CH), 0) == 0

    def long_conv(x, order):
        cols = slice(order * HY_CH, (order + 1) * HY_CH)
        kr, ki = kr_ref[:, cols], ki_ref[:, cols]
        spec = jnp.dot(f_ref[...], x.astype(BF16), preferred_element_type=F32)
        ur, ui = spec[:n], spec[n:]
        yr = ur * kr - jnp.where(row0, 0.0, ui * ki)
        yi = jnp.where(row0, ui * ki, ur * ki + ui * kr)
        y = jnp.dot(g_ref[...], jnp.concatenate([yr, yi], axis=0).astype(BF16), preferred_element_type=F32)
        return y + x * bias_ref[order:order + 1, :]

    y = x1 * long_conv(v, 0)
    o_ref[...] = x2 * long_conv(y, 1)


def _hyena(u, prev, *, latent, fwd, inv, kr, ki, conv_w, conv_b, bias):
    n = DEC_SEQ if latent else SEQ
    nb = DEC_BATCH if latent else BATCH
    row0 = N_CTX // n if latent else 0
    half = HY_ORDER * HY_CH
    const = lambda b: (0, 0)
    args = [u, conv_w, conv_b.reshape(1, 3 * HY_CH), fwd, inv, kr, ki, bias]
    in_specs = [
        pl.BlockSpec((n, 3 * HY_CH), lambda b: (row0 + b, COL_HY // (3 * HY_CH))),
        pl.BlockSpec((3, 3 * HY_CH), const),
        pl.BlockSpec((1, 3 * HY_CH), const),
        pl.BlockSpec((2 * n, n), const),
        pl.BlockSpec((n, 2 * n), const),
        pl.BlockSpec((n, half), const),
        pl.BlockSpec((n, half), const),
        pl.BlockSpec((HY_ORDER, HY_CH), const),
    ]
    aliases = {}
    if latent:
        aliases = {len(args): 0}
        args.append(prev)
        in_specs.append(pl.BlockSpec(memory_space=pl.ANY))

    def body(*refs):
        if latent:
            refs = refs[:len(args) - 1] + refs[len(args):]
        _hyena_kernel(*refs)

    return pl.pallas_call(
        body, grid=(nb,), in_specs=in_specs,
        out_specs=pl.BlockSpec((n, HY_CH), lambda b: (row0 + b, 0)),
        out_shape=jax.ShapeDtypeStruct((N_TOK, HY_CH), F32),
        input_output_aliases=aliases,
        compiler_params=pltpu.CompilerParams(dimension_semantics=("parallel",)),
        name="hyena_latent" if latent else "hyena_context",
    )(*args)


def _gla_kernel(*refs, latent):
    if latent:
        q_ref, k_ref, v_ref, gl_ref, gw_ref, gb_ref, mg_ref, s0_ref, o_ref, st_ref = refs
        sf_ref = None
    else:
        q_ref, k_ref, v_ref, gl_ref, gw_ref, gb_ref, mg_ref, o_ref, sf_ref, st_ref = refs
    n = q_ref.shape[0]
    c = GLA_CHUNK
    n_chunks = n // c
    ri = lax.broadcasted_iota(jnp.int32, (c, c), 0)
    ci = lax.broadcasted_iota(jnp.int32, (c, c), 1)

    def direction(z):
        causal = (ci <= ri) if z == 0 else (ci >= ri)
        tri = causal.astype(F32)
        if latent:
            st_ref[...] = s0_ref[z]
        else:
            st_ref[...] = jnp.zeros_like(st_ref)

        def chunk(step, carry):
            idx = step if z == 0 else n_chunks - 1 - step
            rows = pl.ds(pl.multiple_of(idx * c, c), c)
            logits = _bdot(gl_ref[rows, :], gw_ref[z]) + gb_ref[z:z + 1, :]
            g = (jnp.minimum(logits, 0.0) - jnp.log1p(jnp.exp(-jnp.abs(logits)))) / GLA_NORM
            b = _fdot(tri, g)
            b_tot = b[c - 1:c, :] if z == 0 else b[0:1, :]
            q_t = q_ref[rows, :] * jnp.exp(b) * (DK_GLA ** -0.5)
            kk = k_ref[rows, :]
            k_t = kk * jnp.exp(-b)
            k_e = kk * jnp.exp(b_tot - b)
            v = v_ref[rows, :]
            st = st_ref[...]
            st_ref[...] = st * jnp.exp(b_tot)
            for h in range(H_GLA):
                kc = slice(h * DK_GLA, (h + 1) * DK_GLA)
                vc = slice(h * DV_GLA, (h + 1) * DV_GLA)
                att = jnp.where(causal, _bdot_nt(q_t[:, kc], k_t[:, kc]), 0.0)
                o = _bdot(att, v[:, vc]) + _bdot_nt(q_t[:, kc], st[:, kc])
                st_ref[:, kc] += _bdot_tn(v[:, vc], k_e[:, kc])
                if z == 0:
                    o_ref[rows, vc] = o
                else:
                    o_ref[rows, vc] = _rms(o_ref[rows, vc] + o, mg_ref[:, vc])
            return carry

        lax.fori_loop(0, n_chunks, chunk, 0)
        if sf_ref is not None:
            sf_ref[z] = st_ref[...]

    direction(0)
    direction(1)


def _gla(u, prev, *, latent, gate_w, gate_b, mix_g_d, s0=None):
    n = DEC_SEQ if latent else SEQ
    nb = DEC_BATCH if latent else BATCH
    row0 = N_CTX // n if latent else 0
    hk = H_GLA * DK_GLA
    const2 = lambda b: (0, 0)
    args = [u, u, u, u, gate_w, gate_b, mix_g_d]
    in_specs = [
        pl.BlockSpec((n, hk), lambda b: (row0 + b, COL_QD // hk)),
        pl.BlockSpec((n, hk), lambda b: (row0 + b, COL_KD // hk)),
        pl.BlockSpec((n, GROUP_WIDTH), lambda b: (row0 + b, COL_VD // GROUP_WIDTH)),
        pl.BlockSpec((n, 128), lambda b: (row0 + b, COL_GD // 128)),
        pl.BlockSpec((2, 128, hk), lambda b: (0, 0, 0)),
        pl.BlockSpec((2, hk), const2),
        pl.BlockSpec((1, GROUP_WIDTH), const2),
    ]
    out_shape = jax.ShapeDtypeStruct((N_TOK, GROUP_WIDTH), F32)
    out_specs = pl.BlockSpec((n, GROUP_WIDTH), lambda b: (row0 + b, 0))
    aliases = {}
    if latent:
        args.append(s0)
        in_specs.append(pl.BlockSpec((None, 2, DV_GLA, hk), lambda b: (b, 0, 0, 0)))
        aliases = {len(args): 0}
        args.append(prev)
        in_specs.append(pl.BlockSpec(memory_space=pl.ANY))
    else:
        out_shape = (out_shape, jax.ShapeDtypeStruct((BATCH, 2, DV_GLA, hk), F32))
        out_specs = (out_specs, pl.BlockSpec((None, 2, DV_GLA, hk), lambda b: (b, 0, 0, 0)))

    def body(*refs):
        if latent:
            refs = refs[:len(args) - 1] + refs[len(args):]
        _gla_kernel(*refs, latent=latent)

    return pl.pallas_call(
        body, grid=(nb,), in_specs=in_specs, out_specs=out_specs, out_shape=out_shape,
        scratch_shapes=[pltpu.VMEM((DV_GLA, hk), F32)],
        input_output_aliases=aliases,
        compiler_params=pltpu.CompilerParams(dimension_semantics=("parallel",)),
        name="gla_latent" if latent else "gla_context",
    )(*args)


def _merge_kernel(h_ref, oa_ref, ob_ref, oc_ref, od_ref, rd_ref, gt_ref, mg_ref, wo_ref, o_ref):
    gw = GROUP_WIDTH
    ys = [
        _rms(oa_ref[...], mg_ref[:, 0:gw]),
        _rms(ob_ref[...], mg_ref[:, gw:2 * gw]),
        _rms(oc_ref[...], mg_ref[:, 2 * gw:3 * gw]),
        od_ref[...] * _silu(rd_ref[...]),
    ]
    mix = _bdot(ys[0], wo_ref[0:gw, :])
    for j in range(1, 4):
        mix = mix + _bdot(ys[j], wo_ref[j * gw:(j + 1) * gw, :])
    o_ref[...] = h_ref[...] + gt_ref[...] * mix


def _merge(h, oa, ob, oc, od, u, mod, mix_g, w_out):
    tm = TM_PROJ
    grp = pl.BlockSpec((tm, GROUP_WIDTH), lambda i: (i, 0))
    return pl.pallas_call(
        _merge_kernel,
        grid=(N_TOK // tm,),
        in_specs=[
            pl.BlockSpec((tm, D_MODEL), lambda i: (i, 0)),
            grp, grp, grp, grp,
            pl.BlockSpec((tm, GROUP_WIDTH), lambda i: (i, COL_RD // GROUP_WIDTH)),
            _mod_spec(tm, 5, 1),
            pl.BlockSpec((1, D_MODEL), lambda i: (0, 0)),
            pl.BlockSpec((D_MODEL, D_MODEL), lambda i: (0, 0)),
        ],
        out_specs=pl.BlockSpec((tm, D_MODEL), lambda i: (i, 0)),
        out_shape=jax.ShapeDtypeStruct((N_TOK, D_MODEL), F32),
        compiler_params=pltpu.CompilerParams(dimension_semantics=("parallel",)),
        name="merge",
    )(h, oa, ob, oc, od, u, mod, mix_g.reshape(1, D_MODEL), w_out)


def _norm_kernel(x_ref, g_ref, o_ref):
    o_ref[...] = _rms(x_ref[...], g_ref[...])


def _final_norm(h, g, row0, rows):
    tm = TM_PROJ
    return pl.pallas_call(
        _norm_kernel,
        grid=(rows // tm,),
        in_specs=[pl.BlockSpec((tm, D_MODEL), lambda i: (row0 // tm + i, 0)),
                  pl.BlockSpec((1, D_MODEL), lambda i: (0, 0))],
        out_specs=pl.BlockSpec((tm, D_MODEL), lambda i: (i, 0)),
        out_shape=jax.ShapeDtypeStruct((rows, D_MODEL), F32),
        compiler_params=pltpu.CompilerParams(dimension_semantics=("parallel",)),
        name="final_norm",
    )(h, g.reshape(1, D_MODEL))


def _permute_w_in(w):
    r = _REF_COLS
    parts = [w[:, r['hy']:r['hy'] + 768], w[:, r['qa']:r['qa'] + 256], w[:, r['ka']:r['ka'] + 128],
             w[:, r['va']:r['va'] + 128], w[:, r['qc']:r['qc'] + 256], w[:, r['kc']:r['kc'] + 128],
             w[:, r['vc']:r['vc'] + 128], w[:, r['qd']:r['qd'] + 128], w[:, r['kd']:r['kd'] + 128],
             w[:, r['vd']:r['vd'] + 256], w[:, r['rd']:r['rd'] + 256], w[:, r['gd']:r['gd'] + 32],
             jnp.zeros((D_MODEL, 128 - 2 * GLA_RANK), w.dtype)]
    return jnp.concatenate(parts, axis=1)


def _gate_weights(gate_w):
    out = jnp.zeros((2, 128, H_GLA * DK_GLA), gate_w.dtype)
    out = out.at[0, 0:GLA_RANK].set(gate_w[0])
    return out.at[1, GLA_RANK:2 * GLA_RANK].set(gate_w[1])


def _states_to_kernel(st):
    b = st.shape[0]
    return st.transpose(0, 1, 4, 2, 3).reshape(b, 2, DV_GLA, H_GLA * DK_GLA)


def _states_from_kernel(st):
    b = st.shape[0]
    return st.reshape(b, 2, DV_GLA, H_GLA, DK_GLA).transpose(0, 1, 3, 4, 2)


def kernel(x_prompt, x_sample, cache_swa_k, cache_swa_v, cache_gqa_k, cache_gqa_v, state_gla, c, c_ctx, w_mod, b_mod, norm_g, ffn_w_in, ffn_w_out, w_in, w_out, mix_g, swa_sink, qk_norm_g, hy_conv_w, hy_conv_b, hy_w1, hy_b1, hy_w2, hy_b2, hy_w3, hy_freq, hy_bias, gla_gate_w, gla_gate_b, final_g):
    kvw = N_KV * HEAD_DIM
    cond = jnp.zeros((MOD_ROWS, D_MODEL), F32).at[0].set(c_ctx).at[1:1 + DEC_BATCH].set(c)
    mod_all = _modulation(cond, w_mod, b_mod).reshape(DEPTH, MOD_ROWS, N_MOD, 1, D_MODEL)

    dft = {n: tuple(jnp.asarray(m).astype(BF16) for m in _dft_matrices(n)) for n in (SEQ, DEC_SEQ)}

    h = jnp.concatenate([x_prompt.reshape(N_CTX, D_MODEL), x_sample.reshape(N_LAT, D_MODEL)], axis=0)
    ks_a, vs_a, ks_c, vs_c, sts = [], [], [], [], []
    for l in range(DEPTH):
        mod = mod_all[l]
        ffn_in = ffn_w_in[l].astype(BF16)
        ffn_out = ffn_w_out[l].astype(BF16)
        h = _ffn(h, mod, 0, norm_g[l, 0], ffn_in[0], ffn_out[0])
        u = _in_proj(h, mod, norm_g[l, 1], _permute_w_in(w_in[l]).astype(BF16))

        ck_a = cache_swa_k[:, l].reshape(DEC_BATCH, PAST_LEN, kvw)
        cv_a = cache_swa_v[:, l].reshape(DEC_BATCH, PAST_LEN, kvw)
        ck_c = cache_gqa_k[:, l].reshape(DEC_BATCH, PAST_LEN, kvw)
        cv_c = cache_gqa_v[:, l].reshape(DEC_BATCH, PAST_LEN, kvw)

        oa = _attention(u, None, latent=False, col_q=COL_QA, col_k=COL_KA, col_v=COL_VA, sink=swa_sink[l])
        oa = _attention(u, oa, latent=True, col_q=COL_QA, col_k=COL_KA, col_v=COL_VA, window=True,
                        sink=swa_sink[l], cache_k=ck_a, cache_v=cv_a)
        oc, kc_n = _attention(u, None, latent=False, col_q=COL_QC, col_k=COL_KC, col_v=COL_VC,
                              qkg=qk_norm_g[l])
        oc = _attention(u, oc, latent=True, col_q=COL_QC, col_k=COL_KC, col_v=COL_VC, qkg=qk_norm_g[l],
                        cache_k=ck_c, cache_v=cv_c)

        w1p = jnp.zeros((128, HY_FH), F32).at[:HY_EMB].set(hy_w1[l])
        ob = None
        for latent, n in ((False, SEQ), (True, DEC_SEQ)):
            fwd, inv = dft[n]
            kr, ki = _hyena_filters(n, fwd, w1p, hy_b1[l], hy_w2[l], hy_b2[l], hy_w3[l], hy_freq[l])
            ob = _hyena(u, ob, latent=latent, fwd=fwd, inv=inv, kr=kr, ki=ki, conv_w=hy_conv_w[l],
                        conv_b=hy_conv_b[l], bias=hy_bias[l])

        gw = _gate_weights(gla_gate_w[l]).astype(BF16)
        mg_d = mix_g[l, 3 * GROUP_WIDTH:].reshape(1, GROUP_WIDTH)
        od, st = _gla(u, None, latent=False, gate_w=gw, gate_b=gla_gate_b[l], mix_g_d=mg_d)
        od = _gla(u, od, latent=True, gate_w=gw, gate_b=gla_gate_b[l], mix_g_d=mg_d,
                  s0=_states_to_kernel(state_gla[:, l]))

        h = _merge(h, oa, ob, oc, od, u, mod, mix_g[l], w_out[l].astype(BF16))
        h = _ffn(h, mod, 6, norm_g[l, 2], ffn_in[1], ffn_out[1])

        ks_a.append(u[:N_CTX, COL_KA:COL_KA + kvw].reshape(BATCH, SEQ, N_KV, HEAD_DIM))
        vs_a.append(u[:N_CTX, COL_VA:COL_VA + kvw].reshape(BATCH, SEQ, N_KV, HEAD_DIM))
        ks_c.append(kc_n.reshape(BATCH, SEQ, N_KV, HEAD_DIM))
        vs_c.append(u[:N_CTX, COL_VC:COL_VC + kvw].reshape(BATCH, SEQ, N_KV, HEAD_DIM))
        sts.append(_states_from_kernel(st))

    y_prompt = _final_norm(h, final_g, 0, N_CTX).reshape(BATCH, SEQ, D_MODEL)
    y_sample = _final_norm(h, final_g, N_CTX, N_LAT).reshape(DEC_BATCH, DEC_SEQ, D_MODEL)
    return (y_prompt, y_sample, jnp.stack(ks_a, axis=1), jnp.stack(vs_a, axis=1),
            jnp.stack(ks_c, axis=1), jnp.stack(vs_c, axis=1), jnp.stack(sts, axis=1))
```

```python
import functools
import math

import numpy as np
import jax
import jax.numpy as jnp
from jax import lax
from jax.experimental import pallas as pl
from jax.experimental.pallas import tpu as pltpu

F32 = jnp.float32
BF16 = jnp.bfloat16

D_MODEL = 1024
BATCH = 32
SEQ = 256
DEPTH = 2
DEC_BATCH = 2
DEC_SEQ = 1024
PAST_LEN = 512
GRID_W = 64
HEAD_DIM = 64
GROUP_WIDTH = 256
N_HEADS = 4
N_KV = 2
KV_WIDTH = N_KV * HEAD_DIM
WINDOW = 128
HY_CH = 256
HY_ORDER = 2
HY_BANDS = 16
HY_EMB = 2 * HY_BANDS + 1
HY_FH = 64
HY_TARGET = 1e-2
HY_FAST = 0.3
HY_SLOW = 1.5
H_GLA = 4
DV_GLA = 64
DK_GLA = 32
GLA_RANK = 16
GLA_CHUNK = 64
GLA_NORM = 16.0
D_FF = 2816
N_MOD = 9
ROPE_THETA = 10000.0
EPS = 1e-6

N_CTX = BATCH * SEQ
N_LAT = DEC_BATCH * DEC_SEQ
N_TOK = N_CTX + N_LAT
MOD_ROWS = 8

COL_HY, COL_QA, COL_KA, COL_VA = 0, 768, 1024, 1152
COL_QC, COL_KC, COL_VC = 1280, 1536, 1664
COL_QD, COL_KD, COL_VD, COL_RD, COL_GD = 1792, 1920, 2048, 2304, 2560
PROJ_WIDTH = 2688
_REF_COLS = dict(qa=0, ka=256, va=384, hy=512, qc=1280, kc=1536, vc=1664, qd=1792, kd=1920, vd=2048,
                 gd=2304, rd=2336)
HEAD_ORDER = (0, 2, 1, 3)
_HEAD_PERM = np.concatenate([np.arange(HEAD_DIM) + HEAD_DIM * h for h in HEAD_ORDER])

TM_FFN = 1024
TK_FFN = 256
TM_PROJ = 512
TQ_ATTN = 256
ROWS_GROUP = 256
NB_CTX = 2


def _bdot(a, b):
    return jnp.dot(a.astype(BF16), b.astype(BF16), preferred_element_type=F32)


def _bdot_nt(a, b):
    return lax.dot_general(a.astype(BF16), b.astype(BF16), (((1,), (1,)), ((), ())),
                           preferred_element_type=F32)


def _bdot_tn(a, b):
    return lax.dot_general(a.astype(BF16), b.astype(BF16), (((0,), (0,)), ((), ())),
                           preferred_element_type=F32)


def _fdot(a, b):
    return jnp.dot(a, b, preferred_element_type=F32, precision=lax.Precision.HIGHEST)


def _split_dot(m, x):
    hi = x.astype(BF16)
    lo = (x - hi.astype(F32)).astype(BF16)
    return jnp.dot(m, hi, preferred_element_type=F32) + jnp.dot(m, lo, preferred_element_type=F32)


def _rms(x, g):
    return x * lax.rsqrt(jnp.mean(x * x, axis=-1, keepdims=True) + EPS) * g


def _silu(x):
    return x * jax.nn.sigmoid(x)


def _mod_row(i, tm):
    n_ctx_tiles = N_CTX // tm
    per_batch = DEC_SEQ // tm
    return jnp.where(i < n_ctx_tiles, 0, 1 + (i - n_ctx_tiles) // per_batch)


def _mod_spec(tm, j, grid_rank):
    if grid_rank == 1:
        return pl.BlockSpec((None, None, 1, D_MODEL), lambda i: (_mod_row(i, tm), j, 0, 0))
    return pl.BlockSpec((None, None, 1, D_MODEL), lambda i, k: (_mod_row(i, tm), j, 0, 0))


def _any_spec():
    return pl.BlockSpec(memory_space=pl.ANY)


def _mod_kernel(cond_ref, w_ref, b_ref, o_ref):
    o_ref[...] = _bdot(_silu(cond_ref[...]), w_ref[...]) + b_ref[...]


def _modulation(cond, w_mod, b_mod):
    tn = D_MODEL
    return pl.pallas_call(
        _mod_kernel,
        grid=(DEPTH, N_MOD * D_MODEL // tn),
        in_specs=[
            pl.BlockSpec((MOD_ROWS, D_MODEL), lambda l, j: (0, 0)),
            pl.BlockSpec((None, D_MODEL, tn), lambda l, j: (l, 0, j)),
            pl.BlockSpec((None, 1, tn), lambda l, j: (l, 0, j)),
        ],
        out_specs=pl.BlockSpec((None, MOD_ROWS, tn), lambda l, j: (l, 0, j)),
        out_shape=jax.ShapeDtypeStruct((DEPTH, MOD_ROWS, N_MOD * D_MODEL), F32),
        name="modulation",
    )(cond, w_mod, b_mod.reshape(DEPTH, 1, N_MOD * D_MODEL))


def _ffn_kernel(x_ref, sh_ref, sc_ref, gt_ref, ng_ref, wa_ref, wb_ref, wo_ref, o_ref, xm_ref, acc_ref):
    k = pl.program_id(1)

    @pl.when(k == 0)
    def _():
        xn = _rms(x_ref[...], ng_ref[...])
        xm_ref[...] = (xn * (1.0 + sc_ref[...]) + sh_ref[...]).astype(BF16)
        acc_ref[...] = jnp.zeros_like(acc_ref)

    xm = xm_ref[...]
    a = jnp.dot(xm, wa_ref[...], preferred_element_type=F32)
    b = jnp.dot(xm, wb_ref[...], preferred_element_type=F32)
    acc_ref[...] += _bdot(_silu(a) * b, wo_ref[...])

    @pl.when(k == pl.num_programs(1) - 1)
    def _():
        o_ref[...] = x_ref[...] + 0.5 * gt_ref[...] * acc_ref[...]


def _ffn(h, mod, j0, ng, w_in, w_out):
    tm, tk = TM_FFN, TK_FFN
    nk = D_FF // tk
    return pl.pallas_call(
        _ffn_kernel,
        grid=(N_TOK // tm, nk),
        in_specs=[
            pl.BlockSpec((tm, D_MODEL), lambda i, k: (i, 0)),
            _mod_spec(tm, j0, 2), _mod_spec(tm, j0 + 1, 2), _mod_spec(tm, j0 + 2, 2),
            pl.BlockSpec((1, D_MODEL), lambda i, k: (0, 0)),
            pl.BlockSpec((D_MODEL, tk), lambda i, k: (0, k)),
            pl.BlockSpec((D_MODEL, tk), lambda i, k: (0, k + nk)),
            pl.BlockSpec((tk, D_MODEL), lambda i, k: (k, 0)),
        ],
        out_specs=pl.BlockSpec((tm, D_MODEL), lambda i, k: (i, 0)),
        out_shape=jax.ShapeDtypeStruct((N_TOK, D_MODEL), F32),
        scratch_shapes=[pltpu.VMEM((tm, D_MODEL), BF16), pltpu.VMEM((tm, D_MODEL), F32)],
        compiler_params=pltpu.CompilerParams(dimension_semantics=("parallel", "arbitrary")),
        name="ffn",
    )(h, mod, mod, mod, ng.reshape(1, D_MODEL), w_in, w_in, w_out)


def _proj_kernel(x_ref, sh_ref, sc_ref, ng_ref, w_ref, o_ref):
    xn = _rms(x_ref[...], ng_ref[...])
    o_ref[...] = _bdot(xn * (1.0 + sc_ref[...]) + sh_ref[...], w_ref[...])


def _in_proj(h, mod, ng, w):
    tm = TM_PROJ
    return pl.pallas_call(
        _proj_kernel,
        grid=(N_TOK // tm,),
        in_specs=[
            pl.BlockSpec((tm, D_MODEL), lambda i: (i, 0)),
            _mod_spec(tm, 3, 1), _mod_spec(tm, 4, 1),
            pl.BlockSpec((1, D_MODEL), lambda i: (0, 0)),
            pl.BlockSpec((D_MODEL, PROJ_WIDTH), lambda i: (0, 0)),
        ],
        out_specs=pl.BlockSpec((tm, PROJ_WIDTH), lambda i: (i, 0)),
        out_shape=jax.ShapeDtypeStruct((N_TOK, PROJ_WIDTH), F32),
        compiler_params=pltpu.CompilerParams(dimension_semantics=("parallel",)),
        name="in_proj",
    )(h, mod, mod, ng.reshape(1, D_MODEL), w)


def _rope_tables(length):
    rows = length // GRID_W
    r = np.repeat(np.arange(rows, dtype=np.float32), GRID_W)
    col = np.tile(np.arange(GRID_W, dtype=np.float32), rows)
    nf = HEAD_DIM // 4
    inv = (np.float32(ROPE_THETA) ** (-np.arange(nf, dtype=np.float32) / nf)).astype(np.float32)
    ang = np.concatenate([r[:, None] * inv, col[:, None] * inv], axis=-1).astype(np.float32)
    cos, sin = np.cos(ang).astype(np.float32), np.sin(ang).astype(np.float32)
    return np.tile(cos, (1, 4)), np.tile(np.concatenate([-sin, sin], axis=-1), (1, 2))


def _pair_lanes(rows):
    lane = lax.broadcasted_iota(jnp.int32, (rows, 2 * HEAD_DIM), 1)
    return lane < HEAD_DIM, (lane % HEAD_DIM) < HEAD_DIM // 2


def _rope_pair(x, cos, sin_signed, first_half):
    partner = jnp.where(first_half, pltpu.roll(x, 3 * HEAD_DIM // 2, 1), pltpu.roll(x, HEAD_DIM // 2, 1))
    return x * cos + partner * sin_signed


def _rms_pair(x, g, lo):
    x2 = x * x
    s_lo = jnp.sum(jnp.where(lo, x2, 0.0), axis=-1, keepdims=True)
    s_hi = jnp.sum(jnp.where(lo, 0.0, x2), axis=-1, keepdims=True)
    ms = jnp.where(lo, s_lo, s_hi) * (1.0 / HEAD_DIM)
    return x * lax.rsqrt(ms + EPS) * g


def _attn_core(q, k_bf, v_bf, lo, *, mask=None, ctx=None, sinks=None):
    tq = q.shape[0]
    pair = 2 * HEAD_DIM
    scale = HEAD_DIM ** -0.5
    qa, qb = q[:, :pair] * scale, q[:, pair:] * scale
    qs = jnp.concatenate([jnp.where(lo, qa, 0.0), jnp.where(lo, 0.0, qa),
                          jnp.where(lo, qb, 0.0), jnp.where(lo, 0.0, qb)], axis=0).astype(BF16)
    s = _bdot_nt(qs, k_bf)
    if ctx is not None:
        s_c = _bdot_nt(qs, ctx[0])
    ps, pcs, dens = [], [], []
    for j in range(N_HEADS):
        rows = slice(j * tq, (j + 1) * tq)
        sj = s[rows]
        if mask is not None:
            sj = jnp.where(mask, sj, -1e30)
        m = jnp.max(sj, axis=-1, keepdims=True)
        if ctx is not None:
            m = jnp.maximum(m, jnp.max(s_c[rows], axis=-1, keepdims=True))
        if sinks is not None:
            m = jnp.maximum(m, sinks[j])
        p = jnp.exp(sj - m)
        den = jnp.sum(p, axis=-1, keepdims=True)
        ps.append(p.astype(BF16))
        if ctx is not None:
            pc = jnp.exp(s_c[rows] - m)
            den = den + jnp.sum(pc, axis=-1, keepdims=True)
            pcs.append(pc.astype(BF16))
        if sinks is not None:
            den = den + jnp.exp(sinks[j] - m)
        dens.append(den)
    r = jnp.dot(jnp.concatenate(ps, axis=0), v_bf, preferred_element_type=F32)
    if ctx is not None:
        r = r + jnp.dot(jnp.concatenate(pcs, axis=0), ctx[1], preferred_element_type=F32)
    o = [r[j * tq:(j + 1) * tq] / dens[j] for j in range(N_HEADS)]
    return jnp.concatenate([jnp.where(lo, o[0], o[1]), jnp.where(lo, o[2], o[3])], axis=1)


def _attn_ctx_kernel(names, *refs):
    r = dict(zip(names, refs))
    lo, _ = _pair_lanes(SEQ)
    sinks = [r['sink'][h] for h in HEAD_ORDER] if 'sink' in r else None
    for bb in range(NB_CTX):
        rows = slice(bb * SEQ, (bb + 1) * SEQ)
        q, k, v = r['q'][rows, :], r['k'][rows, :], r['v'][rows, :]
        if 'qkg' in r:
            gq, gk = r['qkg'][0:1, :], r['qkg'][1:2, :]
            q = jnp.concatenate([_rms_pair(q[:, :KV_WIDTH], gq, lo), _rms_pair(q[:, KV_WIDTH:], gq, lo)], axis=1)
            k = _rms_pair(k, gk, lo)
        r['k_out'][bb] = k
        r['v_out'][bb] = v
        r['o'][rows, :] = _attn_core(q, k.astype(BF16), v.astype(BF16), lo, sinks=sinks)


def _attn_lat_kernel(names, window, *refs):
    r = dict(zip(names, refs))
    tq = TQ_ATTN
    i = pl.program_id(1)
    lo, first_half = _pair_lanes(tq)
    qk_norm = 'qkg' in r

    @pl.when(i == 0)
    def _():
        lo_k, first_half_k = _pair_lanes(DEC_SEQ)
        k = r['k'][...]
        if qk_norm:
            k = _rms_pair(k, r['qkg'][1:2, :], lo_k)
        r['k_scr'][...] = _rope_pair(k, r['cos_k'][...], r['sin_k'][...], first_half_k).astype(BF16)

    q = r['q'][...]
    halves = []
    for c0 in (0, KV_WIDTH):
        x = q[:, c0:c0 + KV_WIDTH]
        if qk_norm:
            x = _rms_pair(x, r['qkg'][0:1, :], lo)
        halves.append(_rope_pair(x, r['cos_q'][...], r['sin_q'][...], first_half))
    q = jnp.concatenate(halves, axis=1)
    sinks = [r['sink'][h] for h in HEAD_ORDER] if 'sink' in r else None
    ctx = (r['kc'][...].astype(BF16), r['vc'][...].astype(BF16))
    if window:
        span = tq + 2 * WINDOW
        start = pl.multiple_of(jnp.clip(i * tq - WINDOW, 0, DEC_SEQ - span), WINDOW)
        qpos = i * tq + lax.broadcasted_iota(jnp.int32, (tq, span), 0)
        kpos = start + lax.broadcasted_iota(jnp.int32, (tq, span), 1)
        mask = jnp.abs(qpos - kpos) <= WINDOW
        k_bf = r['k_scr'][pl.ds(start, span), :]
        v_bf = r['v'][pl.ds(start, span), :].astype(BF16)
    else:
        mask = None
        k_bf = r['k_scr'][...]
        v_bf = r['v'][...].astype(BF16)
    r['o'][...] = _attn_core(q, k_bf, v_bf, lo, mask=mask, ctx=ctx, sinks=sinks)


def _attention_context(u, layer, cache_prev, *, col_q, col_k, col_v, sink=None, qkg2=None):
    rows = NB_CTX * SEQ
    qb, kb, vb = col_q // GROUP_WIDTH, col_k // KV_WIDTH, col_v // KV_WIDTH
    names = ['q', 'k', 'v']
    args = [u, u, u]
    in_specs = [pl.BlockSpec((rows, GROUP_WIDTH), lambda b: (b, qb)),
                pl.BlockSpec((rows, KV_WIDTH), lambda b: (b, kb)),
                pl.BlockSpec((rows, KV_WIDTH), lambda b: (b, vb))]
    if sink is not None:
        names.append('sink'); args.append(sink)
        in_specs.append(pl.BlockSpec(memory_space=pltpu.SMEM))
    if qkg2 is not None:
        names.append('qkg'); args.append(qkg2)
        in_specs.append(pl.BlockSpec((2, KV_WIDTH), lambda b: (0, 0)))
    aliases = {}
    if cache_prev is not None:
        for j, prev in enumerate(cache_prev):
            aliases[len(args)] = 1 + j
            names.append(f'prev{j}'); args.append(prev); in_specs.append(_any_spec())
    names += ['o', 'k_out', 'v_out']
    cache_shape = jax.ShapeDtypeStruct((BATCH, DEPTH, SEQ, KV_WIDTH), F32)
    cache_spec = pl.BlockSpec((NB_CTX, None, SEQ, KV_WIDTH), lambda b: (b, layer, 0, 0))
    return pl.pallas_call(
        functools.partial(_attn_ctx_kernel, tuple(names)),
        grid=(BATCH // NB_CTX,), in_specs=in_specs,
        out_specs=(pl.BlockSpec((rows, GROUP_WIDTH), lambda b: (b, 0)), cache_spec, cache_spec),
        out_shape=(jax.ShapeDtypeStruct((N_TOK, GROUP_WIDTH), F32), cache_shape, cache_shape),
        input_output_aliases=aliases,
        compiler_params=pltpu.CompilerParams(dimension_semantics=("parallel",)),
        name="attn_context",
    )(*args)


def _attention_latent(u, prev, *, col_q, col_k, col_v, cache_k, cache_v, window=False, sink=None, qkg2=None):
    tq = TQ_ATTN
    nq = DEC_SEQ // tq
    row0_q, row0_k = N_CTX // tq, N_CTX // DEC_SEQ
    qb, kb, vb = col_q // GROUP_WIDTH, col_k // KV_WIDTH, col_v // KV_WIDTH
    cos, sin = _rope_tables(DEC_SEQ)
    const = lambda b, i: (0, 0)
    names = ['q', 'k', 'v', 'kc', 'vc', 'cos_q', 'sin_q', 'cos_k', 'sin_k']
    args = [u, u, u, cache_k, cache_v, cos, sin, cos, sin]
    in_specs = [
        pl.BlockSpec((tq, GROUP_WIDTH), lambda b, i: (row0_q + b * nq + i, qb)),
        pl.BlockSpec((DEC_SEQ, KV_WIDTH), lambda b, i: (row0_k + b, kb)),
        pl.BlockSpec((DEC_SEQ, KV_WIDTH), lambda b, i: (row0_k + b, vb)),
        pl.BlockSpec((None, PAST_LEN, KV_WIDTH), lambda b, i: (b, 0, 0)),
        pl.BlockSpec((None, PAST_LEN, KV_WIDTH), lambda b, i: (b, 0, 0)),
        pl.BlockSpec((tq, KV_WIDTH), lambda b, i: (i, 0)),
        pl.BlockSpec((tq, KV_WIDTH), lambda b, i: (i, 0)),
        pl.BlockSpec((DEC_SEQ, KV_WIDTH), const),
        pl.BlockSpec((DEC_SEQ, KV_WIDTH), const),
    ]
    if sink is not None:
        names.append('sink'); args.append(sink)
        in_specs.append(pl.BlockSpec(memory_space=pltpu.SMEM))
    if qkg2 is not None:
        names.append('qkg'); args.append(qkg2)
        in_specs.append(pl.BlockSpec((2, KV_WIDTH), const))
    aliases = {len(args): 0}
    names.append('prev'); args.append(prev); in_specs.append(_any_spec())
    names += ['o', 'k_scr']
    return pl.pallas_call(
        functools.partial(_attn_lat_kernel, tuple(names), window),
        grid=(DEC_BATCH, nq), in_specs=in_specs,
        out_specs=pl.BlockSpec((tq, GROUP_WIDTH), lambda b, i: (row0_q + b * nq + i, 0)),
        out_shape=jax.ShapeDtypeStruct((N_TOK, GROUP_WIDTH), F32),
        scratch_shapes=[pltpu.VMEM((DEC_SEQ, KV_WIDTH), BF16)],
        input_output_aliases=aliases,
        compiler_params=pltpu.CompilerParams(dimension_semantics=("parallel", "arbitrary")),
        name="attn_latent",
    )(*args)


def _dft_matrices(length):
    n = length
    k = np.arange(n, dtype=np.int64)[:, None]
    s = np.arange(n, dtype=np.int64)[None, :]
    ang = np.pi * ((k * s) % (2 * n)).astype(np.float64) / n
    fwd_cos = np.cos(ang)
    fwd_sin = -np.sin(ang)
    fwd_sin[0, :] = 1.0 - 2.0 * (np.arange(n) % 2)
    fwd = np.concatenate([fwd_cos, fwd_sin], axis=0)
    wk = np.full((n,), 2.0)
    wk[0] = 1.0
    inv_cos = (np.cos(ang) * wk[:, None]).T / (2 * n)
    inv_sin = (-2.0 * np.sin(ang)).T / (2 * n)
    inv_sin[:, 0] = (1.0 - 2.0 * (np.arange(n) % 2)) / (2 * n)
    inv = np.concatenate([inv_cos, inv_sin], axis=1)
    return fwd.astype(np.float32), inv.astype(np.float32)


def _filter_features(length):
    t = np.linspace(0.0, 1.0, length, dtype=np.float32)[:, None]
    w = (np.float32(2.0 * math.pi / length) * np.arange(length, dtype=np.float32))[:, None]
    bands = np.linspace(1e-4, HY_BANDS - 1, HY_BANDS, dtype=np.float32)[None, :]
    z = np.concatenate([t, np.cos(bands * w), -np.sin(bands * w)], axis=-1).astype(np.float32)
    zp = np.zeros((length, 128), np.float32)
    zp[:, :HY_EMB] = z
    deltas = np.linspace(math.log(HY_TARGET) / HY_SLOW, math.log(HY_TARGET) / HY_FAST, HY_CH, dtype=np.float32)
    decay = np.exp(-t * np.abs(deltas)).astype(np.float32)
    return zp, decay


def _filter_kernel(z_ref, w1_ref, b1_ref, w2_ref, b2_ref, w3_ref, fr_ref, dec_ref, f_ref, kr_ref, ki_ref):
    n = z_ref.shape[0]
    h = jnp.sin(fr_ref[0:1, :] * (_fdot(z_ref[...], w1_ref[...]) + b1_ref[...]))
    h = jnp.sin(fr_ref[1:2, :] * (_fdot(h, w2_ref[...]) + b2_ref[...]))
    h = _fdot(h, w3_ref[...])
    dec = dec_ref[...]
    half = HY_ORDER * HY_CH
    pos = jnp.concatenate([h[:, j * HY_CH:(j + 1) * HY_CH] * dec for j in range(HY_ORDER)], axis=-1)
    neg = jnp.concatenate([h[:, half + j * HY_CH:half + (j + 1) * HY_CH] * dec for j in range(HY_ORDER)], axis=-1)
    row = lax.broadcasted_iota(jnp.int32, (n, half), 0)
    neg = jnp.where(row == 0, 0.0, neg)
    fwd = f_ref[...]
    a = jnp.dot(fwd, (pos + neg).astype(BF16), preferred_element_type=F32)
    b = jnp.dot(fwd, (pos - neg).astype(BF16), preferred_element_type=F32)
    kr_ref[...] = a[:n]
    ki_ref[...] = jnp.where(row == 0, a[n:], b[n:])


def _hyena_filters(length, fwd, w1p, b1, w2, b2, w3, freq):
    zp, decay = _filter_features(length)
    half = HY_ORDER * HY_CH
    return pl.pallas_call(
        _filter_kernel,
        out_shape=(jax.ShapeDtypeStruct((length, half), F32), jax.ShapeDtypeStruct((length, half), F32)),
        name="hyena_filters",
    )(zp, w1p, b1.reshape(1, HY_FH), w2, b2.reshape(1, HY_FH), w3, freq, decay, fwd)


def _hyena_kernel(z_ref, cw_ref, cb_ref, f_ref, g_ref, kr_ref, ki_ref, bias_ref, o_ref):
    n = z_ref.shape[0]
    z = z_ref[...]
    row = lax.broadcasted_iota(jnp.int32, z.shape, 0)
    prev = jnp.where(row == 0, 0.0, pltpu.roll(z, 1, 0))
    nxt = jnp.where(row == n - 1, 0.0, pltpu.roll(z, n - 1, 0))
    z = prev * cw_ref[0:1, :] + z * cw_ref[1:2, :] + nxt * cw_ref[2:3, :] + cb_ref[...]
    v, x1, x2 = z[:, :HY_CH], z[:, HY_CH:2 * HY_CH], z[:, 2 * HY_CH:]
    row0 = lax.broadcasted_iota(jnp.int32, (n, HY_CH), 0) == 0

    def long_conv(x, order):
        cols = slice(order * HY_CH, (order + 1) * HY_CH)
        kr, ki = kr_ref[:, cols], ki_ref[:, cols]
        spec = jnp.dot(f_ref[...], x.astype(BF16), preferred_element_type=F32)
        ur, ui = spec[:n], spec[n:]
        yr = ur * kr - jnp.where(row0, 0.0, ui * ki)
        yi = jnp.where(row0, ui * ki, ur * ki + ui * kr)
        y = jnp.dot(g_ref[...], jnp.concatenate([yr, yi], axis=0).astype(BF16), preferred_element_type=F32)
        return y + x * bias_ref[order:order + 1, :]

    y = x1 * long_conv(v, 0)
    o_ref[...] = x2 * long_conv(y, 1)


def _hyena(u, prev, *, latent, fwd, inv, kr, ki, conv_w, conv_b, bias):
    n = DEC_SEQ if latent else SEQ
    nb = DEC_BATCH if latent else BATCH
    row0 = N_CTX // n if latent else 0
    half = HY_ORDER * HY_CH
    const = lambda b: (0, 0)
    args = [u, conv_w, conv_b.reshape(1, 3 * HY_CH), fwd, inv, kr, ki, bias]
    in_specs = [
        pl.BlockSpec((n, 3 * HY_CH), lambda b: (row0 + b, COL_HY // (3 * HY_CH))),
        pl.BlockSpec((3, 3 * HY_CH), const),
        pl.BlockSpec((1, 3 * HY_CH), const),
        pl.BlockSpec((2 * n, n), const),
        pl.BlockSpec((n, 2 * n), const),
        pl.BlockSpec((n, half), const),
        pl.BlockSpec((n, half), const),
        pl.BlockSpec((HY_ORDER, HY_CH), const),
    ]
    aliases = {}
    if latent:
        aliases = {len(args): 0}
        args.append(prev)
        in_specs.append(_any_spec())

    def body(*refs):
        if latent:
            refs = refs[:len(args) - 1] + refs[len(args):]
        _hyena_kernel(*refs)

    return pl.pallas_call(
        body, grid=(nb,), in_specs=in_specs,
        out_specs=pl.BlockSpec((n, HY_CH), lambda b: (row0 + b, 0)),
        out_shape=jax.ShapeDtypeStruct((N_TOK, HY_CH), F32),
        input_output_aliases=aliases,
        compiler_params=pltpu.CompilerParams(dimension_semantics=("parallel",)),
        name="hyena_latent" if latent else "hyena_context",
    )(*args)


def _gla_masks():
    t = np.arange(ROWS_GROUP)[:, None]
    s = np.arange(ROWS_GROUP)[None, :]
    same = (t // GLA_CHUNK) == (s // GLA_CHUNK)
    return np.stack([same & (s <= t), same & (s >= t), same]).astype(np.float32)


def _gla_group(r, rows, z, st):
    c = GLA_CHUNK
    hk = H_GLA * DK_GLA
    n_chunks = ROWS_GROUP // c
    logits = _bdot(r['gl'][rows, :], r['gw'][z]) + r['gb'][z:z + 1, :]
    g = (jnp.minimum(logits, 0.0) - jnp.log1p(jnp.exp(-jnp.abs(logits)))) / GLA_NORM
    causal = r['masks'][z]
    b = _split_dot(causal.astype(BF16), g)
    tot = _split_dot(r['masks'][2].astype(BF16), g)
    k = r['k'][rows, :]
    q_t = r['q'][rows, :] * jnp.exp(b) * (DK_GLA ** -0.5)
    k_t = (k * jnp.exp(-b)).astype(BF16)
    k_e = (k * jnp.exp(tot - b)).astype(BF16)
    dec = jnp.exp(tot)
    v = r['v'][rows, :].astype(BF16)

    qlane = lax.broadcasted_iota(jnp.int32, (ROWS_GROUP, hk), 1) // DK_GLA
    vlane = lax.broadcasted_iota(jnp.int32, (ROWS_GROUP, GROUP_WIDTH), 1) // DV_GLA
    qs = jnp.concatenate([jnp.where(qlane == h, q_t, 0.0) for h in range(H_GLA)], axis=0).astype(BF16)
    s = _bdot_nt(qs, k_t)
    valid = causal != 0.0
    p = jnp.concatenate([jnp.where(valid, s[h * ROWS_GROUP:(h + 1) * ROWS_GROUP], 0.0).astype(BF16)
                         for h in range(H_GLA)], axis=0)
    res = jnp.dot(p, v, preferred_element_type=F32)
    o_intra = jnp.where(vlane == 0, res[0:ROWS_GROUP], 0.0)
    for h in range(1, H_GLA):
        o_intra = jnp.where(vlane == h, res[h * ROWS_GROUP:(h + 1) * ROWS_GROUP], o_intra)

    bd = (lax.broadcasted_iota(jnp.int32, (GROUP_WIDTH, hk), 0) // DV_GLA
          == lax.broadcasted_iota(jnp.int32, (GROUP_WIDTH, hk), 1) // DK_GLA)
    q_bf = q_t.astype(BF16)
    o_inter = [None] * n_chunks
    for step in range(n_chunks):
        ci = step if z == 0 else n_chunks - 1 - step
        cr = slice(ci * c, (ci + 1) * c)
        o_inter[ci] = _bdot_nt(q_bf[cr], st)
        st = st * dec[ci * c:ci * c + 1, :] + jnp.where(bd, _bdot_tn(v[cr], k_e[cr]), 0.0)
    return o_intra + jnp.concatenate(o_inter, axis=0), st


def _head_rms(o, g):
    hb = (lax.broadcasted_iota(jnp.int32, (GROUP_WIDTH, GROUP_WIDTH), 0) // DV_GLA
          == lax.broadcasted_iota(jnp.int32, (GROUP_WIDTH, GROUP_WIDTH), 1) // DV_GLA)
    ms = _split_dot_rhs(o * o, hb.astype(BF16)) * (1.0 / DV_GLA)
    return o * lax.rsqrt(ms + EPS) * g


def _split_dot_rhs(x, m):
    hi = x.astype(BF16)
    lo = (x - hi.astype(F32)).astype(BF16)
    return jnp.dot(hi, m, preferred_element_type=F32) + jnp.dot(lo, m, preferred_element_type=F32)


def _expand_state(s):
    hk = H_GLA * DK_GLA
    bd = (lax.broadcasted_iota(jnp.int32, (GROUP_WIDTH, hk), 0) // DV_GLA
          == lax.broadcasted_iota(jnp.int32, (GROUP_WIDTH, hk), 1) // DK_GLA)
    return jnp.where(bd, jnp.concatenate([s] * H_GLA, axis=0), 0.0)


def _compact_state(st):
    out = st[0:DV_GLA]
    for h in range(1, H_GLA):
        out = out + st[h * DV_GLA:(h + 1) * DV_GLA]
    return out


def _gla_ctx_kernel(names, *refs):
    r = dict(zip(names, refs))
    hk = H_GLA * DK_GLA
    for bb in range(NB_CTX):
        rows = slice(bb * SEQ, (bb + 1) * SEQ)
        zero = jnp.zeros((GROUP_WIDTH, hk), F32)
        o_f, st_f = _gla_group(r, rows, 0, zero)
        o_b, st_b = _gla_group(r, rows, 1, zero)
        r['o'][rows, :] = _head_rms(o_f + o_b, r['mg'][...])
        r['sf'][bb, 0] = _compact_state(st_f)
        r['sf'][bb, 1] = _compact_state(st_b)


def _gla_lat_kernel(names, *refs):
    r = dict(zip(names, refs))
    n_groups = DEC_SEQ // ROWS_GROUP

    def run(z):
        r['st'][...] = _expand_state(r['s0'][z])

        def body(step, carry):
            gi = step if z == 0 else n_groups - 1 - step
            rows = pl.ds(pl.multiple_of(gi * ROWS_GROUP, ROWS_GROUP), ROWS_GROUP)
            o, st = _gla_group(r, rows, z, r['st'][...])
            r['st'][...] = st
            if z == 0:
                r['o'][rows, :] = o
            else:
                r['o'][rows, :] = _head_rms(r['o'][rows, :] + o, r['mg'][...])
            return carry

        lax.fori_loop(0, n_groups, body, 0)

    run(0)
    run(1)


def _gla(u, prev, *, latent, gate_w, gate_b, mix_g_d, s0=None):
    n = DEC_SEQ if latent else NB_CTX * SEQ
    steps = DEC_BATCH if latent else BATCH // NB_CTX
    row0 = N_CTX // n if latent else 0
    hk = H_GLA * DK_GLA
    const2 = lambda b: (0, 0)
    names = ['q', 'k', 'v', 'gl', 'gw', 'gb', 'mg', 'masks']
    args = [u, u, u, u, gate_w, gate_b, mix_g_d, _gla_masks()]
    in_specs = [
        pl.BlockSpec((n, hk), lambda b: (row0 + b, COL_QD // hk)),
        pl.BlockSpec((n, hk), lambda b: (row0 + b, COL_KD // hk)),
        pl.BlockSpec((n, GROUP_WIDTH), lambda b: (row0 + b, COL_VD // GROUP_WIDTH)),
        pl.BlockSpec((n, 128), lambda b: (row0 + b, COL_GD // 128)),
        pl.BlockSpec((2, 128, hk), lambda b: (0, 0, 0)),
        pl.BlockSpec((2, hk), const2),
        pl.BlockSpec((1, GROUP_WIDTH), const2),
        pl.BlockSpec((3, ROWS_GROUP, ROWS_GROUP), lambda b: (0, 0, 0)),
    ]
    out_shape = jax.ShapeDtypeStruct((N_TOK, GROUP_WIDTH), F32)
    out_specs = pl.BlockSpec((n, GROUP_WIDTH), lambda b: (row0 + b, 0))
    aliases, scratch = {}, []
    if latent:
        names.append('s0'); args.append(s0)
        in_specs.append(pl.BlockSpec((None, 2, DV_GLA, hk), lambda b: (b, 0, 0, 0)))
        aliases = {len(args): 0}
        names.append('prev'); args.append(prev); in_specs.append(_any_spec())
        names += ['o', 'st']
        scratch = [pltpu.VMEM((GROUP_WIDTH, hk), F32)]
        body = functools.partial(_gla_lat_kernel, tuple(names))
    else:
        names += ['o', 'sf']
        out_shape = (out_shape, jax.ShapeDtypeStruct((BATCH, 2, DV_GLA, hk), F32))
        out_specs = (out_specs, pl.BlockSpec((NB_CTX, 2, DV_GLA, hk), lambda b: (b, 0, 0, 0)))
        body = functools.partial(_gla_ctx_kernel, tuple(names))
    return pl.pallas_call(
        body, grid=(steps,), in_specs=in_specs, out_specs=out_specs, out_shape=out_shape,
        scratch_shapes=scratch, input_output_aliases=aliases,
        compiler_params=pltpu.CompilerParams(dimension_semantics=("parallel",)),
        name="gla_latent" if latent else "gla_context",
    )(*args)


def _merge_kernel(h_ref, oa_ref, ob_ref, oc_ref, od_ref, rd_ref, gt_ref, mg_ref, wo_ref, o_ref):
    gw = GROUP_WIDTH
    ys = [
        _rms(oa_ref[...], mg_ref[:, 0:gw]),
        _rms(ob_ref[...], mg_ref[:, gw:2 * gw]),
        _rms(oc_ref[...], mg_ref[:, 2 * gw:3 * gw]),
        od_ref[...] * _silu(rd_ref[...]),
    ]
    mix = _bdot(ys[0], wo_ref[0:gw, :])
    for j in range(1, 4):
        mix = mix + _bdot(ys[j], wo_ref[j * gw:(j + 1) * gw, :])
    o_ref[...] = h_ref[...] + gt_ref[...] * mix


def _merge(h, oa, ob, oc, od, u, mod, mix_g, w_out):
    tm = TM_PROJ
    grp = pl.BlockSpec((tm, GROUP_WIDTH), lambda i: (i, 0))
    return pl.pallas_call(
        _merge_kernel,
        grid=(N_TOK // tm,),
        in_specs=[
            pl.BlockSpec((tm, D_MODEL), lambda i: (i, 0)),
            grp, grp, grp, grp,
            pl.BlockSpec((tm, GROUP_WIDTH), lambda i: (i, COL_RD // GROUP_WIDTH)),
            _mod_spec(tm, 5, 1),
            pl.BlockSpec((1, D_MODEL), lambda i: (0, 0)),
            pl.BlockSpec((D_MODEL, D_MODEL), lambda i: (0, 0)),
        ],
        out_specs=pl.BlockSpec((tm, D_MODEL), lambda i: (i, 0)),
        out_shape=jax.ShapeDtypeStruct((N_TOK, D_MODEL), F32),
        compiler_params=pltpu.CompilerParams(dimension_semantics=("parallel",)),
        name="merge",
    )(h, oa, ob, oc, od, u, mod, mix_g.reshape(1, D_MODEL), w_out)


def _norm_kernel(x_ref, g_ref, o_ref):
    o_ref[...] = _rms(x_ref[...], g_ref[...])


def _final_norm(h, g, row0, rows):
    tm = TM_PROJ
    return pl.pallas_call(
        _norm_kernel,
        grid=(rows // tm,),
        in_specs=[pl.BlockSpec((tm, D_MODEL), lambda i: (row0 // tm + i, 0)),
                  pl.BlockSpec((1, D_MODEL), lambda i: (0, 0))],
        out_specs=pl.BlockSpec((tm, D_MODEL), lambda i: (i, 0)),
        out_shape=jax.ShapeDtypeStruct((rows, D_MODEL), F32),
        compiler_params=pltpu.CompilerParams(dimension_semantics=("parallel",)),
        name="final_norm",
    )(h, g.reshape(1, D_MODEL))


def _permute_w_in(w):
    r = _REF_COLS
    parts = [w[:, r['hy']:r['hy'] + 768], w[:, r['qa'] + _HEAD_PERM], w[:, r['ka']:r['ka'] + 128],
             w[:, r['va']:r['va'] + 128], w[:, r['qc'] + _HEAD_PERM], w[:, r['kc']:r['kc'] + 128],
             w[:, r['vc']:r['vc'] + 128], w[:, r['qd']:r['qd'] + 128], w[:, r['kd']:r['kd'] + 128],
             w[:, r['vd']:r['vd'] + 256], w[:, r['rd']:r['rd'] + 256], w[:, r['gd']:r['gd'] + 32],
             jnp.zeros((D_MODEL, 128 - 2 * GLA_RANK), w.dtype)]
    return jnp.concatenate(parts, axis=1)


def _mix_perm():
    gw = GROUP_WIDTH
    return np.concatenate([_HEAD_PERM, gw + np.arange(gw), 2 * gw + _HEAD_PERM, 3 * gw + np.arange(gw)])


def _gate_weights(gate_w):
    out = jnp.zeros((2, 128, H_GLA * DK_GLA), gate_w.dtype)
    out = out.at[0, 0:GLA_RANK].set(gate_w[0])
    return out.at[1, GLA_RANK:2 * GLA_RANK].set(gate_w[1])


def _states_to_kernel(st):
    b = st.shape[0]
    return st.transpose(0, 1, 4, 2, 3).reshape(b, 2, DV_GLA, H_GLA * DK_GLA)


def _states_from_kernel(st):
    b = st.shape[0]
    return st.reshape(b, 2, DV_GLA, H_GLA, DK_GLA).transpose(0, 1, 3, 4, 2)


def kernel(x_prompt, x_sample, cache_swa_k, cache_swa_v, cache_gqa_k, cache_gqa_v, state_gla, c, c_ctx, w_mod, b_mod, norm_g, ffn_w_in, ffn_w_out, w_in, w_out, mix_g, swa_sink, qk_norm_g, hy_conv_w, hy_conv_b, hy_w1, hy_b1, hy_w2, hy_b2, hy_w3, hy_freq, hy_bias, gla_gate_w, gla_gate_b, final_g):
    cond = jnp.zeros((MOD_ROWS, D_MODEL), F32).at[0].set(c_ctx).at[1:1 + DEC_BATCH].set(c)
    mod_all = _modulation(cond, w_mod, b_mod).reshape(DEPTH, MOD_ROWS, N_MOD, 1, D_MODEL)

    dft = {n: tuple(jnp.asarray(m).astype(BF16) for m in _dft_matrices(n)) for n in (SEQ, DEC_SEQ)}
    mix_perm = _mix_perm()

    h = jnp.concatenate([x_prompt.reshape(N_CTX, D_MODEL), x_sample.reshape(N_LAT, D_MODEL)], axis=0)
    cache_a = cache_c = None
    sts = []
    for l in range(DEPTH):
        mod = mod_all[l]
        ffn_in = ffn_w_in[l].astype(BF16)
        ffn_out = ffn_w_out[l].astype(BF16)
        h = _ffn(h, mod, 0, norm_g[l, 0], ffn_in[0], ffn_out[0])
        u = _in_proj(h, mod, norm_g[l, 1], _permute_w_in(w_in[l]).astype(BF16))

        ck_a = cache_swa_k[:, l].reshape(DEC_BATCH, PAST_LEN, KV_WIDTH)
        cv_a = cache_swa_v[:, l].reshape(DEC_BATCH, PAST_LEN, KV_WIDTH)
        ck_c = cache_gqa_k[:, l].reshape(DEC_BATCH, PAST_LEN, KV_WIDTH)
        cv_c = cache_gqa_v[:, l].reshape(DEC_BATCH, PAST_LEN, KV_WIDTH)
        qkg2 = jnp.tile(qk_norm_g[l], (1, N_KV))

        oa, *cache_a = _attention_context(u, l, cache_a, col_q=COL_QA, col_k=COL_KA, col_v=COL_VA,
                                          sink=swa_sink[l])
        oa = _attention_latent(u, oa, col_q=COL_QA, col_k=COL_KA, col_v=COL_VA, cache_k=ck_a, cache_v=cv_a,
                               window=True, sink=swa_sink[l])
        oc, *cache_c = _attention_context(u, l, cache_c, col_q=COL_QC, col_k=COL_KC, col_v=COL_VC, qkg2=qkg2)
        oc = _attention_latent(u, oc, col_q=COL_QC, col_k=COL_KC, col_v=COL_VC, cache_k=ck_c, cache_v=cv_c,
                               qkg2=qkg2)

        w1p = jnp.zeros((128, HY_FH), F32).at[:HY_EMB].set(hy_w1[l])
        ob = None
        for latent, n in ((False, SEQ), (True, DEC_SEQ)):
            fwd, inv = dft[n]
            kr, ki = _hyena_filters(n, fwd, w1p, hy_b1[l], hy_w2[l], hy_b2[l], hy_w3[l], hy_freq[l])
            ob = _hyena(u, ob, latent=latent, fwd=fwd, inv=inv, kr=kr, ki=ki, conv_w=hy_conv_w[l],
                        conv_b=hy_conv_b[l], bias=hy_bias[l])

        gw = _gate_weights(gla_gate_w[l]).astype(BF16)
        mg_d = mix_g[l, 3 * GROUP_WIDTH:].reshape(1, GROUP_WIDTH)
        od, st = _gla(u, None, latent=False, gate_w=gw, gate_b=gla_gate_b[l], mix_g_d=mg_d)
        od = _gla(u, od, latent=True, gate_w=gw, gate_b=gla_gate_b[l], mix_g_d=mg_d,
                  s0=_states_to_kernel(state_gla[:, l]))

        h = _merge(h, oa, ob, oc, od, u, mod, mix_g[l][mix_perm], w_out[l][mix_perm].astype(BF16))
        h = _ffn(h, mod, 6, norm_g[l, 2], ffn_in[1], ffn_out[1])
        sts.append(_states_from_kernel(st))

    y_prompt = _final_norm(h, final_g, 0, N_CTX).reshape(BATCH, SEQ, D_MODEL)
    y_sample = _final_norm(h, final_g, N_CTX, N_LAT).reshape(DEC_BATCH, DEC_SEQ, D_MODEL)
    caches = [x.reshape(BATCH, DEPTH, SEQ, N_KV, HEAD_DIM) for x in (*cache_a, *cache_c)]
    return (y_prompt, y_sample, *caches, jnp.stack(sts, axis=1))
```

```python
import functools
import math

import numpy as np
import jax
import jax.numpy as jnp
from jax import lax
from jax.experimental import pallas as pl
from jax.experimental.pallas import tpu as pltpu

F32 = jnp.float32
BF16 = jnp.bfloat16

D_MODEL = 1024
BATCH = 32
SEQ = 256
DEPTH = 2
DEC_BATCH = 2
DEC_SEQ = 1024
PAST_LEN = 512
GRID_W = 64
HEAD_DIM = 64
GROUP_WIDTH = 256
N_HEADS = 4
N_KV = 2
KV_WIDTH = N_KV * HEAD_DIM
WINDOW = 128
HY_CH = 256
HY_ORDER = 2
HY_BANDS = 16
HY_EMB = 2 * HY_BANDS + 1
HY_FH = 64
HY_TARGET = 1e-2
HY_FAST = 0.3
HY_SLOW = 1.5
H_GLA = 4
DV_GLA = 64
DK_GLA = 32
GLA_RANK = 16
GLA_CHUNK = 64
GLA_NORM = 16.0
D_FF = 2816
N_MOD = 9
ROPE_THETA = 10000.0
EPS = 1e-6

N_CTX = BATCH * SEQ
N_LAT = DEC_BATCH * DEC_SEQ
N_TOK = N_CTX + N_LAT
MOD_ROWS = 8

COL_HY, COL_QA, COL_KA, COL_VA = 0, 768, 1024, 1152
COL_QC, COL_KC, COL_VC = 1280, 1536, 1664
COL_QD, COL_KD, COL_VD, COL_RD, COL_GD = 1792, 1920, 2048, 2304, 2560
PROJ_WIDTH = 2688
_REF_COLS = dict(qa=0, ka=256, va=384, hy=512, qc=1280, kc=1536, vc=1664, qd=1792, kd=1920, vd=2048,
                 gd=2304, rd=2336)
HEAD_ORDER = (0, 2, 1, 3)
_HEAD_PERM = np.concatenate([np.arange(HEAD_DIM) + HEAD_DIM * h for h in HEAD_ORDER])

TM_FFN = 1024
TK_FFN = 256
FFN_VMEM_BYTES = 60 * 1024 * 1024
TM_PROJ = 512
TQ_ATTN = 256
ROWS_GROUP = 256
NB_CTX = 2


def _bdot(a, b):
    return jnp.dot(a.astype(BF16), b.astype(BF16), preferred_element_type=F32)


def _bdot_nt(a, b):
    return lax.dot_general(a.astype(BF16), b.astype(BF16), (((1,), (1,)), ((), ())),
                           preferred_element_type=F32)


def _bdot_tn(a, b):
    return lax.dot_general(a.astype(BF16), b.astype(BF16), (((0,), (0,)), ((), ())),
                           preferred_element_type=F32)


def _fdot(a, b):
    return jnp.dot(a, b, preferred_element_type=F32, precision=lax.Precision.HIGHEST)


def _split_dot(m, x):
    hi = x.astype(BF16)
    lo = (x - hi.astype(F32)).astype(BF16)
    return jnp.dot(m, hi, preferred_element_type=F32) + jnp.dot(m, lo, preferred_element_type=F32)


def _rms(x, g):
    return x * lax.rsqrt(jnp.mean(x * x, axis=-1, keepdims=True) + EPS) * g


def _silu(x):
    return x * jax.nn.sigmoid(x)


def _mod_row(i, tm):
    n_ctx_tiles = N_CTX // tm
    per_batch = DEC_SEQ // tm
    return jnp.where(i < n_ctx_tiles, 0, 1 + (i - n_ctx_tiles) // per_batch)


def _mod_spec(tm, j, grid_rank):
    if grid_rank == 1:
        return pl.BlockSpec((None, None, 1, D_MODEL), lambda i: (_mod_row(i, tm), j, 0, 0))
    return pl.BlockSpec((None, None, 1, D_MODEL), lambda i, k: (_mod_row(i, tm), j, 0, 0))


def _any_spec():
    return pl.BlockSpec(memory_space=pl.ANY)


def _mod_kernel(cond_ref, w_ref, b_ref, o_ref):
    o_ref[...] = _bdot(_silu(cond_ref[...]), w_ref[...]) + b_ref[...]


def _modulation(cond, w_mod, b_mod):
    tn = D_MODEL
    return pl.pallas_call(
        _mod_kernel,
        grid=(DEPTH, N_MOD * D_MODEL // tn),
        in_specs=[
            pl.BlockSpec((MOD_ROWS, D_MODEL), lambda l, j: (0, 0)),
            pl.BlockSpec((None, D_MODEL, tn), lambda l, j: (l, 0, j)),
            pl.BlockSpec((None, 1, tn), lambda l, j: (l, 0, j)),
        ],
        out_specs=pl.BlockSpec((None, MOD_ROWS, tn), lambda l, j: (l, 0, j)),
        out_shape=jax.ShapeDtypeStruct((DEPTH, MOD_ROWS, N_MOD * D_MODEL), F32),
        name="modulation",
    )(cond, w_mod, b_mod.reshape(DEPTH, 1, N_MOD * D_MODEL))


def _ffn_kernel(split_input, *refs):
    if split_input:
        xa_ref, xb_ref, *refs = refs
        x = jnp.where(pl.program_id(0) < N_CTX // TM_FFN, xa_ref[...], xb_ref[...])
    else:
        x_ref, *refs = refs
        x = x_ref[...]
    sh_ref, sc_ref, gt_ref, ng_ref, wi_ref, wo_ref, o_ref, xm_ref, act_ref = refs
    xn = _rms(x, ng_ref[...])
    xm_ref[...] = (xn * (1.0 + sc_ref[...]) + sh_ref[...]).astype(BF16)
    tk = TK_FFN
    for j in range(D_FF // tk):
        xm = xm_ref[...]
        a = jnp.dot(xm, wi_ref[:, j * tk:(j + 1) * tk], preferred_element_type=F32)
        b = jnp.dot(xm, wi_ref[:, D_FF + j * tk:D_FF + (j + 1) * tk], preferred_element_type=F32)
        act_ref[:, j * tk:(j + 1) * tk] = (_silu(a) * b).astype(BF16)
    y = jnp.dot(act_ref[...], wo_ref[...], preferred_element_type=F32)
    o_ref[...] = x + 0.5 * gt_ref[...] * y


def _ffn(xs, mod, j0, ng, w_in, w_out, layer, which):
    tm = TM_FFN
    n_ctx_tiles = N_CTX // tm
    if len(xs) == 2:
        x_specs = [pl.BlockSpec((tm, D_MODEL), lambda i: (jnp.minimum(i, n_ctx_tiles - 1), 0)),
                   pl.BlockSpec((tm, D_MODEL), lambda i: (jnp.maximum(i - n_ctx_tiles, 0), 0))]
    else:
        x_specs = [pl.BlockSpec((tm, D_MODEL), lambda i: (i, 0))]
    resident = dict(pipeline_mode=pl.Buffered(1))
    return pl.pallas_call(
        functools.partial(_ffn_kernel, len(xs) == 2),
        grid=(N_TOK // tm,),
        in_specs=x_specs + [
            _mod_spec(tm, j0, 1), _mod_spec(tm, j0 + 1, 1), _mod_spec(tm, j0 + 2, 1),
            pl.BlockSpec((None, None, 1, D_MODEL), lambda i: (layer, j0 // 3, 0, 0)),
            pl.BlockSpec((None, None, D_MODEL, 2 * D_FF), lambda i: (layer, which, 0, 0), **resident),
            pl.BlockSpec((None, None, D_FF, D_MODEL), lambda i: (layer, which, 0, 0), **resident),
        ],
        out_specs=pl.BlockSpec((tm, D_MODEL), lambda i: (i, 0)),
        out_shape=jax.ShapeDtypeStruct((N_TOK, D_MODEL), F32),
        scratch_shapes=[pltpu.VMEM((tm, D_MODEL), BF16), pltpu.VMEM((tm, D_FF), BF16)],
        compiler_params=pltpu.CompilerParams(dimension_semantics=("parallel",),
                                             vmem_limit_bytes=FFN_VMEM_BYTES),
        name="ffn",
    )(*xs, mod, mod, mod, ng, w_in, w_out)


def _proj_kernel(x_ref, sh_ref, sc_ref, ng_ref, w_ref, o_ref):
    xn = _rms(x_ref[...], ng_ref[...])
    o_ref[...] = _bdot(xn * (1.0 + sc_ref[...]) + sh_ref[...], w_ref[...])


def _in_proj(h, mod, ng, w, layer):
    tm = TM_PROJ
    return pl.pallas_call(
        _proj_kernel,
        grid=(N_TOK // tm,),
        in_specs=[
            pl.BlockSpec((tm, D_MODEL), lambda i: (i, 0)),
            _mod_spec(tm, 3, 1), _mod_spec(tm, 4, 1),
            pl.BlockSpec((None, None, 1, D_MODEL), lambda i: (layer, 1, 0, 0)),
            pl.BlockSpec((None, D_MODEL, PROJ_WIDTH), lambda i: (layer, 0, 0)),
        ],
        out_specs=pl.BlockSpec((tm, PROJ_WIDTH), lambda i: (i, 0)),
        out_shape=jax.ShapeDtypeStruct((N_TOK, PROJ_WIDTH), F32),
        compiler_params=pltpu.CompilerParams(dimension_semantics=("parallel",)),
        name="in_proj",
    )(h, mod, mod, ng, w)


def _rope_tables(length):
    rows = length // GRID_W
    r = np.repeat(np.arange(rows, dtype=np.float32), GRID_W)
    col = np.tile(np.arange(GRID_W, dtype=np.float32), rows)
    nf = HEAD_DIM // 4
    inv = (np.float32(ROPE_THETA) ** (-np.arange(nf, dtype=np.float32) / nf)).astype(np.float32)
    ang = np.concatenate([r[:, None] * inv, col[:, None] * inv], axis=-1).astype(np.float32)
    cos, sin = np.cos(ang).astype(np.float32), np.sin(ang).astype(np.float32)
    return np.tile(cos, (1, 4)), np.tile(np.concatenate([-sin, sin], axis=-1), (1, 2))


def _pair_lanes(rows):
    lane = lax.broadcasted_iota(jnp.int32, (rows, 2 * HEAD_DIM), 1)
    return lane < HEAD_DIM, (lane % HEAD_DIM) < HEAD_DIM // 2


def _rope_pair(x, cos, sin_signed, first_half):
    partner = jnp.where(first_half, pltpu.roll(x, 3 * HEAD_DIM // 2, 1), pltpu.roll(x, HEAD_DIM // 2, 1))
    return x * cos + partner * sin_signed


def _rms_pair(x, g, lo):
    x2 = x * x
    s_lo = jnp.sum(jnp.where(lo, x2, 0.0), axis=-1, keepdims=True)
    s_hi = jnp.sum(jnp.where(lo, 0.0, x2), axis=-1, keepdims=True)
    ms = jnp.where(lo, s_lo, s_hi) * (1.0 / HEAD_DIM)
    return x * lax.rsqrt(ms + EPS) * g


def _attn_core(q, k_bf, v_bf, lo, *, mask=None, ctx=None, sinks=None):
    tq = q.shape[0]
    pair = 2 * HEAD_DIM
    scale = HEAD_DIM ** -0.5
    qa, qb = q[:, :pair] * scale, q[:, pair:] * scale
    qs = jnp.concatenate([jnp.where(lo, qa, 0.0), jnp.where(lo, 0.0, qa),
                          jnp.where(lo, qb, 0.0), jnp.where(lo, 0.0, qb)], axis=0).astype(BF16)
    s = _bdot_nt(qs, k_bf)
    if ctx is not None:
        s_c = _bdot_nt(qs, ctx[0])
    ps, pcs, dens = [], [], []
    for j in range(N_HEADS):
        rows = slice(j * tq, (j + 1) * tq)
        sj = s[rows]
        if mask is not None:
            sj = jnp.where(mask, sj, -1e30)
        m = jnp.max(sj, axis=-1, keepdims=True)
        if ctx is not None:
            m = jnp.maximum(m, jnp.max(s_c[rows], axis=-1, keepdims=True))
        if sinks is not None:
            m = jnp.maximum(m, sinks[j])
        p = jnp.exp(sj - m)
        den = jnp.sum(p, axis=-1, keepdims=True)
        ps.append(p.astype(BF16))
        if ctx is not None:
            pc = jnp.exp(s_c[rows] - m)
            den = den + jnp.sum(pc, axis=-1, keepdims=True)
            pcs.append(pc.astype(BF16))
        if sinks is not None:
            den = den + jnp.exp(sinks[j] - m)
        dens.append(den)
    r = jnp.dot(jnp.concatenate(ps, axis=0), v_bf, preferred_element_type=F32)
    if ctx is not None:
        r = r + jnp.dot(jnp.concatenate(pcs, axis=0), ctx[1], preferred_element_type=F32)
    o = [r[j * tq:(j + 1) * tq] / dens[j] for j in range(N_HEADS)]
    return jnp.concatenate([jnp.where(lo, o[0], o[1]), jnp.where(lo, o[2], o[3])], axis=1)


def _attn_ctx_kernel(names, *refs):
    r = dict(zip(names, refs))
    lo, _ = _pair_lanes(SEQ)
    sinks = [r['sink'][h] for h in HEAD_ORDER] if 'sink' in r else None
    for bb in range(NB_CTX):
        rows = slice(bb * SEQ, (bb + 1) * SEQ)
        q, k, v = r['q'][rows, :], r['k'][rows, :], r['v'][rows, :]
        if 'qkg' in r:
            gq, gk = r['qkg'][0:1, :], r['qkg'][1:2, :]
            q = jnp.concatenate([_rms_pair(q[:, :KV_WIDTH], gq, lo), _rms_pair(q[:, KV_WIDTH:], gq, lo)], axis=1)
            k = _rms_pair(k, gk, lo)
        r['k_out'][bb] = k
        r['v_out'][bb] = v
        r['o'][rows, :] = _attn_core(q, k.astype(BF16), v.astype(BF16), lo, sinks=sinks)


def _attn_lat_kernel(names, window, *refs):
    r = dict(zip(names, refs))
    tq = TQ_ATTN
    i = pl.program_id(1)
    lo, first_half = _pair_lanes(tq)
    qk_norm = 'qkg' in r

    @pl.when(i == 0)
    def _():
        lo_k, first_half_k = _pair_lanes(DEC_SEQ)
        k = r['k'][...]
        if qk_norm:
            k = _rms_pair(k, r['qkg'][1:2, :], lo_k)
        r['k_scr'][...] = _rope_pair(k, r['cos_k'][...], r['sin_k'][...], first_half_k).astype(BF16)

    q = r['q'][...]
    halves = []
    for c0 in (0, KV_WIDTH):
        x = q[:, c0:c0 + KV_WIDTH]
        if qk_norm:
            x = _rms_pair(x, r['qkg'][0:1, :], lo)
        halves.append(_rope_pair(x, r['cos_q'][...], r['sin_q'][...], first_half))
    q = jnp.concatenate(halves, axis=1)
    sinks = [r['sink'][h] for h in HEAD_ORDER] if 'sink' in r else None
    ctx = (r['kc'][...].astype(BF16), r['vc'][...].astype(BF16))
    if window:
        span = tq + 2 * WINDOW
        start = pl.multiple_of(jnp.clip(i * tq - WINDOW, 0, DEC_SEQ - span), WINDOW)
        qpos = i * tq + lax.broadcasted_iota(jnp.int32, (tq, span), 0)
        kpos = start + lax.broadcasted_iota(jnp.int32, (tq, span), 1)
        mask = jnp.abs(qpos - kpos) <= WINDOW
        k_bf = r['k_scr'][pl.ds(start, span), :]
        v_bf = r['v'][pl.ds(start, span), :].astype(BF16)
    else:
        mask = None
        k_bf = r['k_scr'][...]
        v_bf = r['v'][...].astype(BF16)
    r['o'][...] = _attn_core(q, k_bf, v_bf, lo, mask=mask, ctx=ctx, sinks=sinks)


def _attention_context(u, layer, cache_prev, *, col_q, col_k, col_v, sink=None, qkg2=None):
    rows = NB_CTX * SEQ
    qb, kb, vb = col_q // GROUP_WIDTH, col_k // KV_WIDTH, col_v // KV_WIDTH
    names = ['q', 'k', 'v']
    args = [u, u, u]
    in_specs = [pl.BlockSpec((rows, GROUP_WIDTH), lambda b: (b, qb)),
                pl.BlockSpec((rows, KV_WIDTH), lambda b: (b, kb)),
                pl.BlockSpec((rows, KV_WIDTH), lambda b: (b, vb))]
    if sink is not None:
        names.append('sink'); args.append(sink)
        in_specs.append(pl.BlockSpec(memory_space=pltpu.SMEM))
    if qkg2 is not None:
        names.append('qkg'); args.append(qkg2)
        in_specs.append(pl.BlockSpec((2, KV_WIDTH), lambda b: (0, 0)))
    aliases = {}
    if cache_prev is not None:
        for j, prev in enumerate(cache_prev):
            aliases[len(args)] = 1 + j
            names.append(f'prev{j}'); args.append(prev); in_specs.append(_any_spec())
    names += ['o', 'k_out', 'v_out']
    cache_shape = jax.ShapeDtypeStruct((BATCH, DEPTH, SEQ, KV_WIDTH), F32)
    cache_spec = pl.BlockSpec((NB_CTX, None, SEQ, KV_WIDTH), lambda b: (b, layer, 0, 0))
    return pl.pallas_call(
        functools.partial(_attn_ctx_kernel, tuple(names)),
        grid=(BATCH // NB_CTX,), in_specs=in_specs,
        out_specs=(pl.BlockSpec((rows, GROUP_WIDTH), lambda b: (b, 0)), cache_spec, cache_spec),
        out_shape=(jax.ShapeDtypeStruct((N_TOK, GROUP_WIDTH), F32), cache_shape, cache_shape),
        input_output_aliases=aliases,
        compiler_params=pltpu.CompilerParams(dimension_semantics=("parallel",)),
        name="attn_context",
    )(*args)


def _attention_latent(u, prev, *, col_q, col_k, col_v, cache_k, cache_v, window=False, sink=None, qkg2=None):
    tq = TQ_ATTN
    nq = DEC_SEQ // tq
    row0_q, row0_k = N_CTX // tq, N_CTX // DEC_SEQ
    qb, kb, vb = col_q // GROUP_WIDTH, col_k // KV_WIDTH, col_v // KV_WIDTH
    cos, sin = _rope_tables(DEC_SEQ)
    const = lambda b, i: (0, 0)
    names = ['q', 'k', 'v', 'kc', 'vc', 'cos_q', 'sin_q', 'cos_k', 'sin_k']
    args = [u, u, u, cache_k, cache_v, cos, sin, cos, sin]
    in_specs = [
        pl.BlockSpec((tq, GROUP_WIDTH), lambda b, i: (row0_q + b * nq + i, qb)),
        pl.BlockSpec((DEC_SEQ, KV_WIDTH), lambda b, i: (row0_k + b, kb)),
        pl.BlockSpec((DEC_SEQ, KV_WIDTH), lambda b, i: (row0_k + b, vb)),
        pl.BlockSpec((None, PAST_LEN, KV_WIDTH), lambda b, i: (b, 0, 0)),
        pl.BlockSpec((None, PAST_LEN, KV_WIDTH), lambda b, i: (b, 0, 0)),
        pl.BlockSpec((tq, KV_WIDTH), lambda b, i: (i, 0)),
        pl.BlockSpec((tq, KV_WIDTH), lambda b, i: (i, 0)),
        pl.BlockSpec((DEC_SEQ, KV_WIDTH), const),
        pl.BlockSpec((DEC_SEQ, KV_WIDTH), const),
    ]
    if sink is not None:
        names.append('sink'); args.append(sink)
        in_specs.append(pl.BlockSpec(memory_space=pltpu.SMEM))
    if qkg2 is not None:
        names.append('qkg'); args.append(qkg2)
        in_specs.append(pl.BlockSpec((2, KV_WIDTH), const))
    aliases = {len(args): 0}
    names.append('prev'); args.append(prev); in_specs.append(_any_spec())
    names += ['o', 'k_scr']
    return pl.pallas_call(
        functools.partial(_attn_lat_kernel, tuple(names), window),
        grid=(DEC_BATCH, nq), in_specs=in_specs,
        out_specs=pl.BlockSpec((tq, GROUP_WIDTH), lambda b, i: (row0_q + b * nq + i, 0)),
        out_shape=jax.ShapeDtypeStruct((N_TOK, GROUP_WIDTH), F32),
        scratch_shapes=[pltpu.VMEM((DEC_SEQ, KV_WIDTH), BF16)],
        input_output_aliases=aliases,
        compiler_params=pltpu.CompilerParams(dimension_semantics=("parallel", "arbitrary")),
        name="attn_latent",
    )(*args)


def _dft_matrices(length):
    n = length
    k = np.arange(n, dtype=np.int64)[:, None]
    s = np.arange(n, dtype=np.int64)[None, :]
    ang = np.pi * ((k * s) % (2 * n)).astype(np.float64) / n
    fwd_cos = np.cos(ang)
    fwd_sin = -np.sin(ang)
    fwd_sin[0, :] = 1.0 - 2.0 * (np.arange(n) % 2)
    fwd = np.concatenate([fwd_cos, fwd_sin], axis=0)
    wk = np.full((n,), 2.0)
    wk[0] = 1.0
    inv_cos = (np.cos(ang) * wk[:, None]).T / (2 * n)
    inv_sin = (-2.0 * np.sin(ang)).T / (2 * n)
    inv_sin[:, 0] = (1.0 - 2.0 * (np.arange(n) % 2)) / (2 * n)
    inv = np.concatenate([inv_cos, inv_sin], axis=1)
    return fwd.astype(np.float32), inv.astype(np.float32)


def _filter_features(length):
    t = np.linspace(0.0, 1.0, length, dtype=np.float32)[:, None]
    w = (np.float32(2.0 * math.pi / length) * np.arange(length, dtype=np.float32))[:, None]
    bands = np.linspace(1e-4, HY_BANDS - 1, HY_BANDS, dtype=np.float32)[None, :]
    z = np.concatenate([t, np.cos(bands * w), -np.sin(bands * w)], axis=-1).astype(np.float32)
    zp = np.zeros((length, 128), np.float32)
    zp[:, :HY_EMB] = z
    deltas = np.linspace(math.log(HY_TARGET) / HY_SLOW, math.log(HY_TARGET) / HY_FAST, HY_CH, dtype=np.float32)
    decay = np.exp(-t * np.abs(deltas)).astype(np.float32)
    return zp, decay


def _filter_kernel(z_ref, w1_ref, b1_ref, w2_ref, b2_ref, w3_ref, fr_ref, dec_ref, f_ref, kr_ref, ki_ref):
    n = z_ref.shape[0]
    h = jnp.sin(fr_ref[0:1, :] * (_fdot(z_ref[...], w1_ref[...]) + b1_ref[...]))
    h = jnp.sin(fr_ref[1:2, :] * (_fdot(h, w2_ref[...]) + b2_ref[...]))
    h = _fdot(h, w3_ref[...])
    dec = dec_ref[...]
    half = HY_ORDER * HY_CH
    pos = jnp.concatenate([h[:, j * HY_CH:(j + 1) * HY_CH] * dec for j in range(HY_ORDER)], axis=-1)
    neg = jnp.concatenate([h[:, half + j * HY_CH:half + (j + 1) * HY_CH] * dec for j in range(HY_ORDER)], axis=-1)
    row = lax.broadcasted_iota(jnp.int32, (n, half), 0)
    neg = jnp.where(row == 0, 0.0, neg)
    fwd = f_ref[...]
    a = jnp.dot(fwd, (pos + neg).astype(BF16), preferred_element_type=F32)
    b = jnp.dot(fwd, (pos - neg).astype(BF16), preferred_element_type=F32)
    kr_ref[...] = a[:n]
    ki_ref[...] = jnp.where(row == 0, a[n:], b[n:])


def _hyena_filters(length, fwd, w1p, b1, w2, b2, w3, freq):
    zp, decay = _filter_features(length)
    half = HY_ORDER * HY_CH
    return pl.pallas_call(
        _filter_kernel,
        out_shape=(jax.ShapeDtypeStruct((length, half), F32), jax.ShapeDtypeStruct((length, half), F32)),
        name="hyena_filters",
    )(zp, w1p, b1.reshape(1, HY_FH), w2, b2.reshape(1, HY_FH), w3, freq, decay, fwd)


def _hyena_kernel(z_ref, cw_ref, cb_ref, f_ref, g_ref, kr_ref, ki_ref, bias_ref, o_ref):
    n = z_ref.shape[0]
    z = z_ref[...]
    row = lax.broadcasted_iota(jnp.int32, z.shape, 0)
    prev = jnp.where(row == 0, 0.0, pltpu.roll(z, 1, 0))
    nxt = jnp.where(row == n - 1, 0.0, pltpu.roll(z, n - 1, 0))
    z = prev * cw_ref[0:1, :] + z * cw_ref[1:2, :] + nxt * cw_ref[2:3, :] + cb_ref[...]
    v, x1, x2 = z[:, :HY_CH], z[:, HY_CH:2 * HY_CH], z[:, 2 * HY_CH:]
    row0 = lax.broadcasted_iota(jnp.int32, (n, HY_CH), 0) == 0

    def long_conv(x, order):
        cols = slice(order * HY_CH, (order + 1) * HY_CH)
        kr, ki = kr_ref[:, cols], ki_ref[:, cols]
        spec = jnp.dot(f_ref[...], x.astype(BF16), preferred_element_type=F32)
        ur, ui = spec[:n], spec[n:]
        yr = ur * kr - jnp.where(row0, 0.0, ui * ki)
        yi = jnp.where(row0, ui * ki, ur * ki + ui * kr)
        y = jnp.dot(g_ref[...], jnp.concatenate([yr, yi], axis=0).astype(BF16), preferred_element_type=F32)
        return y + x * bias_ref[order:order + 1, :]

    y = x1 * long_conv(v, 0)
    o_ref[...] = x2 * long_conv(y, 1)


def _hyena(u, prev, *, latent, fwd, inv, kr, ki, conv_w, conv_b, bias):
    n = DEC_SEQ if latent else SEQ
    nb = DEC_BATCH if latent else BATCH
    row0 = N_CTX // n if latent else 0
    half = HY_ORDER * HY_CH
    const = lambda b: (0, 0)
    args = [u, conv_w, conv_b.reshape(1, 3 * HY_CH), fwd, inv, kr, ki, bias]
    in_specs = [
        pl.BlockSpec((n, 3 * HY_CH), lambda b: (row0 + b, COL_HY // (3 * HY_CH))),
        pl.BlockSpec((3, 3 * HY_CH), const),
        pl.BlockSpec((1, 3 * HY_CH), const),
        pl.BlockSpec((2 * n, n), const),
        pl.BlockSpec((n, 2 * n), const),
        pl.BlockSpec((n, half), const),
        pl.BlockSpec((n, half), const),
        pl.BlockSpec((HY_ORDER, HY_CH), const),
    ]
    aliases = {}
    if latent:
        aliases = {len(args): 0}
        args.append(prev)
        in_specs.append(_any_spec())

    def body(*refs):
        if latent:
            refs = refs[:len(args) - 1] + refs[len(args):]
        _hyena_kernel(*refs)

    return pl.pallas_call(
        body, grid=(nb,), in_specs=in_specs,
        out_specs=pl.BlockSpec((n, HY_CH), lambda b: (row0 + b, 0)),
        out_shape=jax.ShapeDtypeStruct((N_TOK, HY_CH), F32),
        input_output_aliases=aliases,
        compiler_params=pltpu.CompilerParams(dimension_semantics=("parallel",)),
        name="hyena_latent" if latent else "hyena_context",
    )(*args)


def _gla_masks():
    t = np.arange(ROWS_GROUP)[:, None]
    s = np.arange(ROWS_GROUP)[None, :]
    same = (t // GLA_CHUNK) == (s // GLA_CHUNK)
    return np.stack([same & (s <= t), same & (s >= t), same]).astype(np.float32)


def _gla_group(r, rows, z, st):
    c = GLA_CHUNK
    hk = H_GLA * DK_GLA
    n_chunks = ROWS_GROUP // c
    logits = _bdot(r['gl'][rows, :], r['gw'][z]) + r['gb'][z:z + 1, :]
    g = (jnp.minimum(logits, 0.0) - jnp.log1p(jnp.exp(-jnp.abs(logits)))) / GLA_NORM
    causal = r['masks'][z]
    b = _split_dot(causal.astype(BF16), g)
    tot = _split_dot(r['masks'][2].astype(BF16), g)
    k = r['k'][rows, :]
    q_t = r['q'][rows, :] * jnp.exp(b) * (DK_GLA ** -0.5)
    k_t = (k * jnp.exp(-b)).astype(BF16)
    k_e = (k * jnp.exp(tot - b)).astype(BF16)
    dec = jnp.exp(tot)
    v = r['v'][rows, :].astype(BF16)

    qlane = lax.broadcasted_iota(jnp.int32, (ROWS_GROUP, hk), 1) // DK_GLA
    vlane = lax.broadcasted_iota(jnp.int32, (ROWS_GROUP, GROUP_WIDTH), 1) // DV_GLA
    qs = jnp.concatenate([jnp.where(qlane == h, q_t, 0.0) for h in range(H_GLA)], axis=0).astype(BF16)
    s = _bdot_nt(qs, k_t)
    valid = causal != 0.0
    p = jnp.concatenate([jnp.where(valid, s[h * ROWS_GROUP:(h + 1) * ROWS_GROUP], 0.0).astype(BF16)
                         for h in range(H_GLA)], axis=0)
    res = jnp.dot(p, v, preferred_element_type=F32)
    o_intra = jnp.where(vlane == 0, res[0:ROWS_GROUP], 0.0)
    for h in range(1, H_GLA):
        o_intra = jnp.where(vlane == h, res[h * ROWS_GROUP:(h + 1) * ROWS_GROUP], o_intra)

    bd = (lax.broadcasted_iota(jnp.int32, (GROUP_WIDTH, hk), 0) // DV_GLA
          == lax.broadcasted_iota(jnp.int32, (GROUP_WIDTH, hk), 1) // DK_GLA)
    q_bf = q_t.astype(BF16)
    o_inter = [None] * n_chunks
    for step in range(n_chunks):
        ci = step if z == 0 else n_chunks - 1 - step
        cr = slice(ci * c, (ci + 1) * c)
        o_inter[ci] = _bdot_nt(q_bf[cr], st)
        st = st * dec[ci * c:ci * c + 1, :] + jnp.where(bd, _bdot_tn(v[cr], k_e[cr]), 0.0)
    return o_intra + jnp.concatenate(o_inter, axis=0), st


def _head_rms(o, g):
    hb = (lax.broadcasted_iota(jnp.int32, (GROUP_WIDTH, GROUP_WIDTH), 0) // DV_GLA
          == lax.broadcasted_iota(jnp.int32, (GROUP_WIDTH, GROUP_WIDTH), 1) // DV_GLA)
    ms = _split_dot_rhs(o * o, hb.astype(BF16)) * (1.0 / DV_GLA)
    return o * lax.rsqrt(ms + EPS) * g


def _split_dot_rhs(x, m):
    hi = x.astype(BF16)
    lo = (x - hi.astype(F32)).astype(BF16)
    return jnp.dot(hi, m, preferred_element_type=F32) + jnp.dot(lo, m, preferred_element_type=F32)


def _expand_state(s):
    hk = H_GLA * DK_GLA
    bd = (lax.broadcasted_iota(jnp.int32, (GROUP_WIDTH, hk), 0) // DV_GLA
          == lax.broadcasted_iota(jnp.int32, (GROUP_WIDTH, hk), 1) // DK_GLA)
    return jnp.where(bd, jnp.concatenate([s] * H_GLA, axis=0), 0.0)


def _compact_state(st):
    out = st[0:DV_GLA]
    for h in range(1, H_GLA):
        out = out + st[h * DV_GLA:(h + 1) * DV_GLA]
    return out


def _gla_ctx_kernel(names, *refs):
    r = dict(zip(names, refs))
    hk = H_GLA * DK_GLA
    for bb in range(NB_CTX):
        rows = slice(bb * SEQ, (bb + 1) * SEQ)
        zero = jnp.zeros((GROUP_WIDTH, hk), F32)
        o_f, st_f = _gla_group(r, rows, 0, zero)
        o_b, st_b = _gla_group(r, rows, 1, zero)
        r['o'][rows, :] = _head_rms(o_f + o_b, r['mg'][...])
        r['sf'][bb, 0] = _compact_state(st_f)
        r['sf'][bb, 1] = _compact_state(st_b)


def _gla_lat_kernel(names, *refs):
    r = dict(zip(names, refs))
    n_groups = DEC_SEQ // ROWS_GROUP

    def run(z):
        r['st'][...] = _expand_state(r['s0'][z])

        def body(step, carry):
            gi = step if z == 0 else n_groups - 1 - step
            rows = pl.ds(pl.multiple_of(gi * ROWS_GROUP, ROWS_GROUP), ROWS_GROUP)
            o, st = _gla_group(r, rows, z, r['st'][...])
            r['st'][...] = st
            if z == 0:
                r['o'][rows, :] = o
            else:
                r['o'][rows, :] = _head_rms(r['o'][rows, :] + o, r['mg'][...])
            return carry

        lax.fori_loop(0, n_groups, body, 0)

    run(0)
    run(1)


def _gla(u, prev, *, latent, gate_w, gate_b, mix_g_d, s0=None):
    n = DEC_SEQ if latent else NB_CTX * SEQ
    steps = DEC_BATCH if latent else BATCH // NB_CTX
    row0 = N_CTX // n if latent else 0
    hk = H_GLA * DK_GLA
    const2 = lambda b: (0, 0)
    names = ['q', 'k', 'v', 'gl', 'gw', 'gb', 'mg', 'masks']
    args = [u, u, u, u, gate_w, gate_b, mix_g_d, _gla_masks()]
    in_specs = [
        pl.BlockSpec((n, hk), lambda b: (row0 + b, COL_QD // hk)),
        pl.BlockSpec((n, hk), lambda b: (row0 + b, COL_KD // hk)),
        pl.BlockSpec((n, GROUP_WIDTH), lambda b: (row0 + b, COL_VD // GROUP_WIDTH)),
        pl.BlockSpec((n, 128), lambda b: (row0 + b, COL_GD // 128)),
        pl.BlockSpec((2, 128, hk), lambda b: (0, 0, 0)),
        pl.BlockSpec((2, hk), const2),
        pl.BlockSpec((1, GROUP_WIDTH), const2),
        pl.BlockSpec((3, ROWS_GROUP, ROWS_GROUP), lambda b: (0, 0, 0)),
    ]
    out_shape = jax.ShapeDtypeStruct((N_TOK, GROUP_WIDTH), F32)
    out_specs = pl.BlockSpec((n, GROUP_WIDTH), lambda b: (row0 + b, 0))
    aliases, scratch = {}, []
    if latent:
        names.append('s0'); args.append(s0)
        in_specs.append(pl.BlockSpec((None, 2, DV_GLA, hk), lambda b: (b, 0, 0, 0)))
        aliases = {len(args): 0}
        names.append('prev'); args.append(prev); in_specs.append(_any_spec())
        names += ['o', 'st']
        scratch = [pltpu.VMEM((GROUP_WIDTH, hk), F32)]
        body = functools.partial(_gla_lat_kernel, tuple(names))
    else:
        names += ['o', 'sf']
        out_shape = (out_shape, jax.ShapeDtypeStruct((BATCH, 2, DV_GLA, hk), F32))
        out_specs = (out_specs, pl.BlockSpec((NB_CTX, 2, DV_GLA, hk), lambda b: (b, 0, 0, 0)))
        body = functools.partial(_gla_ctx_kernel, tuple(names))
    return pl.pallas_call(
        body, grid=(steps,), in_specs=in_specs, out_specs=out_specs, out_shape=out_shape,
        scratch_shapes=scratch, input_output_aliases=aliases,
        compiler_params=pltpu.CompilerParams(dimension_semantics=("parallel",)),
        name="gla_latent" if latent else "gla_context",
    )(*args)


def _merge_kernel(h_ref, oa_ref, ob_ref, oc_ref, od_ref, rd_ref, gt_ref, mg_ref, wo_ref, o_ref):
    gw = GROUP_WIDTH
    ys = [
        _rms(oa_ref[...], mg_ref[:, 0:gw]),
        _rms(ob_ref[...], mg_ref[:, gw:2 * gw]),
        _rms(oc_ref[...], mg_ref[:, 2 * gw:3 * gw]),
        od_ref[...] * _silu(rd_ref[...]),
    ]
    mix = _bdot(ys[0], wo_ref[0:gw, :])
    for j in range(1, 4):
        mix = mix + _bdot(ys[j], wo_ref[j * gw:(j + 1) * gw, :])
    o_ref[...] = h_ref[...] + gt_ref[...] * mix


def _merge(h, oa, ob, oc, od, u, mod, mix_g, w_out, layer):
    tm = TM_PROJ
    grp = pl.BlockSpec((tm, GROUP_WIDTH), lambda i: (i, 0))
    return pl.pallas_call(
        _merge_kernel,
        grid=(N_TOK // tm,),
        in_specs=[
            pl.BlockSpec((tm, D_MODEL), lambda i: (i, 0)),
            grp, grp, grp, grp,
            pl.BlockSpec((tm, GROUP_WIDTH), lambda i: (i, COL_RD // GROUP_WIDTH)),
            _mod_spec(tm, 5, 1),
            pl.BlockSpec((None, 1, D_MODEL), lambda i: (layer, 0, 0)),
            pl.BlockSpec((None, D_MODEL, D_MODEL), lambda i: (layer, 0, 0)),
        ],
        out_specs=pl.BlockSpec((tm, D_MODEL), lambda i: (i, 0)),
        out_shape=jax.ShapeDtypeStruct((N_TOK, D_MODEL), F32),
        compiler_params=pltpu.CompilerParams(dimension_semantics=("parallel",)),
        name="merge",
    )(h, oa, ob, oc, od, u, mod, mix_g, w_out)


def _norm_kernel(x_ref, g_ref, o_ref):
    o_ref[...] = _rms(x_ref[...], g_ref[...])


def _final_norm(h, g, row0, rows):
    tm = TM_PROJ
    return pl.pallas_call(
        _norm_kernel,
        grid=(rows // tm,),
        in_specs=[pl.BlockSpec((tm, D_MODEL), lambda i: (row0 // tm + i, 0)),
                  pl.BlockSpec((1, D_MODEL), lambda i: (0, 0))],
        out_specs=pl.BlockSpec((tm, D_MODEL), lambda i: (i, 0)),
        out_shape=jax.ShapeDtypeStruct((rows, D_MODEL), F32),
        compiler_params=pltpu.CompilerParams(dimension_semantics=("parallel",)),
        name="final_norm",
    )(h, g.reshape(1, D_MODEL))


def _permute_w_in(w):
    r = _REF_COLS
    span = lambda name, width: np.arange(r[name], r[name] + width)
    cols = np.concatenate([span('hy', 768), r['qa'] + _HEAD_PERM, span('ka', 128), span('va', 128),
                           r['qc'] + _HEAD_PERM, span('kc', 128), span('vc', 128), span('qd', 128),
                           span('kd', 128), span('vd', 256), span('rd', 256), span('gd', 2 * GLA_RANK)])
    return jnp.pad(w[..., cols], ((0, 0), (0, 0), (0, 128 - 2 * GLA_RANK)))


def _mix_perm():
    gw = GROUP_WIDTH
    return np.concatenate([_HEAD_PERM, gw + np.arange(gw), 2 * gw + _HEAD_PERM, 3 * gw + np.arange(gw)])


def _gate_weights(gate_w):
    out = jnp.zeros((2, 128, H_GLA * DK_GLA), gate_w.dtype)
    out = out.at[0, 0:GLA_RANK].set(gate_w[0])
    return out.at[1, GLA_RANK:2 * GLA_RANK].set(gate_w[1])


def _states_to_kernel(st):
    b = st.shape[0]
    return st.transpose(0, 1, 4, 2, 3).reshape(b, 2, DV_GLA, H_GLA * DK_GLA)


def _states_from_kernel(st):
    b = st.shape[0]
    return st.reshape(b, 2, DV_GLA, H_GLA, DK_GLA).transpose(0, 1, 3, 4, 2)


def kernel(x_prompt, x_sample, cache_swa_k, cache_swa_v, cache_gqa_k, cache_gqa_v, state_gla, c, c_ctx, w_mod, b_mod, norm_g, ffn_w_in, ffn_w_out, w_in, w_out, mix_g, swa_sink, qk_norm_g, hy_conv_w, hy_conv_b, hy_w1, hy_b1, hy_w2, hy_b2, hy_w3, hy_freq, hy_bias, gla_gate_w, gla_gate_b, final_g):
    cond = jnp.zeros((MOD_ROWS, D_MODEL), F32).at[0].set(c_ctx).at[1:1 + DEC_BATCH].set(c)
    mod_all = _modulation(cond, w_mod, b_mod).reshape(DEPTH, MOD_ROWS, N_MOD, 1, D_MODEL)

    dft = {n: tuple(jnp.asarray(m).astype(BF16) for m in _dft_matrices(n)) for n in (SEQ, DEC_SEQ)}
    mix_perm = _mix_perm()

    xs = [x_prompt.reshape(N_CTX, D_MODEL), x_sample.reshape(N_LAT, D_MODEL)]
    ffn_in = ffn_w_in.astype(BF16)
    ffn_out = ffn_w_out.astype(BF16)
    ng = norm_g.reshape(DEPTH, 3, 1, D_MODEL)
    proj_w = _permute_w_in(w_in).astype(BF16)
    out_w = w_out[:, mix_perm].astype(BF16)
    mix_gp = mix_g[:, mix_perm].reshape(DEPTH, 1, D_MODEL)
    cache_a = cache_c = None
    sts = []
    for l in range(DEPTH):
        mod = mod_all[l]
        h = _ffn(xs, mod, 0, ng, ffn_in, ffn_out, l, 0)
        u = _in_proj(h, mod, ng, proj_w, l)

        ck_a = cache_swa_k[:, l].reshape(DEC_BATCH, PAST_LEN, KV_WIDTH)
        cv_a = cache_swa_v[:, l].reshape(DEC_BATCH, PAST_LEN, KV_WIDTH)
        ck_c = cache_gqa_k[:, l].reshape(DEC_BATCH, PAST_LEN, KV_WIDTH)
        cv_c = cache_gqa_v[:, l].reshape(DEC_BATCH, PAST_LEN, KV_WIDTH)
        qkg2 = jnp.tile(qk_norm_g[l], (1, N_KV))

        oa, *cache_a = _attention_context(u, l, cache_a, col_q=COL_QA, col_k=COL_KA, col_v=COL_VA,
                                          sink=swa_sink[l])
        oa = _attention_latent(u, oa, col_q=COL_QA, col_k=COL_KA, col_v=COL_VA, cache_k=ck_a, cache_v=cv_a,
                               window=True, sink=swa_sink[l])
        oc, *cache_c = _attention_context(u, l, cache_c, col_q=COL_QC, col_k=COL_KC, col_v=COL_VC, qkg2=qkg2)
        oc = _attention_latent(u, oc, col_q=COL_QC, col_k=COL_KC, col_v=COL_VC, cache_k=ck_c, cache_v=cv_c,
                               qkg2=qkg2)

        w1p = jnp.zeros((128, HY_FH), F32).at[:HY_EMB].set(hy_w1[l])
        ob = None
        for latent, n in ((False, SEQ), (True, DEC_SEQ)):
            fwd, inv = dft[n]
            kr, ki = _hyena_filters(n, fwd, w1p, hy_b1[l], hy_w2[l], hy_b2[l], hy_w3[l], hy_freq[l])
            ob = _hyena(u, ob, latent=latent, fwd=fwd, inv=inv, kr=kr, ki=ki, conv_w=hy_conv_w[l],
                        conv_b=hy_conv_b[l], bias=hy_bias[l])

        gw = _gate_weights(gla_gate_w[l]).astype(BF16)
        mg_d = mix_g[l, 3 * GROUP_WIDTH:].reshape(1, GROUP_WIDTH)
        od, st = _gla(u, None, latent=False, gate_w=gw, gate_b=gla_gate_b[l], mix_g_d=mg_d)
        od = _gla(u, od, latent=True, gate_w=gw, gate_b=gla_gate_b[l], mix_g_d=mg_d,
                  s0=_states_to_kernel(state_gla[:, l]))

        h = _merge(h, oa, ob, oc, od, u, mod, mix_gp, out_w, l)
        h = _ffn([h], mod, 6, ng, ffn_in, ffn_out, l, 1)
        xs = [h]
        sts.append(_states_from_kernel(st))

    y_prompt = _final_norm(h, final_g, 0, N_CTX).reshape(BATCH, SEQ, D_MODEL)
    y_sample = _final_norm(h, final_g, N_CTX, N_LAT).reshape(DEC_BATCH, DEC_SEQ, D_MODEL)
    caches = [x.reshape(BATCH, DEPTH, SEQ, N_KV, HEAD_DIM) for x in (*cache_a, *cache_c)]
    return (y_prompt, y_sample, *caches, jnp.stack(sts, axis=1))
```

```python
import functools
import math

import numpy as np
import jax
import jax.numpy as jnp
from jax import lax
from jax.experimental import pallas as pl
from jax.experimental.pallas import tpu as pltpu

F32 = jnp.float32
BF16 = jnp.bfloat16

D_MODEL = 1024
BATCH = 32
SEQ = 256
DEPTH = 2
DEC_BATCH = 2
DEC_SEQ = 1024
PAST_LEN = 512
GRID_W = 64
HEAD_DIM = 64
GROUP_WIDTH = 256
N_HEADS = 4
N_KV = 2
KV_WIDTH = N_KV * HEAD_DIM
WINDOW = 128
HY_CH = 256
HY_ORDER = 2
HY_BANDS = 16
HY_EMB = 2 * HY_BANDS + 1
HY_FH = 64
HY_TARGET = 1e-2
HY_FAST = 0.3
HY_SLOW = 1.5
H_GLA = 4
DV_GLA = 64
DK_GLA = 32
GLA_RANK = 16
GLA_CHUNK = 64
GLA_NORM = 16.0
D_FF = 2816
N_MOD = 9
ROPE_THETA = 10000.0
EPS = 1e-6

N_CTX = BATCH * SEQ
N_LAT = DEC_BATCH * DEC_SEQ
N_TOK = N_CTX + N_LAT
MOD_ROWS = 8

COL_HY, COL_QA, COL_KA, COL_VA = 0, 768, 1024, 1152
COL_QC, COL_KC, COL_VC = 1280, 1536, 1664
COL_QD, COL_KD, COL_VD, COL_GD, COL_RD = 1792, 1920, 2048, 2304, 2336
PROJ_WIDTH = 2688
REF_QA, REF_KA, REF_HY, REF_QC, REF_KC, REF_END = 0, 256, 512, 1280, 1536, 2592
COL_RD_BLOCK = 384
HEAD_ORDER = (0, 2, 1, 3)
_HEAD_PERM = np.concatenate([np.arange(HEAD_DIM) + HEAD_DIM * h for h in HEAD_ORDER])

TM_FFN = 1024
TK_FFN = 256
FFN_VMEM_BYTES = 60 * 1024 * 1024
TM_PROJ = 512
TQ_ATTN = 256
ROWS_GROUP = 256
NB_CTX = 2
NB_HYENA = 4


def _bdot(a, b):
    return jnp.dot(a.astype(BF16), b.astype(BF16), preferred_element_type=F32)


def _bdot_nt(a, b):
    return lax.dot_general(a.astype(BF16), b.astype(BF16), (((1,), (1,)), ((), ())),
                           preferred_element_type=F32)


def _bdot_tn(a, b):
    return lax.dot_general(a.astype(BF16), b.astype(BF16), (((0,), (0,)), ((), ())),
                           preferred_element_type=F32)


def _fdot(a, b):
    return jnp.dot(a, b, preferred_element_type=F32, precision=lax.Precision.HIGHEST)


def _split_dot(m, x):
    hi = x.astype(BF16)
    lo = (x - hi.astype(F32)).astype(BF16)
    return jnp.dot(m, hi, preferred_element_type=F32) + jnp.dot(m, lo, preferred_element_type=F32)


def _rms(x, g):
    return x * lax.rsqrt(jnp.mean(x * x, axis=-1, keepdims=True) + EPS) * g


def _silu(x):
    return x * jax.nn.sigmoid(x)


def _mod_row(i, tm):
    n_ctx_tiles = N_CTX // tm
    per_batch = DEC_SEQ // tm
    return jnp.where(i < n_ctx_tiles, 0, 1 + (i - n_ctx_tiles) // per_batch)


def _mod_spec(tm, j, grid_rank):
    if grid_rank == 1:
        return pl.BlockSpec((None, None, 1, D_MODEL), lambda i: (_mod_row(i, tm), j, 0, 0))
    return pl.BlockSpec((None, None, 1, D_MODEL), lambda i, k: (_mod_row(i, tm), j, 0, 0))


def _any_spec():
    return pl.BlockSpec(memory_space=pl.ANY)


def _mod_kernel(cond_ref, w_ref, b_ref, o_ref):
    o_ref[...] = _bdot(_silu(cond_ref[...]), w_ref[...]) + b_ref[...]


def _modulation(cond, w_mod, b_mod):
    tn = D_MODEL
    return pl.pallas_call(
        _mod_kernel,
        grid=(DEPTH, N_MOD * D_MODEL // tn),
        in_specs=[
            pl.BlockSpec((MOD_ROWS, D_MODEL), lambda l, j: (0, 0)),
            pl.BlockSpec((None, D_MODEL, tn), lambda l, j: (l, 0, j)),
            pl.BlockSpec((None, 1, tn), lambda l, j: (l, 0, j)),
        ],
        out_specs=pl.BlockSpec((None, MOD_ROWS, tn), lambda l, j: (l, 0, j)),
        out_shape=jax.ShapeDtypeStruct((DEPTH, MOD_ROWS, N_MOD * D_MODEL), F32),
        name="modulation",
    )(cond, w_mod, b_mod.reshape(DEPTH, 1, N_MOD * D_MODEL))


def _ffn_kernel(split_input, *refs):
    if split_input:
        xa_ref, xb_ref, *refs = refs
        x = jnp.where(pl.program_id(0) < N_CTX // TM_FFN, xa_ref[...], xb_ref[...])
    else:
        x_ref, *refs = refs
        x = x_ref[...]
    sh_ref, sc_ref, gt_ref, ng_ref, wi_ref, wo_ref, o_ref, xm_ref, act_ref = refs
    xn = _rms(x, ng_ref[...])
    xm_ref[...] = (xn * (1.0 + sc_ref[...]) + sh_ref[...]).astype(BF16)
    tk = TK_FFN
    for j in range(D_FF // tk):
        xm = xm_ref[...]
        a = jnp.dot(xm, wi_ref[:, j * tk:(j + 1) * tk], preferred_element_type=F32)
        b = jnp.dot(xm, wi_ref[:, D_FF + j * tk:D_FF + (j + 1) * tk], preferred_element_type=F32)
        act_ref[:, j * tk:(j + 1) * tk] = (_silu(a) * b).astype(BF16)
    y = jnp.dot(act_ref[...], wo_ref[...], preferred_element_type=F32)
    o_ref[...] = x + 0.5 * gt_ref[...] * y


def _ffn(xs, mod, j0, ng, w_in, w_out, layer, which):
    tm = TM_FFN
    n_ctx_tiles = N_CTX // tm
    if len(xs) == 2:
        x_specs = [pl.BlockSpec((tm, D_MODEL), lambda i: (jnp.minimum(i, n_ctx_tiles - 1), 0)),
                   pl.BlockSpec((tm, D_MODEL), lambda i: (jnp.maximum(i - n_ctx_tiles, 0), 0))]
    else:
        x_specs = [pl.BlockSpec((tm, D_MODEL), lambda i: (i, 0))]
    resident = dict(pipeline_mode=pl.Buffered(1))
    return pl.pallas_call(
        functools.partial(_ffn_kernel, len(xs) == 2),
        grid=(N_TOK // tm,),
        in_specs=x_specs + [
            _mod_spec(tm, j0, 1), _mod_spec(tm, j0 + 1, 1), _mod_spec(tm, j0 + 2, 1),
            pl.BlockSpec((None, None, 1, D_MODEL), lambda i: (layer, j0 // 3, 0, 0)),
            pl.BlockSpec((None, None, D_MODEL, 2 * D_FF), lambda i: (layer, which, 0, 0), **resident),
            pl.BlockSpec((None, None, D_FF, D_MODEL), lambda i: (layer, which, 0, 0), **resident),
        ],
        out_specs=pl.BlockSpec((tm, D_MODEL), lambda i: (i, 0)),
        out_shape=jax.ShapeDtypeStruct((N_TOK, D_MODEL), F32),
        scratch_shapes=[pltpu.VMEM((tm, D_MODEL), BF16), pltpu.VMEM((tm, D_FF), BF16)],
        compiler_params=pltpu.CompilerParams(dimension_semantics=("parallel",),
                                             vmem_limit_bytes=FFN_VMEM_BYTES),
        name="ffn",
    )(*xs, mod, mod, mod, ng, w_in, w_out)


def _proj_kernel(x_ref, sh_ref, sc_ref, ng_ref, w_ref, wq_ref, o_ref):
    xn = _rms(x_ref[...], ng_ref[...])
    xm = (xn * (1.0 + sc_ref[...]) + sh_ref[...]).astype(BF16)

    def put(col, w):
        o_ref[:, col:col + w.shape[1]] = jnp.dot(xm, w, preferred_element_type=F32)

    put(COL_HY, w_ref[:, REF_HY:REF_QC])
    put(COL_QA, wq_ref[:, 0:GROUP_WIDTH])
    put(COL_KA, w_ref[:, REF_KA:REF_HY])
    put(COL_QC, wq_ref[:, GROUP_WIDTH:2 * GROUP_WIDTH])
    put(COL_KC, w_ref[:, REF_KC:COL_GD])
    put(COL_GD, w_ref[:, COL_GD:REF_END])
    o_ref[:, REF_END:PROJ_WIDTH] = jnp.zeros((x_ref.shape[0], PROJ_WIDTH - REF_END), F32)


def _in_proj(h, mod, ng, w, wq, layer):
    tm = TM_PROJ
    return pl.pallas_call(
        _proj_kernel,
        grid=(N_TOK // tm,),
        in_specs=[
            pl.BlockSpec((tm, D_MODEL), lambda i: (i, 0)),
            _mod_spec(tm, 3, 1), _mod_spec(tm, 4, 1),
            pl.BlockSpec((None, None, 1, D_MODEL), lambda i: (layer, 1, 0, 0)),
            pl.BlockSpec((None, D_MODEL, REF_END), lambda i: (layer, 0, 0)),
            pl.BlockSpec((None, D_MODEL, 2 * GROUP_WIDTH), lambda i: (layer, 0, 0)),
        ],
        out_specs=pl.BlockSpec((tm, PROJ_WIDTH), lambda i: (i, 0)),
        out_shape=jax.ShapeDtypeStruct((N_TOK, PROJ_WIDTH), F32),
        compiler_params=pltpu.CompilerParams(dimension_semantics=("parallel",)),
        name="in_proj",
    )(h, mod, mod, ng, w, wq)


def _rope_tables(length):
    rows = length // GRID_W
    r = np.repeat(np.arange(rows, dtype=np.float32), GRID_W)
    col = np.tile(np.arange(GRID_W, dtype=np.float32), rows)
    nf = HEAD_DIM // 4
    inv = (np.float32(ROPE_THETA) ** (-np.arange(nf, dtype=np.float32) / nf)).astype(np.float32)
    ang = np.concatenate([r[:, None] * inv, col[:, None] * inv], axis=-1).astype(np.float32)
    cos, sin = np.cos(ang).astype(np.float32), np.sin(ang).astype(np.float32)
    return np.tile(cos, (1, 4)), np.tile(np.concatenate([-sin, sin], axis=-1), (1, 2))


def _pair_lanes(rows):
    lane = lax.broadcasted_iota(jnp.int32, (rows, 2 * HEAD_DIM), 1)
    return lane < HEAD_DIM, (lane % HEAD_DIM) < HEAD_DIM // 2


def _rope_pair(x, cos, sin_signed, first_half):
    partner = jnp.where(first_half, pltpu.roll(x, 3 * HEAD_DIM // 2, 1), pltpu.roll(x, HEAD_DIM // 2, 1))
    return x * cos + partner * sin_signed


def _rms_pair(x, g, lo):
    x2 = x * x
    s_lo = jnp.sum(jnp.where(lo, x2, 0.0), axis=-1, keepdims=True)
    s_hi = jnp.sum(jnp.where(lo, 0.0, x2), axis=-1, keepdims=True)
    ms = jnp.where(lo, s_lo, s_hi) * (1.0 / HEAD_DIM)
    return x * lax.rsqrt(ms + EPS) * g


def _attn_core(q, k_bf, v_bf, lo, *, mask=None, ctx=None, sinks=None):
    tq = q.shape[0]
    pair = 2 * HEAD_DIM
    scale = HEAD_DIM ** -0.5
    qa, qb = q[:, :pair] * scale, q[:, pair:] * scale
    qs = jnp.concatenate([jnp.where(lo, qa, 0.0), jnp.where(lo, 0.0, qa),
                          jnp.where(lo, qb, 0.0), jnp.where(lo, 0.0, qb)], axis=0).astype(BF16)
    s = _bdot_nt(qs, k_bf)
    if ctx is not None:
        s_c = _bdot_nt(qs, ctx[0])
    ps, pcs, dens = [], [], []
    for j in range(N_HEADS):
        rows = slice(j * tq, (j + 1) * tq)
        sj = s[rows]
        if mask is not None:
            sj = jnp.where(mask, sj, -1e30)
        m = jnp.max(sj, axis=-1, keepdims=True)
        if ctx is not None:
            m = jnp.maximum(m, jnp.max(s_c[rows], axis=-1, keepdims=True))
        if sinks is not None:
            m = jnp.maximum(m, sinks[j])
        p = jnp.exp(sj - m)
        den = jnp.sum(p, axis=-1, keepdims=True)
        ps.append(p.astype(BF16))
        if ctx is not None:
            pc = jnp.exp(s_c[rows] - m)
            den = den + jnp.sum(pc, axis=-1, keepdims=True)
            pcs.append(pc.astype(BF16))
        if sinks is not None:
            den = den + jnp.exp(sinks[j] - m)
        dens.append(den)
    r = jnp.dot(jnp.concatenate(ps, axis=0), v_bf, preferred_element_type=F32)
    if ctx is not None:
        r = r + jnp.dot(jnp.concatenate(pcs, axis=0), ctx[1], preferred_element_type=F32)
    o = [r[j * tq:(j + 1) * tq] / dens[j] for j in range(N_HEADS)]
    return jnp.concatenate([jnp.where(lo, o[0], pltpu.roll(o[2], HEAD_DIM, 1)),
                            jnp.where(lo, pltpu.roll(o[1], HEAD_DIM, 1), o[3])], axis=1)


def _attn_ctx_kernel(names, *refs):
    r = dict(zip(names, refs))
    lo, _ = _pair_lanes(SEQ)
    sinks = [r['sink'][h] for h in HEAD_ORDER] if 'sink' in r else None
    for bb in range(NB_CTX):
        rows = slice(bb * SEQ, (bb + 1) * SEQ)
        q, k, v = r['q'][rows, :], r['k'][rows, :], r['v'][rows, :]
        if 'qkg' in r:
            gq, gk = r['qkg'][0:1, :], r['qkg'][1:2, :]
            q = jnp.concatenate([_rms_pair(q[:, :KV_WIDTH], gq, lo), _rms_pair(q[:, KV_WIDTH:], gq, lo)], axis=1)
            k = _rms_pair(k, gk, lo)
        r['k_out'][bb] = k
        r['v_out'][bb] = v
        r['o'][rows, :] = _attn_core(q, k.astype(BF16), v.astype(BF16), lo, sinks=sinks)


def _attn_lat_kernel(names, window, *refs):
    r = dict(zip(names, refs))
    tq = TQ_ATTN
    i = pl.program_id(1)
    lo, first_half = _pair_lanes(tq)
    qk_norm = 'qkg' in r

    @pl.when(i == 0)
    def _():
        lo_k, first_half_k = _pair_lanes(DEC_SEQ)
        k = r['k'][...]
        if qk_norm:
            k = _rms_pair(k, r['qkg'][1:2, :], lo_k)
        r['k_scr'][...] = _rope_pair(k, r['cos_k'][...], r['sin_k'][...], first_half_k).astype(BF16)

    q = r['q'][...]
    halves = []
    for c0 in (0, KV_WIDTH):
        x = q[:, c0:c0 + KV_WIDTH]
        if qk_norm:
            x = _rms_pair(x, r['qkg'][0:1, :], lo)
        halves.append(_rope_pair(x, r['cos_q'][...], r['sin_q'][...], first_half))
    q = jnp.concatenate(halves, axis=1)
    sinks = [r['sink'][h] for h in HEAD_ORDER] if 'sink' in r else None
    ctx = (r['kc'][...].astype(BF16), r['vc'][...].astype(BF16))
    if window:
        span = tq + 2 * WINDOW
        start = pl.multiple_of(jnp.clip(i * tq - WINDOW, 0, DEC_SEQ - span), WINDOW)
        qpos = i * tq + lax.broadcasted_iota(jnp.int32, (tq, span), 0)
        kpos = start + lax.broadcasted_iota(jnp.int32, (tq, span), 1)
        mask = jnp.abs(qpos - kpos) <= WINDOW
        k_bf = r['k_scr'][pl.ds(start, span), :]
        v_bf = r['v'][pl.ds(start, span), :].astype(BF16)
    else:
        mask = None
        k_bf = r['k_scr'][...]
        v_bf = r['v'][...].astype(BF16)
    r['o'][...] = _attn_core(q, k_bf, v_bf, lo, mask=mask, ctx=ctx, sinks=sinks)


def _attention_context(u, layer, cache_prev, *, col_q, col_k, col_v, sink=None, qkg2=None):
    rows = NB_CTX * SEQ
    qb, kb, vb = col_q // GROUP_WIDTH, col_k // KV_WIDTH, col_v // KV_WIDTH
    names = ['q', 'k', 'v']
    args = [u, u, u]
    in_specs = [pl.BlockSpec((rows, GROUP_WIDTH), lambda b: (b, qb)),
                pl.BlockSpec((rows, KV_WIDTH), lambda b: (b, kb)),
                pl.BlockSpec((rows, KV_WIDTH), lambda b: (b, vb))]
    if sink is not None:
        names.append('sink'); args.append(sink)
        in_specs.append(pl.BlockSpec(memory_space=pltpu.SMEM))
    if qkg2 is not None:
        names.append('qkg'); args.append(qkg2)
        in_specs.append(pl.BlockSpec((2, KV_WIDTH), lambda b: (0, 0)))
    aliases = {}
    if cache_prev is not None:
        for j, prev in enumerate(cache_prev):
            aliases[len(args)] = 1 + j
            names.append(f'prev{j}'); args.append(prev); in_specs.append(_any_spec())
    names += ['o', 'k_out', 'v_out']
    cache_shape = jax.ShapeDtypeStruct((BATCH, DEPTH, SEQ, KV_WIDTH), F32)
    cache_spec = pl.BlockSpec((NB_CTX, None, SEQ, KV_WIDTH), lambda b: (b, layer, 0, 0))
    return pl.pallas_call(
        functools.partial(_attn_ctx_kernel, tuple(names)),
        grid=(BATCH // NB_CTX,), in_specs=in_specs,
        out_specs=(pl.BlockSpec((rows, GROUP_WIDTH), lambda b: (b, 0)), cache_spec, cache_spec),
        out_shape=(jax.ShapeDtypeStruct((N_TOK, GROUP_WIDTH), F32), cache_shape, cache_shape),
        input_output_aliases=aliases,
        compiler_params=pltpu.CompilerParams(dimension_semantics=("parallel",)),
        name="attn_context",
    )(*args)


def _attention_latent(u, prev, *, col_q, col_k, col_v, cache_k, cache_v, window=False, sink=None, qkg2=None):
    tq = TQ_ATTN
    nq = DEC_SEQ // tq
    row0_q, row0_k = N_CTX // tq, N_CTX // DEC_SEQ
    qb, kb, vb = col_q // GROUP_WIDTH, col_k // KV_WIDTH, col_v // KV_WIDTH
    cos, sin = _rope_tables(DEC_SEQ)
    const = lambda b, i: (0, 0)
    names = ['q', 'k', 'v', 'kc', 'vc', 'cos_q', 'sin_q', 'cos_k', 'sin_k']
    args = [u, u, u, cache_k, cache_v, cos, sin, cos, sin]
    in_specs = [
        pl.BlockSpec((tq, GROUP_WIDTH), lambda b, i: (row0_q + b * nq + i, qb)),
        pl.BlockSpec((DEC_SEQ, KV_WIDTH), lambda b, i: (row0_k + b, kb)),
        pl.BlockSpec((DEC_SEQ, KV_WIDTH), lambda b, i: (row0_k + b, vb)),
        pl.BlockSpec((None, PAST_LEN, KV_WIDTH), lambda b, i: (b, 0, 0)),
        pl.BlockSpec((None, PAST_LEN, KV_WIDTH), lambda b, i: (b, 0, 0)),
        pl.BlockSpec((tq, KV_WIDTH), lambda b, i: (i, 0)),
        pl.BlockSpec((tq, KV_WIDTH), lambda b, i: (i, 0)),
        pl.BlockSpec((DEC_SEQ, KV_WIDTH), const),
        pl.BlockSpec((DEC_SEQ, KV_WIDTH), const),
    ]
    if sink is not None:
        names.append('sink'); args.append(sink)
        in_specs.append(pl.BlockSpec(memory_space=pltpu.SMEM))
    if qkg2 is not None:
        names.append('qkg'); args.append(qkg2)
        in_specs.append(pl.BlockSpec((2, KV_WIDTH), const))
    aliases = {len(args): 0}
    names.append('prev'); args.append(prev); in_specs.append(_any_spec())
    names += ['o', 'k_scr']
    return pl.pallas_call(
        functools.partial(_attn_lat_kernel, tuple(names), window),
        grid=(DEC_BATCH, nq), in_specs=in_specs,
        out_specs=pl.BlockSpec((tq, GROUP_WIDTH), lambda b, i: (row0_q + b * nq + i, 0)),
        out_shape=jax.ShapeDtypeStruct((N_TOK, GROUP_WIDTH), F32),
        scratch_shapes=[pltpu.VMEM((DEC_SEQ, KV_WIDTH), BF16)],
        input_output_aliases=aliases,
        compiler_params=pltpu.CompilerParams(dimension_semantics=("parallel", "arbitrary")),
        name="attn_latent",
    )(*args)


def _dft_matrices(length):
    n = length
    k = np.arange(n, dtype=np.int64)[:, None]
    s = np.arange(n, dtype=np.int64)[None, :]
    ang = np.pi * ((k * s) % (2 * n)).astype(np.float64) / n
    fwd_cos = np.cos(ang)
    fwd_sin = -np.sin(ang)
    fwd_sin[0, :] = 1.0 - 2.0 * (np.arange(n) % 2)
    fwd = np.concatenate([fwd_cos, fwd_sin], axis=0)
    wk = np.full((n,), 2.0)
    wk[0] = 1.0
    inv_cos = (np.cos(ang) * wk[:, None]).T / (2 * n)
    inv_sin = (-2.0 * np.sin(ang)).T / (2 * n)
    inv_sin[:, 0] = (1.0 - 2.0 * (np.arange(n) % 2)) / (2 * n)
    inv = np.concatenate([inv_cos, inv_sin], axis=1)
    return fwd.astype(np.float32), inv.astype(np.float32)


def _filter_features(length):
    t = np.linspace(0.0, 1.0, length, dtype=np.float32)[:, None]
    w = (np.float32(2.0 * math.pi / length) * np.arange(length, dtype=np.float32))[:, None]
    bands = np.linspace(1e-4, HY_BANDS - 1, HY_BANDS, dtype=np.float32)[None, :]
    z = np.concatenate([t, np.cos(bands * w), -np.sin(bands * w)], axis=-1).astype(np.float32)
    zp = np.zeros((length, 128), np.float32)
    zp[:, :HY_EMB] = z
    deltas = np.linspace(math.log(HY_TARGET) / HY_SLOW, math.log(HY_TARGET) / HY_FAST, HY_CH, dtype=np.float32)
    decay = np.exp(-t * np.abs(deltas)).astype(np.float32)
    return zp, decay


def _filter_kernel(z_ref, w1_ref, b1_ref, w2_ref, b2_ref, w3_ref, fr_ref, dec_ref, f_ref, kr_ref, ki_ref):
    n = z_ref.shape[0]
    h = jnp.sin(fr_ref[0:1, :] * (_fdot(z_ref[...], w1_ref[...]) + b1_ref[...]))
    h = jnp.sin(fr_ref[1:2, :] * (_fdot(h, w2_ref[...]) + b2_ref[...]))
    h = _fdot(h, w3_ref[...])
    dec = dec_ref[...]
    half = HY_ORDER * HY_CH
    pos = jnp.concatenate([h[:, j * HY_CH:(j + 1) * HY_CH] * dec for j in range(HY_ORDER)], axis=-1)
    neg = jnp.concatenate([h[:, half + j * HY_CH:half + (j + 1) * HY_CH] * dec for j in range(HY_ORDER)], axis=-1)
    row = lax.broadcasted_iota(jnp.int32, (n, half), 0)
    neg = jnp.where(row == 0, 0.0, neg)
    fwd = f_ref[...]
    a = jnp.dot(fwd, (pos + neg).astype(BF16), preferred_element_type=F32)
    b = jnp.dot(fwd, (pos - neg).astype(BF16), preferred_element_type=F32)
    kr_ref[...] = a[:n]
    ki_ref[...] = jnp.where(row == 0, a[n:], b[n:])


def _hyena_filters(length, fwd, w1p, b1, w2, b2, w3, freq):
    zp, decay = _filter_features(length)
    half = HY_ORDER * HY_CH
    return pl.pallas_call(
        _filter_kernel,
        out_shape=(jax.ShapeDtypeStruct((length, half), F32), jax.ShapeDtypeStruct((length, half), F32)),
        name="hyena_filters",
    )(zp, w1p, b1.reshape(1, HY_FH), w2, b2.reshape(1, HY_FH), w3, freq, decay, fwd)


def _hyena_kernel(z_ref, cw_ref, cb_ref, f_ref, g_ref, kr_ref, ki_ref, bias_ref, o_ref, *, n):
    nb = z_ref.shape[0] // n
    row = lax.broadcasted_iota(jnp.int32, (n, 3 * HY_CH), 0)
    row0 = lax.broadcasted_iota(jnp.int32, (n, HY_CH), 0) == 0
    vs, x1s, x2s = [], [], []
    for bb in range(nb):
        z = z_ref[bb * n:(bb + 1) * n, :]
        prev = jnp.where(row == 0, 0.0, pltpu.roll(z, 1, 0))
        nxt = jnp.where(row == n - 1, 0.0, pltpu.roll(z, n - 1, 0))
        z = prev * cw_ref[0:1, :] + z * cw_ref[1:2, :] + nxt * cw_ref[2:3, :] + cb_ref[...]
        vs.append(z[:, :HY_CH])
        x1s.append(z[:, HY_CH:2 * HY_CH])
        x2s.append(z[:, 2 * HY_CH:])

    def long_conv(xs, order):
        cols = slice(order * HY_CH, (order + 1) * HY_CH)
        kr, ki = kr_ref[:, cols], ki_ref[:, cols]
        spec = jnp.dot(f_ref[...], jnp.concatenate([x.astype(BF16) for x in xs], axis=1),
                       preferred_element_type=F32)
        prods = []
        for bb in range(nb):
            ur, ui = spec[:n, bb * HY_CH:(bb + 1) * HY_CH], spec[n:, bb * HY_CH:(bb + 1) * HY_CH]
            yr = ur * kr - jnp.where(row0, 0.0, ui * ki)
            yi = jnp.where(row0, ui * ki, ur * ki + ui * kr)
            prods.append(jnp.concatenate([yr, yi], axis=0).astype(BF16))
        y = jnp.dot(g_ref[...], jnp.concatenate(prods, axis=1), preferred_element_type=F32)
        return [y[:, bb * HY_CH:(bb + 1) * HY_CH] + xs[bb] * bias_ref[order:order + 1, :] for bb in range(nb)]

    ys = long_conv(vs, 0)
    ys = long_conv([x1s[bb] * ys[bb] for bb in range(nb)], 1)
    for bb in range(nb):
        o_ref[bb * n:(bb + 1) * n, :] = x2s[bb] * ys[bb]


def _hyena(u, prev, *, latent, fwd, inv, kr, ki, conv_w, conv_b, bias):
    n = DEC_SEQ if latent else SEQ
    nb = 1 if latent else NB_HYENA
    steps = DEC_BATCH if latent else BATCH // nb
    row0 = N_CTX // n if latent else 0
    half = HY_ORDER * HY_CH
    const = lambda b: (0, 0)
    args = [u, conv_w, conv_b.reshape(1, 3 * HY_CH), fwd, inv, kr, ki, bias]
    in_specs = [
        pl.BlockSpec((nb * n, 3 * HY_CH), lambda b: (row0 + b, COL_HY // (3 * HY_CH))),
        pl.BlockSpec((3, 3 * HY_CH), const),
        pl.BlockSpec((1, 3 * HY_CH), const),
        pl.BlockSpec((2 * n, n), const),
        pl.BlockSpec((n, 2 * n), const),
        pl.BlockSpec((n, half), const),
        pl.BlockSpec((n, half), const),
        pl.BlockSpec((HY_ORDER, HY_CH), const),
    ]
    aliases = {}
    if latent:
        aliases = {len(args): 0}
        args.append(prev)
        in_specs.append(_any_spec())

    def body(*refs):
        if latent:
            refs = refs[:len(args) - 1] + refs[len(args):]
        _hyena_kernel(*refs, n=n)

    return pl.pallas_call(
        body, grid=(steps,), in_specs=in_specs,
        out_specs=pl.BlockSpec((nb * n, HY_CH), lambda b: (row0 + b, 0)),
        out_shape=jax.ShapeDtypeStruct((N_TOK, HY_CH), F32),
        input_output_aliases=aliases,
        compiler_params=pltpu.CompilerParams(dimension_semantics=("parallel",)),
        name="hyena_latent" if latent else "hyena_context",
    )(*args)


def _gla_masks():
    t = np.arange(ROWS_GROUP)[:, None]
    s = np.arange(ROWS_GROUP)[None, :]
    same = (t // GLA_CHUNK) == (s // GLA_CHUNK)
    return np.stack([same & (s <= t), same & (s >= t), same]).astype(np.float32)


def _gla_group(r, rows, z, st):
    c = GLA_CHUNK
    hk = H_GLA * DK_GLA
    n_chunks = ROWS_GROUP // c
    logits = _bdot(r['gl'][rows, :], r['gw'][z]) + r['gb'][z:z + 1, :]
    g = (jnp.minimum(logits, 0.0) - jnp.log1p(jnp.exp(-jnp.abs(logits)))) / GLA_NORM
    causal = r['masks'][z]
    b = _split_dot(causal.astype(BF16), g)
    tot = _split_dot(r['masks'][2].astype(BF16), g)
    k = r['k'][rows, :]
    q_t = r['q'][rows, :] * jnp.exp(b) * (DK_GLA ** -0.5)
    k_t = (k * jnp.exp(-b)).astype(BF16)
    k_e = (k * jnp.exp(tot - b)).astype(BF16)
    dec = jnp.exp(tot)
    v = r['v'][rows, :].astype(BF16)

    qlane = lax.broadcasted_iota(jnp.int32, (ROWS_GROUP, hk), 1) // DK_GLA
    vlane = lax.broadcasted_iota(jnp.int32, (ROWS_GROUP, GROUP_WIDTH), 1) // DV_GLA
    qs = jnp.concatenate([jnp.where(qlane == h, q_t, 0.0) for h in range(H_GLA)], axis=0).astype(BF16)
    s = _bdot_nt(qs, k_t)
    valid = causal != 0.0
    p = jnp.concatenate([jnp.where(valid, s[h * ROWS_GROUP:(h + 1) * ROWS_GROUP], 0.0).astype(BF16)
                         for h in range(H_GLA)], axis=0)
    res = jnp.dot(p, v, preferred_element_type=F32)
    o_intra = jnp.where(vlane == 0, res[0:ROWS_GROUP], 0.0)
    for h in range(1, H_GLA):
        o_intra = jnp.where(vlane == h, res[h * ROWS_GROUP:(h + 1) * ROWS_GROUP], o_intra)

    bd = (lax.broadcasted_iota(jnp.int32, (GROUP_WIDTH, hk), 0) // DV_GLA
          == lax.broadcasted_iota(jnp.int32, (GROUP_WIDTH, hk), 1) // DK_GLA)
    q_bf = q_t.astype(BF16)
    o_inter = [None] * n_chunks
    for step in range(n_chunks):
        ci = step if z == 0 else n_chunks - 1 - step
        cr = slice(ci * c, (ci + 1) * c)
        o_inter[ci] = _bdot_nt(q_bf[cr], st)
        st = st * dec[ci * c:ci * c + 1, :] + jnp.where(bd, _bdot_tn(v[cr], k_e[cr]), 0.0)
    return o_intra + jnp.concatenate(o_inter, axis=0), st


def _head_rms(o, g):
    hb = (lax.broadcasted_iota(jnp.int32, (GROUP_WIDTH, GROUP_WIDTH), 0) // DV_GLA
          == lax.broadcasted_iota(jnp.int32, (GROUP_WIDTH, GROUP_WIDTH), 1) // DV_GLA)
    ms = _split_dot_rhs(o * o, hb.astype(BF16)) * (1.0 / DV_GLA)
    return o * lax.rsqrt(ms + EPS) * g


def _split_dot_rhs(x, m):
    hi = x.astype(BF16)
    lo = (x - hi.astype(F32)).astype(BF16)
    return jnp.dot(hi, m, preferred_element_type=F32) + jnp.dot(lo, m, preferred_element_type=F32)


def _expand_state(s):
    hk = H_GLA * DK_GLA
    bd = (lax.broadcasted_iota(jnp.int32, (GROUP_WIDTH, hk), 0) // DV_GLA
          == lax.broadcasted_iota(jnp.int32, (GROUP_WIDTH, hk), 1) // DK_GLA)
    return jnp.where(bd, jnp.concatenate([s] * H_GLA, axis=0), 0.0)


def _compact_state(st):
    out = st[0:DV_GLA]
    for h in range(1, H_GLA):
        out = out + st[h * DV_GLA:(h + 1) * DV_GLA]
    return out


def _gla_ctx_kernel(names, *refs):
    r = dict(zip(names, refs))
    hk = H_GLA * DK_GLA
    for bb in range(NB_CTX):
        rows = slice(bb * SEQ, (bb + 1) * SEQ)
        zero = jnp.zeros((GROUP_WIDTH, hk), F32)
        o_f, st_f = _gla_group(r, rows, 0, zero)
        o_b, st_b = _gla_group(r, rows, 1, zero)
        r['o'][rows, :] = _head_rms(o_f + o_b, r['mg'][...])
        r['sf'][bb, 0] = _compact_state(st_f)
        r['sf'][bb, 1] = _compact_state(st_b)


def _gla_lat_kernel(names, *refs):
    r = dict(zip(names, refs))
    n_groups = DEC_SEQ // ROWS_GROUP

    def run(z):
        r['st'][...] = _expand_state(r['s0'][z])

        def body(step, carry):
            gi = step if z == 0 else n_groups - 1 - step
            rows = pl.ds(pl.multiple_of(gi * ROWS_GROUP, ROWS_GROUP), ROWS_GROUP)
            o, st = _gla_group(r, rows, z, r['st'][...])
            r['st'][...] = st
            if z == 0:
                r['o'][rows, :] = o
            else:
                r['o'][rows, :] = _head_rms(r['o'][rows, :] + o, r['mg'][...])
            return carry

        lax.fori_loop(0, n_groups, body, 0)

    run(0)
    run(1)


def _gla(u, prev, *, latent, gate_w, gate_b, mix_g_d, s0=None):
    n = DEC_SEQ if latent else NB_CTX * SEQ
    steps = DEC_BATCH if latent else BATCH // NB_CTX
    row0 = N_CTX // n if latent else 0
    hk = H_GLA * DK_GLA
    const2 = lambda b: (0, 0)
    names = ['q', 'k', 'v', 'gl', 'gw', 'gb', 'mg', 'masks']
    args = [u, u, u, u, gate_w, gate_b, mix_g_d, _gla_masks()]
    in_specs = [
        pl.BlockSpec((n, hk), lambda b: (row0 + b, COL_QD // hk)),
        pl.BlockSpec((n, hk), lambda b: (row0 + b, COL_KD // hk)),
        pl.BlockSpec((n, GROUP_WIDTH), lambda b: (row0 + b, COL_VD // GROUP_WIDTH)),
        pl.BlockSpec((n, 128), lambda b: (row0 + b, COL_GD // 128)),
        pl.BlockSpec((2, 128, hk), lambda b: (0, 0, 0)),
        pl.BlockSpec((2, hk), const2),
        pl.BlockSpec((1, GROUP_WIDTH), const2),
        pl.BlockSpec((3, ROWS_GROUP, ROWS_GROUP), lambda b: (0, 0, 0)),
    ]
    out_shape = jax.ShapeDtypeStruct((N_TOK, GROUP_WIDTH), F32)
    out_specs = pl.BlockSpec((n, GROUP_WIDTH), lambda b: (row0 + b, 0))
    aliases, scratch = {}, []
    if latent:
        names.append('s0'); args.append(s0)
        in_specs.append(pl.BlockSpec((None, 2, DV_GLA, hk), lambda b: (b, 0, 0, 0)))
        aliases = {len(args): 0}
        names.append('prev'); args.append(prev); in_specs.append(_any_spec())
        names += ['o', 'st']
        scratch = [pltpu.VMEM((GROUP_WIDTH, hk), F32)]
        body = functools.partial(_gla_lat_kernel, tuple(names))
    else:
        names += ['o', 'sf']
        out_shape = (out_shape, jax.ShapeDtypeStruct((BATCH, 2, DV_GLA, hk), F32))
        out_specs = (out_specs, pl.BlockSpec((NB_CTX, 2, DV_GLA, hk), lambda b: (b, 0, 0, 0)))
        body = functools.partial(_gla_ctx_kernel, tuple(names))
    return pl.pallas_call(
        body, grid=(steps,), in_specs=in_specs, out_specs=out_specs, out_shape=out_shape,
        scratch_shapes=scratch, input_output_aliases=aliases,
        compiler_params=pltpu.CompilerParams(dimension_semantics=("parallel",)),
        name="gla_latent" if latent else "gla_context",
    )(*args)


def _merge_kernel(h_ref, oa_ref, ob_ref, oc_ref, od_ref, rd_ref, gt_ref, mg_ref, wo_ref, o_ref):
    gw = GROUP_WIDTH
    ys = [
        _rms(oa_ref[...], mg_ref[:, 0:gw]),
        _rms(ob_ref[...], mg_ref[:, gw:2 * gw]),
        _rms(oc_ref[...], mg_ref[:, 2 * gw:3 * gw]),
        od_ref[...] * _silu(rd_ref[:, COL_RD - COL_GD:COL_RD - COL_GD + gw]),
    ]
    mix = _bdot(ys[0], wo_ref[0:gw, :])
    for j in range(1, 4):
        mix = mix + _bdot(ys[j], wo_ref[j * gw:(j + 1) * gw, :])
    o_ref[...] = h_ref[...] + gt_ref[...] * mix


def _merge(h, oa, ob, oc, od, u, mod, mix_g, w_out, layer):
    tm = TM_PROJ
    grp = pl.BlockSpec((tm, GROUP_WIDTH), lambda i: (i, 0))
    return pl.pallas_call(
        _merge_kernel,
        grid=(N_TOK // tm,),
        in_specs=[
            pl.BlockSpec((tm, D_MODEL), lambda i: (i, 0)),
            grp, grp, grp, grp,
            pl.BlockSpec((tm, COL_RD_BLOCK), lambda i: (i, COL_GD // COL_RD_BLOCK)),
            _mod_spec(tm, 5, 1),
            pl.BlockSpec((None, 1, D_MODEL), lambda i: (layer, 0, 0)),
            pl.BlockSpec((None, D_MODEL, D_MODEL), lambda i: (layer, 0, 0)),
        ],
        out_specs=pl.BlockSpec((tm, D_MODEL), lambda i: (i, 0)),
        out_shape=jax.ShapeDtypeStruct((N_TOK, D_MODEL), F32),
        compiler_params=pltpu.CompilerParams(dimension_semantics=("parallel",)),
        name="merge",
    )(h, oa, ob, oc, od, u, mod, mix_g, w_out)


def _norm_kernel(x_ref, g_ref, o_ref):
    o_ref[...] = _rms(x_ref[...], g_ref[...])


def _final_norm(h, g, row0, rows):
    tm = TM_PROJ
    return pl.pallas_call(
        _norm_kernel,
        grid=(rows // tm,),
        in_specs=[pl.BlockSpec((tm, D_MODEL), lambda i: (row0 // tm + i, 0)),
                  pl.BlockSpec((1, D_MODEL), lambda i: (0, 0))],
        out_specs=pl.BlockSpec((tm, D_MODEL), lambda i: (i, 0)),
        out_shape=jax.ShapeDtypeStruct((rows, D_MODEL), F32),
        compiler_params=pltpu.CompilerParams(dimension_semantics=("parallel",)),
        name="final_norm",
    )(h, g.reshape(1, D_MODEL))


def _query_weights(w):
    heads = [w[..., base + HEAD_DIM * h:base + HEAD_DIM * (h + 1)]
             for base in (REF_QA, REF_QC) for h in HEAD_ORDER]
    return jnp.concatenate(heads, axis=-1)


def _gate_weights(gate_w):
    out = jnp.zeros((2, 128, H_GLA * DK_GLA), gate_w.dtype)
    out = out.at[0, 0:GLA_RANK].set(gate_w[0])
    return out.at[1, GLA_RANK:2 * GLA_RANK].set(gate_w[1])


def _states_to_kernel(st):
    b = st.shape[0]
    return st.transpose(0, 1, 4, 2, 3).reshape(b, 2, DV_GLA, H_GLA * DK_GLA)


def _states_from_kernel(st):
    b = st.shape[0]
    return st.reshape(b, 2, DV_GLA, H_GLA, DK_GLA).transpose(0, 1, 3, 4, 2)


def kernel(x_prompt, x_sample, cache_swa_k, cache_swa_v, cache_gqa_k, cache_gqa_v, state_gla, c, c_ctx, w_mod, b_mod, norm_g, ffn_w_in, ffn_w_out, w_in, w_out, mix_g, swa_sink, qk_norm_g, hy_conv_w, hy_conv_b, hy_w1, hy_b1, hy_w2, hy_b2, hy_w3, hy_freq, hy_bias, gla_gate_w, gla_gate_b, final_g):
    cond = jnp.zeros((MOD_ROWS, D_MODEL), F32).at[0].set(c_ctx).at[1:1 + DEC_BATCH].set(c)
    mod_all = _modulation(cond, w_mod, b_mod).reshape(DEPTH, MOD_ROWS, N_MOD, 1, D_MODEL)

    dft = {n: tuple(jnp.asarray(m).astype(BF16) for m in _dft_matrices(n)) for n in (SEQ, DEC_SEQ)}

    xs = [x_prompt.reshape(N_CTX, D_MODEL), x_sample.reshape(N_LAT, D_MODEL)]
    ffn_in = ffn_w_in.astype(BF16)
    ffn_out = ffn_w_out.astype(BF16)
    ng = norm_g.reshape(DEPTH, 3, 1, D_MODEL)
    proj_w = w_in.astype(BF16)
    proj_wq = _query_weights(w_in).astype(BF16)
    out_w = w_out.astype(BF16)
    mix_gp = mix_g.reshape(DEPTH, 1, D_MODEL)
    cache_a = cache_c = None
    sts = []
    for l in range(DEPTH):
        mod = mod_all[l]
        h = _ffn(xs, mod, 0, ng, ffn_in, ffn_out, l, 0)
        u = _in_proj(h, mod, ng, proj_w, proj_wq, l)

        ck_a = cache_swa_k[:, l].reshape(DEC_BATCH, PAST_LEN, KV_WIDTH)
        cv_a = cache_swa_v[:, l].reshape(DEC_BATCH, PAST_LEN, KV_WIDTH)
        ck_c = cache_gqa_k[:, l].reshape(DEC_BATCH, PAST_LEN, KV_WIDTH)
        cv_c = cache_gqa_v[:, l].reshape(DEC_BATCH, PAST_LEN, KV_WIDTH)
        qkg2 = jnp.tile(qk_norm_g[l], (1, N_KV))

        oa, *cache_a = _attention_context(u, l, cache_a, col_q=COL_QA, col_k=COL_KA, col_v=COL_VA,
                                          sink=swa_sink[l])
        oa = _attention_latent(u, oa, col_q=COL_QA, col_k=COL_KA, col_v=COL_VA, cache_k=ck_a, cache_v=cv_a,
                               window=True, sink=swa_sink[l])
        oc, *cache_c = _attention_context(u, l, cache_c, col_q=COL_QC, col_k=COL_KC, col_v=COL_VC, qkg2=qkg2)
        oc = _attention_latent(u, oc, col_q=COL_QC, col_k=COL_KC, col_v=COL_VC, cache_k=ck_c, cache_v=cv_c,
                               qkg2=qkg2)

        w1p = jnp.zeros((128, HY_FH), F32).at[:HY_EMB].set(hy_w1[l])
        ob = None
        for latent, n in ((False, SEQ), (True, DEC_SEQ)):
            fwd, inv = dft[n]
            kr, ki = _hyena_filters(n, fwd, w1p, hy_b1[l], hy_w2[l], hy_b2[l], hy_w3[l], hy_freq[l])
            ob = _hyena(u, ob, latent=latent, fwd=fwd, inv=inv, kr=kr, ki=ki, conv_w=hy_conv_w[l],
                        conv_b=hy_conv_b[l], bias=hy_bias[l])

        gw = _gate_weights(gla_gate_w[l]).astype(BF16)
        mg_d = mix_g[l, 3 * GROUP_WIDTH:].reshape(1, GROUP_WIDTH)
        od, st = _gla(u, None, latent=False, gate_w=gw, gate_b=gla_gate_b[l], mix_g_d=mg_d)
        od = _gla(u, od, latent=True, gate_w=gw, gate_b=gla_gate_b[l], mix_g_d=mg_d,
                  s0=_states_to_kernel(state_gla[:, l]))

        h = _merge(h, oa, ob, oc, od, u, mod, mix_gp, out_w, l)
        h = _ffn([h], mod, 6, ng, ffn_in, ffn_out, l, 1)
        xs = [h]
        sts.append(_states_from_kernel(st))

    y_prompt = _final_norm(h, final_g, 0, N_CTX).reshape(BATCH, SEQ, D_MODEL)
    y_sample = _final_norm(h, final_g, N_CTX, N_LAT).reshape(DEC_BATCH, DEC_SEQ, D_MODEL)
    caches = [x.reshape(BATCH, DEPTH, SEQ, N_KV, HEAD_DIM) for x in (*cache_a, *cache_c)]
    return (y_prompt, y_sample, *caches, jnp.stack(sts, axis=1))
```

```python
import functools
import math

import numpy as np
import jax
import jax.numpy as jnp
from jax import lax
from jax.experimental import pallas as pl
from jax.experimental.pallas import tpu as pltpu

F32 = jnp.float32
BF16 = jnp.bfloat16

D_MODEL = 1024
BATCH = 32
SEQ = 256
DEPTH = 2
DEC_BATCH = 2
DEC_SEQ = 1024
PAST_LEN = 512
GRID_W = 64
HEAD_DIM = 64
GROUP_WIDTH = 256
N_HEADS = 4
N_KV = 2
KV_WIDTH = N_KV * HEAD_DIM
WINDOW = 128
HY_CH = 256
HY_ORDER = 2
HY_BANDS = 16
HY_EMB = 2 * HY_BANDS + 1
HY_FH = 64
HY_TARGET = 1e-2
HY_FAST = 0.3
HY_SLOW = 1.5
H_GLA = 4
DV_GLA = 64
DK_GLA = 32
GLA_RANK = 16
GLA_CHUNK = 64
GLA_NORM = 16.0
D_FF = 2816
N_MOD = 9
ROPE_THETA = 10000.0
EPS = 1e-6

N_CTX = BATCH * SEQ
N_LAT = DEC_BATCH * DEC_SEQ
N_TOK = N_CTX + N_LAT
MOD_ROWS = 8

COL_HY, COL_QA, COL_KA, COL_VA = 0, 768, 1024, 1152
COL_QC, COL_KC, COL_VC = 1280, 1536, 1664
COL_QD, COL_KD, COL_VD, COL_GD, COL_RD = 1792, 1920, 2048, 2304, 2336
PROJ_WIDTH = 2688
REF_QA, REF_KA, REF_HY, REF_QC, REF_KC, REF_END = 0, 256, 512, 1280, 1536, 2592
COL_RD_BLOCK = 384
HEAD_ORDER = (0, 2, 1, 3)
_HEAD_PERM = np.concatenate([np.arange(HEAD_DIM) + HEAD_DIM * h for h in HEAD_ORDER])

TM_FFN = 1024
TK_FFN = 256
FFN_VMEM_BYTES = 60 * 1024 * 1024
TM_PROJ = 1024
TQ_ATTN = 256
ROWS_GROUP = 256
NB_CTX = 2
NB_HYENA = 4


def _bdot(a, b):
    return jnp.dot(a.astype(BF16), b.astype(BF16), preferred_element_type=F32)


def _bdot_nt(a, b):
    return lax.dot_general(a.astype(BF16), b.astype(BF16), (((1,), (1,)), ((), ())),
                           preferred_element_type=F32)


def _bdot_tn(a, b):
    return lax.dot_general(a.astype(BF16), b.astype(BF16), (((0,), (0,)), ((), ())),
                           preferred_element_type=F32)


def _fdot(a, b):
    return jnp.dot(a, b, preferred_element_type=F32, precision=lax.Precision.HIGHEST)


def _split_dot(m, x):
    hi = x.astype(BF16)
    lo = (x - hi.astype(F32)).astype(BF16)
    return jnp.dot(m, hi, preferred_element_type=F32) + jnp.dot(m, lo, preferred_element_type=F32)


def _rms(x, g):
    return x * lax.rsqrt(jnp.mean(x * x, axis=-1, keepdims=True) + EPS) * g


def _silu(x):
    return x * jax.nn.sigmoid(x)


def _mod_row(i, tm):
    n_ctx_tiles = N_CTX // tm
    per_batch = DEC_SEQ // tm
    return jnp.where(i < n_ctx_tiles, 0, 1 + (i - n_ctx_tiles) // per_batch)


def _mod_spec(tm, j, grid_rank):
    if grid_rank == 1:
        return pl.BlockSpec((None, None, 1, D_MODEL), lambda i: (_mod_row(i, tm), j, 0, 0))
    return pl.BlockSpec((None, None, 1, D_MODEL), lambda i, k: (_mod_row(i, tm), j, 0, 0))


def _any_spec():
    return pl.BlockSpec(memory_space=pl.ANY)


def _mod_kernel(cond_ref, w_ref, b_ref, o_ref):
    o_ref[...] = _bdot(_silu(cond_ref[...]), w_ref[...]) + b_ref[...]


def _modulation(cond, w_mod, b_mod):
    tn = D_MODEL
    return pl.pallas_call(
        _mod_kernel,
        grid=(DEPTH, N_MOD * D_MODEL // tn),
        in_specs=[
            pl.BlockSpec((MOD_ROWS, D_MODEL), lambda l, j: (0, 0)),
            pl.BlockSpec((None, D_MODEL, tn), lambda l, j: (l, 0, j)),
            pl.BlockSpec((None, 1, tn), lambda l, j: (l, 0, j)),
        ],
        out_specs=pl.BlockSpec((None, MOD_ROWS, tn), lambda l, j: (l, 0, j)),
        out_shape=jax.ShapeDtypeStruct((DEPTH, MOD_ROWS, N_MOD * D_MODEL), F32),
        name="modulation",
    )(cond, w_mod, b_mod.reshape(DEPTH, 1, N_MOD * D_MODEL))


def _ffn_kernel(split_input, final, *refs):
    is_ctx = pl.program_id(0) < N_CTX // TM_FFN
    if split_input:
        xa_ref, xb_ref, *refs = refs
        x = jnp.where(is_ctx, xa_ref[...], xb_ref[...])
    else:
        x_ref, *refs = refs
        x = x_ref[...]
    if final:
        sh_ref, sc_ref, gt_ref, ng_ref, wi_ref, wo_ref, fg_ref, yc_ref, yl_ref, xm_ref, act_ref = refs
    else:
        sh_ref, sc_ref, gt_ref, ng_ref, wi_ref, wo_ref, o_ref, xm_ref, act_ref = refs
    xn = _rms(x, ng_ref[...])
    xm_ref[...] = (xn * (1.0 + sc_ref[...]) + sh_ref[...]).astype(BF16)
    tk = TK_FFN
    for j in range(D_FF // tk):
        xm = xm_ref[...]
        a = jnp.dot(xm, wi_ref[:, j * tk:(j + 1) * tk], preferred_element_type=F32)
        b = jnp.dot(xm, wi_ref[:, D_FF + j * tk:D_FF + (j + 1) * tk], preferred_element_type=F32)
        act_ref[:, j * tk:(j + 1) * tk] = (_silu(a) * b).astype(BF16)
    y = jnp.dot(act_ref[...], wo_ref[...], preferred_element_type=F32)
    out = x + 0.5 * gt_ref[...] * y
    if not final:
        o_ref[...] = out
        return
    out = _rms(out, fg_ref[...])

    @pl.when(is_ctx)
    def _():
        yc_ref[...] = out

    @pl.when(jnp.logical_not(is_ctx))
    def _():
        yl_ref[...] = out


def _ffn(xs, mod, j0, ng, w_in, w_out, layer, which, final_g=None):
    tm = TM_FFN
    n_ctx_tiles = N_CTX // tm
    split_specs = [pl.BlockSpec((tm, D_MODEL), lambda i: (jnp.minimum(i, n_ctx_tiles - 1), 0)),
                   pl.BlockSpec((tm, D_MODEL), lambda i: (jnp.maximum(i - n_ctx_tiles, 0), 0))]
    whole_spec = pl.BlockSpec((tm, D_MODEL), lambda i: (i, 0))
    x_specs = split_specs if len(xs) == 2 else [whole_spec]
    resident = dict(pipeline_mode=pl.Buffered(1))
    final = final_g is not None
    extra_args, extra_specs = [], []
    if final:
        extra_args = [final_g.reshape(1, D_MODEL)]
        extra_specs = [pl.BlockSpec((1, D_MODEL), lambda i: (0, 0))]
        out_specs = tuple(split_specs)
        out_shape = (jax.ShapeDtypeStruct((N_CTX, D_MODEL), F32), jax.ShapeDtypeStruct((N_LAT, D_MODEL), F32))
    else:
        out_specs = whole_spec
        out_shape = jax.ShapeDtypeStruct((N_TOK, D_MODEL), F32)
    return pl.pallas_call(
        functools.partial(_ffn_kernel, len(xs) == 2, final),
        grid=(N_TOK // tm,),
        in_specs=x_specs + [
            _mod_spec(tm, j0, 1), _mod_spec(tm, j0 + 1, 1), _mod_spec(tm, j0 + 2, 1),
            pl.BlockSpec((None, None, 1, D_MODEL), lambda i: (layer, j0 // 3, 0, 0)),
            pl.BlockSpec((None, None, D_MODEL, 2 * D_FF), lambda i: (layer, which, 0, 0), **resident),
            pl.BlockSpec((None, None, D_FF, D_MODEL), lambda i: (layer, which, 0, 0), **resident),
        ] + extra_specs,
        out_specs=out_specs,
        out_shape=out_shape,
        scratch_shapes=[pltpu.VMEM((tm, D_MODEL), BF16), pltpu.VMEM((tm, D_FF), BF16)],
        compiler_params=pltpu.CompilerParams(dimension_semantics=("arbitrary",),
                                             vmem_limit_bytes=FFN_VMEM_BYTES),
        name="ffn",
    )(*xs, mod, mod, mod, ng, w_in, w_out, *extra_args)


def _proj_kernel(x_ref, sh_ref, sc_ref, ng_ref, w_ref, wq_ref, o_ref):
    xn = _rms(x_ref[...], ng_ref[...])
    xm = (xn * (1.0 + sc_ref[...]) + sh_ref[...]).astype(BF16)

    def put(col, w):
        o_ref[:, col:col + w.shape[1]] = jnp.dot(xm, w, preferred_element_type=F32)

    put(COL_HY, w_ref[:, REF_HY:REF_QC])
    put(COL_QA, wq_ref[:, 0:GROUP_WIDTH])
    put(COL_KA, w_ref[:, REF_KA:REF_HY])
    put(COL_QC, wq_ref[:, GROUP_WIDTH:2 * GROUP_WIDTH])
    put(COL_KC, w_ref[:, REF_KC:COL_GD])
    put(COL_GD, w_ref[:, COL_GD:REF_END])
    o_ref[:, REF_END:PROJ_WIDTH] = jnp.zeros((x_ref.shape[0], PROJ_WIDTH - REF_END), F32)


def _in_proj(h, mod, ng, w, wq, layer):
    tm = TM_PROJ
    return pl.pallas_call(
        _proj_kernel,
        grid=(N_TOK // tm,),
        in_specs=[
            pl.BlockSpec((tm, D_MODEL), lambda i: (i, 0)),
            _mod_spec(tm, 3, 1), _mod_spec(tm, 4, 1),
            pl.BlockSpec((None, None, 1, D_MODEL), lambda i: (layer, 1, 0, 0)),
            pl.BlockSpec((None, D_MODEL, REF_END), lambda i: (layer, 0, 0)),
            pl.BlockSpec((None, D_MODEL, 2 * GROUP_WIDTH), lambda i: (layer, 0, 0)),
        ],
        out_specs=pl.BlockSpec((tm, PROJ_WIDTH), lambda i: (i, 0)),
        out_shape=jax.ShapeDtypeStruct((N_TOK, PROJ_WIDTH), F32),
        compiler_params=pltpu.CompilerParams(dimension_semantics=("parallel",)),
        name="in_proj",
    )(h, mod, mod, ng, w, wq)


def _rope_tables(length):
    rows = length // GRID_W
    r = np.repeat(np.arange(rows, dtype=np.float32), GRID_W)
    col = np.tile(np.arange(GRID_W, dtype=np.float32), rows)
    nf = HEAD_DIM // 4
    inv = (np.float32(ROPE_THETA) ** (-np.arange(nf, dtype=np.float32) / nf)).astype(np.float32)
    ang = np.concatenate([r[:, None] * inv, col[:, None] * inv], axis=-1).astype(np.float32)
    cos, sin = np.cos(ang).astype(np.float32), np.sin(ang).astype(np.float32)
    return np.tile(cos, (1, 4)), np.tile(np.concatenate([-sin, sin], axis=-1), (1, 2))


def _pair_lanes(rows):
    lane = lax.broadcasted_iota(jnp.int32, (rows, 2 * HEAD_DIM), 1)
    return lane < HEAD_DIM, (lane % HEAD_DIM) < HEAD_DIM // 2


def _rope_pair(x, cos, sin_signed, first_half):
    partner = jnp.where(first_half, pltpu.roll(x, 3 * HEAD_DIM // 2, 1), pltpu.roll(x, HEAD_DIM // 2, 1))
    return x * cos + partner * sin_signed


def _rms_pair(x, g, lo):
    x2 = x * x
    s_lo = jnp.sum(jnp.where(lo, x2, 0.0), axis=-1, keepdims=True)
    s_hi = jnp.sum(jnp.where(lo, 0.0, x2), axis=-1, keepdims=True)
    ms = jnp.where(lo, s_lo, s_hi) * (1.0 / HEAD_DIM)
    return x * lax.rsqrt(ms + EPS) * g


def _attn_core(q, k_bf, v_bf, lo, *, mask=None, ctx=None, sinks=None):
    tq = q.shape[0]
    pair = 2 * HEAD_DIM
    scale = HEAD_DIM ** -0.5
    qa, qb = q[:, :pair] * scale, q[:, pair:] * scale
    qs = jnp.concatenate([jnp.where(lo, qa, 0.0), jnp.where(lo, 0.0, qa),
                          jnp.where(lo, qb, 0.0), jnp.where(lo, 0.0, qb)], axis=0).astype(BF16)
    s = _bdot_nt(qs, k_bf)
    if ctx is not None:
        s_c = _bdot_nt(qs, ctx[0])
    ps, pcs, dens = [], [], []
    for j in range(N_HEADS):
        rows = slice(j * tq, (j + 1) * tq)
        sj = s[rows]
        if mask is not None:
            sj = jnp.where(mask, sj, -1e30)
        m = jnp.max(sj, axis=-1, keepdims=True)
        if ctx is not None:
            m = jnp.maximum(m, jnp.max(s_c[rows], axis=-1, keepdims=True))
        if sinks is not None:
            m = jnp.maximum(m, sinks[j])
        p = jnp.exp(sj - m)
        den = jnp.sum(p, axis=-1, keepdims=True)
        ps.append(p.astype(BF16))
        if ctx is not None:
            pc = jnp.exp(s_c[rows] - m)
            den = den + jnp.sum(pc, axis=-1, keepdims=True)
            pcs.append(pc.astype(BF16))
        if sinks is not None:
            den = den + jnp.exp(sinks[j] - m)
        dens.append(den)
    r = jnp.dot(jnp.concatenate(ps, axis=0), v_bf, preferred_element_type=F32)
    if ctx is not None:
        r = r + jnp.dot(jnp.concatenate(pcs, axis=0), ctx[1], preferred_element_type=F32)
    o = [r[j * tq:(j + 1) * tq] / dens[j] for j in range(N_HEADS)]
    return jnp.concatenate([jnp.where(lo, o[0], o[1]), jnp.where(lo, o[2], o[3])], axis=1)


def _attn_ctx_kernel(names, *refs):
    r = dict(zip(names, refs))
    lo, _ = _pair_lanes(SEQ)
    sinks = [r['sink'][h] for h in HEAD_ORDER] if 'sink' in r else None
    for bb in range(NB_CTX):
        rows = slice(bb * SEQ, (bb + 1) * SEQ)
        q, k, v = r['q'][rows, :], r['k'][rows, :], r['v'][rows, :]
        if 'qkg' in r:
            gq, gk = r['qkg'][0:1, :], r['qkg'][1:2, :]
            q = jnp.concatenate([_rms_pair(q[:, :KV_WIDTH], gq, lo), _rms_pair(q[:, KV_WIDTH:], gq, lo)], axis=1)
            k = _rms_pair(k, gk, lo)
        r['k_out'][bb] = k
        r['v_out'][bb] = v
        r['o'][rows, :] = _attn_core(q, k.astype(BF16), v.astype(BF16), lo, sinks=sinks)


def _attn_lat_kernel(names, window, *refs):
    r = dict(zip(names, refs))
    tq = TQ_ATTN
    i = pl.program_id(1)
    lo, first_half = _pair_lanes(tq)
    qk_norm = 'qkg' in r

    @pl.when(i == 0)
    def _():
        lo_k, first_half_k = _pair_lanes(DEC_SEQ)
        k = r['k'][...]
        if qk_norm:
            k = _rms_pair(k, r['qkg'][1:2, :], lo_k)
        r['k_scr'][...] = _rope_pair(k, r['cos_k'][...], r['sin_k'][...], first_half_k).astype(BF16)

    q = r['q'][...]
    halves = []
    for c0 in (0, KV_WIDTH):
        x = q[:, c0:c0 + KV_WIDTH]
        if qk_norm:
            x = _rms_pair(x, r['qkg'][0:1, :], lo)
        halves.append(_rope_pair(x, r['cos_q'][...], r['sin_q'][...], first_half))
    q = jnp.concatenate(halves, axis=1)
    sinks = [r['sink'][h] for h in HEAD_ORDER] if 'sink' in r else None
    ctx = (r['kc'][...].astype(BF16), r['vc'][...].astype(BF16))
    if window:
        span = tq + 2 * WINDOW
        start = pl.multiple_of(jnp.clip(i * tq - WINDOW, 0, DEC_SEQ - span), WINDOW)
        qpos = i * tq + lax.broadcasted_iota(jnp.int32, (tq, span), 0)
        kpos = start + lax.broadcasted_iota(jnp.int32, (tq, span), 1)
        mask = jnp.abs(qpos - kpos) <= WINDOW
        k_bf = r['k_scr'][pl.ds(start, span), :]
        v_bf = r['v'][pl.ds(start, span), :].astype(BF16)
    else:
        mask = None
        k_bf = r['k_scr'][...]
        v_bf = r['v'][...].astype(BF16)
    r['o'][...] = _attn_core(q, k_bf, v_bf, lo, mask=mask, ctx=ctx, sinks=sinks)


def _attention_context(u, layer, cache_prev, *, col_q, col_k, col_v, sink=None, qkg2=None):
    rows = NB_CTX * SEQ
    qb, kb, vb = col_q // GROUP_WIDTH, col_k // KV_WIDTH, col_v // KV_WIDTH
    names = ['q', 'k', 'v']
    args = [u, u, u]
    in_specs = [pl.BlockSpec((rows, GROUP_WIDTH), lambda b: (b, qb)),
                pl.BlockSpec((rows, KV_WIDTH), lambda b: (b, kb)),
                pl.BlockSpec((rows, KV_WIDTH), lambda b: (b, vb))]
    if sink is not None:
        names.append('sink'); args.append(sink)
        in_specs.append(pl.BlockSpec(memory_space=pltpu.SMEM))
    if qkg2 is not None:
        names.append('qkg'); args.append(qkg2)
        in_specs.append(pl.BlockSpec((2, KV_WIDTH), lambda b: (0, 0)))
    aliases = {}
    if cache_prev is not None:
        for j, prev in enumerate(cache_prev):
            aliases[len(args)] = 1 + j
            names.append(f'prev{j}'); args.append(prev); in_specs.append(_any_spec())
    names += ['o', 'k_out', 'v_out']
    cache_shape = jax.ShapeDtypeStruct((BATCH, DEPTH, SEQ, KV_WIDTH), F32)
    cache_spec = pl.BlockSpec((NB_CTX, None, SEQ, KV_WIDTH), lambda b: (b, layer, 0, 0))
    return pl.pallas_call(
        functools.partial(_attn_ctx_kernel, tuple(names)),
        grid=(BATCH // NB_CTX,), in_specs=in_specs,
        out_specs=(pl.BlockSpec((rows, GROUP_WIDTH), lambda b: (b, 0)), cache_spec, cache_spec),
        out_shape=(jax.ShapeDtypeStruct((N_TOK, GROUP_WIDTH), F32), cache_shape, cache_shape),
        input_output_aliases=aliases,
        compiler_params=pltpu.CompilerParams(dimension_semantics=("parallel",)),
        name="attn_context",
    )(*args)


def _attention_latent(u, prev, *, col_q, col_k, col_v, cache_k, cache_v, window=False, sink=None, qkg2=None):
    tq = TQ_ATTN
    nq = DEC_SEQ // tq
    row0_q, row0_k = N_CTX // tq, N_CTX // DEC_SEQ
    qb, kb, vb = col_q // GROUP_WIDTH, col_k // KV_WIDTH, col_v // KV_WIDTH
    cos, sin = _rope_tables(DEC_SEQ)
    const = lambda b, i: (0, 0)
    names = ['q', 'k', 'v', 'kc', 'vc', 'cos_q', 'sin_q', 'cos_k', 'sin_k']
    args = [u, u, u, cache_k, cache_v, cos, sin, cos, sin]
    in_specs = [
        pl.BlockSpec((tq, GROUP_WIDTH), lambda b, i: (row0_q + b * nq + i, qb)),
        pl.BlockSpec((DEC_SEQ, KV_WIDTH), lambda b, i: (row0_k + b, kb)),
        pl.BlockSpec((DEC_SEQ, KV_WIDTH), lambda b, i: (row0_k + b, vb)),
        pl.BlockSpec((None, PAST_LEN, KV_WIDTH), lambda b, i: (b, 0, 0)),
        pl.BlockSpec((None, PAST_LEN, KV_WIDTH), lambda b, i: (b, 0, 0)),
        pl.BlockSpec((tq, KV_WIDTH), lambda b, i: (i, 0)),
        pl.BlockSpec((tq, KV_WIDTH), lambda b, i: (i, 0)),
        pl.BlockSpec((DEC_SEQ, KV_WIDTH), const),
        pl.BlockSpec((DEC_SEQ, KV_WIDTH), const),
    ]
    if sink is not None:
        names.append('sink'); args.append(sink)
        in_specs.append(pl.BlockSpec(memory_space=pltpu.SMEM))
    if qkg2 is not None:
        names.append('qkg'); args.append(qkg2)
        in_specs.append(pl.BlockSpec((2, KV_WIDTH), const))
    aliases = {len(args): 0}
    names.append('prev'); args.append(prev); in_specs.append(_any_spec())
    names += ['o', 'k_scr']
    return pl.pallas_call(
        functools.partial(_attn_lat_kernel, tuple(names), window),
        grid=(DEC_BATCH, nq), in_specs=in_specs,
        out_specs=pl.BlockSpec((tq, GROUP_WIDTH), lambda b, i: (row0_q + b * nq + i, 0)),
        out_shape=jax.ShapeDtypeStruct((N_TOK, GROUP_WIDTH), F32),
        scratch_shapes=[pltpu.VMEM((DEC_SEQ, KV_WIDTH), BF16)],
        input_output_aliases=aliases,
        compiler_params=pltpu.CompilerParams(dimension_semantics=("parallel", "arbitrary")),
        name="attn_latent",
    )(*args)


def _dft_matrices(length):
    n = length
    k = np.arange(n, dtype=np.int64)[:, None]
    s = np.arange(n, dtype=np.int64)[None, :]
    ang = np.pi * ((k * s) % (2 * n)).astype(np.float64) / n
    fwd_cos = np.cos(ang)
    fwd_sin = -np.sin(ang)
    fwd_sin[0, :] = 1.0 - 2.0 * (np.arange(n) % 2)
    fwd = np.concatenate([fwd_cos, fwd_sin], axis=0)
    wk = np.full((n,), 2.0)
    wk[0] = 1.0
    inv_cos = (np.cos(ang) * wk[:, None]).T / (2 * n)
    inv_sin = (-2.0 * np.sin(ang)).T / (2 * n)
    inv_sin[:, 0] = (1.0 - 2.0 * (np.arange(n) % 2)) / (2 * n)
    inv = np.concatenate([inv_cos, inv_sin], axis=1)
    return fwd.astype(np.float32), inv.astype(np.float32)


def _filter_features(length):
    t = np.linspace(0.0, 1.0, length, dtype=np.float32)[:, None]
    w = (np.float32(2.0 * math.pi / length) * np.arange(length, dtype=np.float32))[:, None]
    bands = np.linspace(1e-4, HY_BANDS - 1, HY_BANDS, dtype=np.float32)[None, :]
    z = np.concatenate([t, np.cos(bands * w), -np.sin(bands * w)], axis=-1).astype(np.float32)
    zp = np.zeros((length, 128), np.float32)
    zp[:, :HY_EMB] = z
    deltas = np.linspace(math.log(HY_TARGET) / HY_SLOW, math.log(HY_TARGET) / HY_FAST, HY_CH, dtype=np.float32)
    decay = np.exp(-t * np.abs(deltas)).astype(np.float32)
    return zp, decay


def _filter_kernel(z_ref, w1_ref, b1_ref, w2_ref, b2_ref, w3_ref, fr_ref, dec_ref, f_ref, kr_ref, ki_ref):
    n = z_ref.shape[0]
    h = jnp.sin(fr_ref[0:1, :] * (_fdot(z_ref[...], w1_ref[...]) + b1_ref[...]))
    h = jnp.sin(fr_ref[1:2, :] * (_fdot(h, w2_ref[...]) + b2_ref[...]))
    h = _fdot(h, w3_ref[...])
    dec = dec_ref[...]
    half = HY_ORDER * HY_CH
    pos = jnp.concatenate([h[:, j * HY_CH:(j + 1) * HY_CH] * dec for j in range(HY_ORDER)], axis=-1)
    neg = jnp.concatenate([h[:, half + j * HY_CH:half + (j + 1) * HY_CH] * dec for j in range(HY_ORDER)], axis=-1)
    row = lax.broadcasted_iota(jnp.int32, (n, half), 0)
    neg = jnp.where(row == 0, 0.0, neg)
    fwd = f_ref[...]
    a = jnp.dot(fwd, (pos + neg).astype(BF16), preferred_element_type=F32)
    b = jnp.dot(fwd, (pos - neg).astype(BF16), preferred_element_type=F32)
    kr_ref[...] = a[:n]
    ki_ref[...] = jnp.where(row == 0, a[n:], b[n:])


def _hyena_filters(length, fwd, w1p, b1, w2, b2, w3, freq):
    zp, decay = _filter_features(length)
    half = HY_ORDER * HY_CH
    return pl.pallas_call(
        _filter_kernel,
        out_shape=(jax.ShapeDtypeStruct((length, half), F32), jax.ShapeDtypeStruct((length, half), F32)),
        name="hyena_filters",
    )(zp, w1p, b1.reshape(1, HY_FH), w2, b2.reshape(1, HY_FH), w3, freq, decay, fwd)


def _hyena_kernel(z_ref, cw_ref, cb_ref, f_ref, g_ref, kr_ref, ki_ref, bias_ref, o_ref, *, n):
    nb = z_ref.shape[0] // n
    row = lax.broadcasted_iota(jnp.int32, (n, 3 * HY_CH), 0)
    row0 = lax.broadcasted_iota(jnp.int32, (n, HY_CH), 0) == 0
    vs, x1s, x2s = [], [], []
    for bb in range(nb):
        z = z_ref[bb * n:(bb + 1) * n, :]
        prev = jnp.where(row == 0, 0.0, pltpu.roll(z, 1, 0))
        nxt = jnp.where(row == n - 1, 0.0, pltpu.roll(z, n - 1, 0))
        z = prev * cw_ref[0:1, :] + z * cw_ref[1:2, :] + nxt * cw_ref[2:3, :] + cb_ref[...]
        vs.append(z[:, :HY_CH])
        x1s.append(z[:, HY_CH:2 * HY_CH])
        x2s.append(z[:, 2 * HY_CH:])

    def long_conv(xs, order):
        cols = slice(order * HY_CH, (order + 1) * HY_CH)
        kr, ki = kr_ref[:, cols], ki_ref[:, cols]
        spec = jnp.dot(f_ref[...], jnp.concatenate([x.astype(BF16) for x in xs], axis=1),
                       preferred_element_type=F32)
        prods = []
        for bb in range(nb):
            ur, ui = spec[:n, bb * HY_CH:(bb + 1) * HY_CH], spec[n:, bb * HY_CH:(bb + 1) * HY_CH]
            yr = ur * kr - jnp.where(row0, 0.0, ui * ki)
            yi = jnp.where(row0, ui * ki, ur * ki + ui * kr)
            prods.append(jnp.concatenate([yr, yi], axis=0).astype(BF16))
        y = jnp.dot(g_ref[...], jnp.concatenate(prods, axis=1), preferred_element_type=F32)
        return [y[:, bb * HY_CH:(bb + 1) * HY_CH] + xs[bb] * bias_ref[order:order + 1, :] for bb in range(nb)]

    ys = long_conv(vs, 0)
    ys = long_conv([x1s[bb] * ys[bb] for bb in range(nb)], 1)
    for bb in range(nb):
        o_ref[bb * n:(bb + 1) * n, :] = x2s[bb] * ys[bb]


def _hyena(u, prev, *, latent, fwd, inv, kr, ki, conv_w, conv_b, bias):
    n = DEC_SEQ if latent else SEQ
    nb = 1 if latent else NB_HYENA
    steps = DEC_BATCH if latent else BATCH // nb
    row0 = N_CTX // n if latent else 0
    half = HY_ORDER * HY_CH
    const = lambda b: (0, 0)
    args = [u, conv_w, conv_b.reshape(1, 3 * HY_CH), fwd, inv, kr, ki, bias]
    in_specs = [
        pl.BlockSpec((nb * n, 3 * HY_CH), lambda b: (row0 + b, COL_HY // (3 * HY_CH))),
        pl.BlockSpec((3, 3 * HY_CH), const),
        pl.BlockSpec((1, 3 * HY_CH), const),
        pl.BlockSpec((2 * n, n), const),
        pl.BlockSpec((n, 2 * n), const),
        pl.BlockSpec((n, half), const),
        pl.BlockSpec((n, half), const),
        pl.BlockSpec((HY_ORDER, HY_CH), const),
    ]
    aliases = {}
    if latent:
        aliases = {len(args): 0}
        args.append(prev)
        in_specs.append(_any_spec())

    def body(*refs):
        if latent:
            refs = refs[:len(args) - 1] + refs[len(args):]
        _hyena_kernel(*refs, n=n)

    return pl.pallas_call(
        body, grid=(steps,), in_specs=in_specs,
        out_specs=pl.BlockSpec((nb * n, HY_CH), lambda b: (row0 + b, 0)),
        out_shape=jax.ShapeDtypeStruct((N_TOK, HY_CH), F32),
        input_output_aliases=aliases,
        compiler_params=pltpu.CompilerParams(dimension_semantics=("parallel",)),
        name="hyena_latent" if latent else "hyena_context",
    )(*args)


def _gla_masks():
    t = np.arange(ROWS_GROUP)[:, None]
    s = np.arange(ROWS_GROUP)[None, :]
    same = (t // GLA_CHUNK) == (s // GLA_CHUNK)
    return np.stack([same & (s <= t), same & (s >= t), same]).astype(np.float32)


def _gla_group(r, rows, z, st):
    c = GLA_CHUNK
    hk = H_GLA * DK_GLA
    n_chunks = ROWS_GROUP // c
    logits = _bdot(r['gl'][rows, :], r['gw'][z]) + r['gb'][z:z + 1, :]
    g = (jnp.minimum(logits, 0.0) - jnp.log1p(jnp.exp(-jnp.abs(logits)))) / GLA_NORM
    causal = r['masks'][z]
    b = _split_dot(causal.astype(BF16), g)
    tot = _split_dot(r['masks'][2].astype(BF16), g)
    k = r['k'][rows, :]
    q_t = r['q'][rows, :] * jnp.exp(b) * (DK_GLA ** -0.5)
    k_t = (k * jnp.exp(-b)).astype(BF16)
    k_e = (k * jnp.exp(tot - b)).astype(BF16)
    dec = jnp.exp(tot)
    v = r['v'][rows, :].astype(BF16)

    qlane = lax.broadcasted_iota(jnp.int32, (ROWS_GROUP, hk), 1) // DK_GLA
    vlane = lax.broadcasted_iota(jnp.int32, (ROWS_GROUP, GROUP_WIDTH), 1) // DV_GLA
    qs = jnp.concatenate([jnp.where(qlane == h, q_t, 0.0) for h in range(H_GLA)], axis=0).astype(BF16)
    s = _bdot_nt(qs, k_t)
    valid = causal != 0.0
    p = jnp.concatenate([jnp.where(valid, s[h * ROWS_GROUP:(h + 1) * ROWS_GROUP], 0.0).astype(BF16)
                         for h in range(H_GLA)], axis=0)
    res = jnp.dot(p, v, preferred_element_type=F32)
    o_intra = jnp.where(vlane == 0, res[0:ROWS_GROUP], 0.0)
    for h in range(1, H_GLA):
        o_intra = jnp.where(vlane == h, res[h * ROWS_GROUP:(h + 1) * ROWS_GROUP], o_intra)

    bd = (lax.broadcasted_iota(jnp.int32, (GROUP_WIDTH, hk), 0) // DV_GLA
          == lax.broadcasted_iota(jnp.int32, (GROUP_WIDTH, hk), 1) // DK_GLA)
    q_bf = q_t.astype(BF16)
    o_inter = [None] * n_chunks
    for step in range(n_chunks):
        ci = step if z == 0 else n_chunks - 1 - step
        cr = slice(ci * c, (ci + 1) * c)
        o_inter[ci] = _bdot_nt(q_bf[cr], st)
        st = st * dec[ci * c:ci * c + 1, :] + jnp.where(bd, _bdot_tn(v[cr], k_e[cr]), 0.0)
    return o_intra + jnp.concatenate(o_inter, axis=0), st


def _head_rms(o, g):
    hb = (lax.broadcasted_iota(jnp.int32, (GROUP_WIDTH, GROUP_WIDTH), 0) // DV_GLA
          == lax.broadcasted_iota(jnp.int32, (GROUP_WIDTH, GROUP_WIDTH), 1) // DV_GLA)
    ms = _split_dot_rhs(o * o, hb.astype(BF16)) * (1.0 / DV_GLA)
    return o * lax.rsqrt(ms + EPS) * g


def _split_dot_rhs(x, m):
    hi = x.astype(BF16)
    lo = (x - hi.astype(F32)).astype(BF16)
    return jnp.dot(hi, m, preferred_element_type=F32) + jnp.dot(lo, m, preferred_element_type=F32)


def _expand_state(s):
    hk = H_GLA * DK_GLA
    bd = (lax.broadcasted_iota(jnp.int32, (GROUP_WIDTH, hk), 0) // DV_GLA
          == lax.broadcasted_iota(jnp.int32, (GROUP_WIDTH, hk), 1) // DK_GLA)
    return jnp.where(bd, jnp.concatenate([s] * H_GLA, axis=0), 0.0)


def _compact_state(st):
    out = st[0:DV_GLA]
    for h in range(1, H_GLA):
        out = out + st[h * DV_GLA:(h + 1) * DV_GLA]
    return out


def _gla_ctx_kernel(names, *refs):
    r = dict(zip(names, refs))
    hk = H_GLA * DK_GLA
    for bb in range(NB_CTX):
        rows = slice(bb * SEQ, (bb + 1) * SEQ)
        zero = jnp.zeros((GROUP_WIDTH, hk), F32)
        o_f, st_f = _gla_group(r, rows, 0, zero)
        o_b, st_b = _gla_group(r, rows, 1, zero)
        r['o'][rows, :] = _head_rms(o_f + o_b, r['mg'][...])
        r['sf'][bb, 0] = _compact_state(st_f)
        r['sf'][bb, 1] = _compact_state(st_b)


def _gla_lat_kernel(names, *refs):
    r = dict(zip(names, refs))
    n_groups = DEC_SEQ // ROWS_GROUP

    def run(z):
        r['st'][...] = _expand_state(r['s0'][z])

        def body(step, carry):
            gi = step if z == 0 else n_groups - 1 - step
            rows = pl.ds(pl.multiple_of(gi * ROWS_GROUP, ROWS_GROUP), ROWS_GROUP)
            o, st = _gla_group(r, rows, z, r['st'][...])
            r['st'][...] = st
            if z == 0:
                r['o'][rows, :] = o
            else:
                r['o'][rows, :] = _head_rms(r['o'][rows, :] + o, r['mg'][...])
            return carry

        lax.fori_loop(0, n_groups, body, 0)

    run(0)
    run(1)


def _gla(u, prev, *, latent, gate_w, gate_b, mix_g_d, s0=None):
    n = DEC_SEQ if latent else NB_CTX * SEQ
    steps = DEC_BATCH if latent else BATCH // NB_CTX
    row0 = N_CTX // n if latent else 0
    hk = H_GLA * DK_GLA
    const2 = lambda b: (0, 0)
    names = ['q', 'k', 'v', 'gl', 'gw', 'gb', 'mg', 'masks']
    args = [u, u, u, u, gate_w, gate_b, mix_g_d, _gla_masks()]
    in_specs = [
        pl.BlockSpec((n, hk), lambda b: (row0 + b, COL_QD // hk)),
        pl.BlockSpec((n, hk), lambda b: (row0 + b, COL_KD // hk)),
        pl.BlockSpec((n, GROUP_WIDTH), lambda b: (row0 + b, COL_VD // GROUP_WIDTH)),
        pl.BlockSpec((n, 128), lambda b: (row0 + b, COL_GD // 128)),
        pl.BlockSpec((2, 128, hk), lambda b: (0, 0, 0)),
        pl.BlockSpec((2, hk), const2),
        pl.BlockSpec((1, GROUP_WIDTH), const2),
        pl.BlockSpec((3, ROWS_GROUP, ROWS_GROUP), lambda b: (0, 0, 0)),
    ]
    out_shape = jax.ShapeDtypeStruct((N_TOK, GROUP_WIDTH), F32)
    out_specs = pl.BlockSpec((n, GROUP_WIDTH), lambda b: (row0 + b, 0))
    aliases, scratch = {}, []
    if latent:
        names.append('s0'); args.append(s0)
        in_specs.append(pl.BlockSpec((None, 2, DV_GLA, hk), lambda b: (b, 0, 0, 0)))
        aliases = {len(args): 0}
        names.append('prev'); args.append(prev); in_specs.append(_any_spec())
        names += ['o', 'st']
        scratch = [pltpu.VMEM((GROUP_WIDTH, hk), F32)]
        body = functools.partial(_gla_lat_kernel, tuple(names))
    else:
        names += ['o', 'sf']
        out_shape = (out_shape, jax.ShapeDtypeStruct((BATCH, 2, DV_GLA, hk), F32))
        out_specs = (out_specs, pl.BlockSpec((NB_CTX, 2, DV_GLA, hk), lambda b: (b, 0, 0, 0)))
        body = functools.partial(_gla_ctx_kernel, tuple(names))
    return pl.pallas_call(
        body, grid=(steps,), in_specs=in_specs, out_specs=out_specs, out_shape=out_shape,
        scratch_shapes=scratch, input_output_aliases=aliases,
        compiler_params=pltpu.CompilerParams(dimension_semantics=("parallel",)),
        name="gla_latent" if latent else "gla_context",
    )(*args)


def _merge_kernel(h_ref, oa_ref, ob_ref, oc_ref, od_ref, rd_ref, gt_ref, mg_ref, wo_ref, o_ref):
    gw = GROUP_WIDTH
    ys = [
        _rms(oa_ref[...], mg_ref[:, 0:gw]),
        _rms(ob_ref[...], mg_ref[:, gw:2 * gw]),
        _rms(oc_ref[...], mg_ref[:, 2 * gw:3 * gw]),
        od_ref[...] * _silu(rd_ref[:, COL_RD - COL_GD:COL_RD - COL_GD + gw]),
    ]
    mix = _bdot(ys[0], wo_ref[0:gw, :])
    for j in range(1, 4):
        mix = mix + _bdot(ys[j], wo_ref[j * gw:(j + 1) * gw, :])
    o_ref[...] = h_ref[...] + gt_ref[...] * mix


def _merge(h, oa, ob, oc, od, u, mod, mix_g, w_out, layer):
    tm = TM_PROJ
    grp = pl.BlockSpec((tm, GROUP_WIDTH), lambda i: (i, 0))
    return pl.pallas_call(
        _merge_kernel,
        grid=(N_TOK // tm,),
        in_specs=[
            pl.BlockSpec((tm, D_MODEL), lambda i: (i, 0)),
            grp, grp, grp, grp,
            pl.BlockSpec((tm, COL_RD_BLOCK), lambda i: (i, COL_GD // COL_RD_BLOCK)),
            _mod_spec(tm, 5, 1),
            pl.BlockSpec((None, 1, D_MODEL), lambda i: (layer, 0, 0)),
            pl.BlockSpec((None, D_MODEL, D_MODEL), lambda i: (layer, 0, 0)),
        ],
        out_specs=pl.BlockSpec((tm, D_MODEL), lambda i: (i, 0)),
        out_shape=jax.ShapeDtypeStruct((N_TOK, D_MODEL), F32),
        compiler_params=pltpu.CompilerParams(dimension_semantics=("parallel",)),
        name="merge",
    )(h, oa, ob, oc, od, u, mod, mix_g, w_out)


def _mix_order(x, axis):
    blocks = []
    for g in range(4):
        order = HEAD_ORDER if g in (0, 2) else range(N_HEADS)
        for h in order:
            start = g * GROUP_WIDTH + h * HEAD_DIM
            blocks.append(lax.slice_in_dim(x, start, start + HEAD_DIM, axis=axis))
    return jnp.concatenate(blocks, axis=axis)


def _query_weights(w):
    heads = [w[..., base + HEAD_DIM * h:base + HEAD_DIM * (h + 1)]
             for base in (REF_QA, REF_QC) for h in HEAD_ORDER]
    return jnp.concatenate(heads, axis=-1)


def _gate_weights(gate_w):
    out = jnp.zeros((2, 128, H_GLA * DK_GLA), gate_w.dtype)
    out = out.at[0, 0:GLA_RANK].set(gate_w[0])
    return out.at[1, GLA_RANK:2 * GLA_RANK].set(gate_w[1])


def _states_to_kernel(st):
    b = st.shape[0]
    return st.transpose(0, 1, 4, 2, 3).reshape(b, 2, DV_GLA, H_GLA * DK_GLA)


def _states_from_kernel(st):
    b = st.shape[0]
    return st.reshape(b, 2, DV_GLA, H_GLA, DK_GLA).transpose(0, 1, 3, 4, 2)


def kernel(x_prompt, x_sample, cache_swa_k, cache_swa_v, cache_gqa_k, cache_gqa_v, state_gla, c, c_ctx, w_mod, b_mod, norm_g, ffn_w_in, ffn_w_out, w_in, w_out, mix_g, swa_sink, qk_norm_g, hy_conv_w, hy_conv_b, hy_w1, hy_b1, hy_w2, hy_b2, hy_w3, hy_freq, hy_bias, gla_gate_w, gla_gate_b, final_g):
    cond = jnp.zeros((MOD_ROWS, D_MODEL), F32).at[0].set(c_ctx).at[1:1 + DEC_BATCH].set(c)
    mod_all = _modulation(cond, w_mod, b_mod).reshape(DEPTH, MOD_ROWS, N_MOD, 1, D_MODEL)

    dft = {n: tuple(jnp.asarray(m).astype(BF16) for m in _dft_matrices(n)) for n in (SEQ, DEC_SEQ)}

    xs = [x_prompt.reshape(N_CTX, D_MODEL), x_sample.reshape(N_LAT, D_MODEL)]
    ffn_in = ffn_w_in.astype(BF16)
    ffn_out = ffn_w_out.astype(BF16)
    ng = norm_g.reshape(DEPTH, 3, 1, D_MODEL)
    proj_w = w_in.astype(BF16)
    proj_wq = _query_weights(w_in).astype(BF16)
    out_w = _mix_order(w_out.astype(BF16), 1)
    mix_gp = _mix_order(mix_g, 1).reshape(DEPTH, 1, D_MODEL)
    cache_a = cache_c = None
    sts = []
    for l in range(DEPTH):
        mod = mod_all[l]
        h = _ffn(xs, mod, 0, ng, ffn_in, ffn_out, l, 0)
        u = _in_proj(h, mod, ng, proj_w, proj_wq, l)

        ck_a = cache_swa_k[:, l].reshape(DEC_BATCH, PAST_LEN, KV_WIDTH)
        cv_a = cache_swa_v[:, l].reshape(DEC_BATCH, PAST_LEN, KV_WIDTH)
        ck_c = cache_gqa_k[:, l].reshape(DEC_BATCH, PAST_LEN, KV_WIDTH)
        cv_c = cache_gqa_v[:, l].reshape(DEC_BATCH, PAST_LEN, KV_WIDTH)
        qkg2 = jnp.tile(qk_norm_g[l], (1, N_KV))

        oa, *cache_a = _attention_context(u, l, cache_a, col_q=COL_QA, col_k=COL_KA, col_v=COL_VA,
                                          sink=swa_sink[l])
        oa = _attention_latent(u, oa, col_q=COL_QA, col_k=COL_KA, col_v=COL_VA, cache_k=ck_a, cache_v=cv_a,
                               window=True, sink=swa_sink[l])
        oc, *cache_c = _attention_context(u, l, cache_c, col_q=COL_QC, col_k=COL_KC, col_v=COL_VC, qkg2=qkg2)
        oc = _attention_latent(u, oc, col_q=COL_QC, col_k=COL_KC, col_v=COL_VC, cache_k=ck_c, cache_v=cv_c,
                               qkg2=qkg2)

        w1p = jnp.zeros((128, HY_FH), F32).at[:HY_EMB].set(hy_w1[l])
        ob = None
        for latent, n in ((False, SEQ), (True, DEC_SEQ)):
            fwd, inv = dft[n]
            kr, ki = _hyena_filters(n, fwd, w1p, hy_b1[l], hy_w2[l], hy_b2[l], hy_w3[l], hy_freq[l])
            ob = _hyena(u, ob, latent=latent, fwd=fwd, inv=inv, kr=kr, ki=ki, conv_w=hy_conv_w[l],
                        conv_b=hy_conv_b[l], bias=hy_bias[l])

        gw = _gate_weights(gla_gate_w[l]).astype(BF16)
        mg_d = mix_g[l, 3 * GROUP_WIDTH:].reshape(1, GROUP_WIDTH)
        od, st = _gla(u, None, latent=False, gate_w=gw, gate_b=gla_gate_b[l], mix_g_d=mg_d)
        od = _gla(u, od, latent=True, gate_w=gw, gate_b=gla_gate_b[l], mix_g_d=mg_d,
                  s0=_states_to_kernel(state_gla[:, l]))

        h = _merge(h, oa, ob, oc, od, u, mod, mix_gp, out_w, l)
        xs = [_ffn([h], mod, 6, ng, ffn_in, ffn_out, l, 1)] if l + 1 < DEPTH else None
        if xs is None:
            y_prompt, y_sample = _ffn([h], mod, 6, ng, ffn_in, ffn_out, l, 1, final_g=final_g)
        sts.append(_states_from_kernel(st))

    y_prompt = y_prompt.reshape(BATCH, SEQ, D_MODEL)
    y_sample = y_sample.reshape(DEC_BATCH, DEC_SEQ, D_MODEL)
    caches = [x.reshape(BATCH, DEPTH, SEQ, N_KV, HEAD_DIM) for x in (*cache_a, *cache_c)]
    return (y_prompt, y_sample, *caches, jnp.stack(sts, axis=1))
```

```python
import functools
import math

import numpy as np
import jax
import jax.numpy as jnp
from jax import lax
from jax.experimental import pallas as pl
from jax.experimental.pallas import tpu as pltpu

F32 = jnp.float32
BF16 = jnp.bfloat16

D_MODEL = 1024
BATCH = 32
SEQ = 256
DEPTH = 2
DEC_BATCH = 2
DEC_SEQ = 1024
PAST_LEN = 512
GRID_W = 64
HEAD_DIM = 64
GROUP_WIDTH = 256
N_HEADS = 4
N_KV = 2
KV_WIDTH = N_KV * HEAD_DIM
WINDOW = 128
HY_CH = 256
HY_ORDER = 2
HY_BANDS = 16
HY_EMB = 2 * HY_BANDS + 1
HY_FH = 64
HY_TARGET = 1e-2
HY_FAST = 0.3
HY_SLOW = 1.5
H_GLA = 4
DV_GLA = 64
DK_GLA = 32
GLA_RANK = 16
GLA_CHUNK = 64
GLA_NORM = 16.0
D_FF = 2816
N_MOD = 9
ROPE_THETA = 10000.0
EPS = 1e-6

N_CTX = BATCH * SEQ
N_LAT = DEC_BATCH * DEC_SEQ
N_TOK = N_CTX + N_LAT
MOD_ROWS = 8

COL_HY, COL_QA, COL_KA, COL_VA = 0, 768, 1024, 1152
COL_QC, COL_KC, COL_VC = 1280, 1536, 1664
COL_QD, COL_KD, COL_VD, COL_GD, COL_RD = 1792, 1920, 2048, 2304, 2336
PROJ_WIDTH = 2688
REF_QA, REF_KA, REF_HY, REF_QC, REF_KC, REF_END = 0, 256, 512, 1280, 1536, 2592
COL_RD_BLOCK = 384
HEAD_ORDER = (0, 2, 1, 3)
_HEAD_PERM = np.concatenate([np.arange(HEAD_DIM) + HEAD_DIM * h for h in HEAD_ORDER])

TM_FFN = 1024
TK_FFN = 256
FFN_VMEM_BYTES = 60 * 1024 * 1024
TM_PROJ = 1024
TQ_ATTN = 256
ROWS_GROUP = 256
NB_CTX = 2
NB_HYENA = 4


def _bdot(a, b):
    return jnp.dot(a.astype(BF16), b.astype(BF16), preferred_element_type=F32)


def _bdot_nt(a, b):
    return lax.dot_general(a.astype(BF16), b.astype(BF16), (((1,), (1,)), ((), ())),
                           preferred_element_type=F32)


def _bdot_tn(a, b):
    return lax.dot_general(a.astype(BF16), b.astype(BF16), (((0,), (0,)), ((), ())),
                           preferred_element_type=F32)


def _fdot(a, b):
    return jnp.dot(a, b, preferred_element_type=F32, precision=lax.Precision.HIGHEST)


def _split_dot(m, x):
    hi = x.astype(BF16)
    lo = (x - hi.astype(F32)).astype(BF16)
    return jnp.dot(m, hi, preferred_element_type=F32) + jnp.dot(m, lo, preferred_element_type=F32)


def _rms(x, g):
    return x * lax.rsqrt(jnp.mean(x * x, axis=-1, keepdims=True) + EPS) * g


def _silu(x):
    return x * jax.nn.sigmoid(x)


def _mod_row(i, tm):
    n_ctx_tiles = N_CTX // tm
    per_batch = DEC_SEQ // tm
    return jnp.where(i < n_ctx_tiles, 0, 1 + (i - n_ctx_tiles) // per_batch)


def _mod_spec(tm, j, grid_rank):
    if grid_rank == 1:
        return pl.BlockSpec((None, None, 1, D_MODEL), lambda i: (_mod_row(i, tm), j, 0, 0))
    return pl.BlockSpec((None, None, 1, D_MODEL), lambda i, k: (_mod_row(i, tm), j, 0, 0))


def _any_spec():
    return pl.BlockSpec(memory_space=pl.ANY)


def _mod_kernel(cond_ref, w_ref, b_ref, o_ref):
    o_ref[...] = _bdot(_silu(cond_ref[...]), w_ref[...]) + b_ref[...]


def _modulation(cond, w_mod, b_mod):
    tn = D_MODEL
    return pl.pallas_call(
        _mod_kernel,
        grid=(DEPTH, N_MOD * D_MODEL // tn),
        in_specs=[
            pl.BlockSpec((MOD_ROWS, D_MODEL), lambda l, j: (0, 0)),
            pl.BlockSpec((None, D_MODEL, tn), lambda l, j: (l, 0, j)),
            pl.BlockSpec((None, 1, tn), lambda l, j: (l, 0, j)),
        ],
        out_specs=pl.BlockSpec((None, MOD_ROWS, tn), lambda l, j: (l, 0, j)),
        out_shape=jax.ShapeDtypeStruct((DEPTH, MOD_ROWS, N_MOD * D_MODEL), F32),
        name="modulation",
    )(cond, w_mod, b_mod.reshape(DEPTH, 1, N_MOD * D_MODEL))


def _ffn_kernel(split_input, final, *refs):
    is_ctx = pl.program_id(0) < N_CTX // TM_FFN
    if split_input:
        xa_ref, xb_ref, *refs = refs
        x = jnp.where(is_ctx, xa_ref[...], xb_ref[...])
    else:
        x_ref, *refs = refs
        x = x_ref[...]
    if final:
        sh_ref, sc_ref, gt_ref, ng_ref, wi_ref, wo_ref, fg_ref, yc_ref, yl_ref, xm_ref, act_ref = refs
    else:
        sh_ref, sc_ref, gt_ref, ng_ref, wi_ref, wo_ref, o_ref, xm_ref, act_ref = refs
    xn = _rms(x, ng_ref[...])
    xm_ref[...] = (xn * (1.0 + sc_ref[...]) + sh_ref[...]).astype(BF16)
    tk = TK_FFN
    for j in range(D_FF // tk):
        xm = xm_ref[...]
        a = jnp.dot(xm, wi_ref[:, j * tk:(j + 1) * tk], preferred_element_type=F32)
        b = jnp.dot(xm, wi_ref[:, D_FF + j * tk:D_FF + (j + 1) * tk], preferred_element_type=F32)
        act_ref[:, j * tk:(j + 1) * tk] = (_silu(a) * b).astype(BF16)
    y = jnp.dot(act_ref[...], wo_ref[...], preferred_element_type=F32)
    out = x + 0.5 * gt_ref[...] * y
    if not final:
        o_ref[...] = out
        return
    out = _rms(out, fg_ref[...])

    @pl.when(is_ctx)
    def _():
        yc_ref[...] = out

    @pl.when(jnp.logical_not(is_ctx))
    def _():
        yl_ref[...] = out


def _ffn(xs, mod, j0, ng, w_in, w_out, layer, which, final_g=None):
    tm = TM_FFN
    n_ctx_tiles = N_CTX // tm
    split_specs = [pl.BlockSpec((tm, D_MODEL), lambda i: (jnp.minimum(i, n_ctx_tiles - 1), 0)),
                   pl.BlockSpec((tm, D_MODEL), lambda i: (jnp.maximum(i - n_ctx_tiles, 0), 0))]
    whole_spec = pl.BlockSpec((tm, D_MODEL), lambda i: (i, 0))
    x_specs = split_specs if len(xs) == 2 else [whole_spec]
    resident = dict(pipeline_mode=pl.Buffered(1))
    final = final_g is not None
    extra_args, extra_specs = [], []
    if final:
        extra_args = [final_g.reshape(1, D_MODEL)]
        extra_specs = [pl.BlockSpec((1, D_MODEL), lambda i: (0, 0))]
        out_specs = tuple(split_specs)
        out_shape = (jax.ShapeDtypeStruct((N_CTX, D_MODEL), F32), jax.ShapeDtypeStruct((N_LAT, D_MODEL), F32))
    else:
        out_specs = whole_spec
        out_shape = jax.ShapeDtypeStruct((N_TOK, D_MODEL), F32)
    return pl.pallas_call(
        functools.partial(_ffn_kernel, len(xs) == 2, final),
        grid=(N_TOK // tm,),
        in_specs=x_specs + [
            _mod_spec(tm, j0, 1), _mod_spec(tm, j0 + 1, 1), _mod_spec(tm, j0 + 2, 1),
            pl.BlockSpec((None, None, 1, D_MODEL), lambda i: (layer, j0 // 3, 0, 0)),
            pl.BlockSpec((None, None, D_MODEL, 2 * D_FF), lambda i: (layer, which, 0, 0), **resident),
            pl.BlockSpec((None, None, D_FF, D_MODEL), lambda i: (layer, which, 0, 0), **resident),
        ] + extra_specs,
        out_specs=out_specs,
        out_shape=out_shape,
        scratch_shapes=[pltpu.VMEM((tm, D_MODEL), BF16), pltpu.VMEM((tm, D_FF), BF16)],
        compiler_params=pltpu.CompilerParams(dimension_semantics=("arbitrary",),
                                             vmem_limit_bytes=FFN_VMEM_BYTES),
        name="ffn",
    )(*xs, mod, mod, mod, ng, w_in, w_out, *extra_args)


def _proj_kernel(x_ref, sh_ref, sc_ref, ng_ref, w_ref, o_ref):
    xn = _rms(x_ref[...], ng_ref[...])
    xm = (xn * (1.0 + sc_ref[...]) + sh_ref[...]).astype(BF16)

    def put(col, w):
        o_ref[:, col:col + w.shape[1]] = jnp.dot(xm, w, preferred_element_type=F32)

    def queries(base):
        return jnp.concatenate([w_ref[:, base + HEAD_DIM * h:base + HEAD_DIM * (h + 1)] for h in HEAD_ORDER], axis=1)

    put(COL_HY, w_ref[:, REF_HY:REF_QC])
    put(COL_QA, queries(REF_QA))
    put(COL_KA, w_ref[:, REF_KA:REF_HY])
    put(COL_QC, queries(REF_QC))
    put(COL_KC, w_ref[:, REF_KC:COL_GD])
    put(COL_GD, w_ref[:, COL_GD:REF_END])
    o_ref[:, REF_END:PROJ_WIDTH] = jnp.zeros((x_ref.shape[0], PROJ_WIDTH - REF_END), F32)


def _in_proj(h, mod, ng, w, layer):
    tm = TM_PROJ
    return pl.pallas_call(
        _proj_kernel,
        grid=(N_TOK // tm,),
        in_specs=[
            pl.BlockSpec((tm, D_MODEL), lambda i: (i, 0)),
            _mod_spec(tm, 3, 1), _mod_spec(tm, 4, 1),
            pl.BlockSpec((None, None, 1, D_MODEL), lambda i: (layer, 1, 0, 0)),
            pl.BlockSpec((None, D_MODEL, REF_END), lambda i: (layer, 0, 0)),
        ],
        out_specs=pl.BlockSpec((tm, PROJ_WIDTH), lambda i: (i, 0)),
        out_shape=jax.ShapeDtypeStruct((N_TOK, PROJ_WIDTH), F32),
        compiler_params=pltpu.CompilerParams(dimension_semantics=("parallel",)),
        name="in_proj",
    )(h, mod, mod, ng, w)


def _rope_tables(length):
    rows = length // GRID_W
    r = np.repeat(np.arange(rows, dtype=np.float32), GRID_W)
    col = np.tile(np.arange(GRID_W, dtype=np.float32), rows)
    nf = HEAD_DIM // 4
    inv = (np.float32(ROPE_THETA) ** (-np.arange(nf, dtype=np.float32) / nf)).astype(np.float32)
    ang = np.concatenate([r[:, None] * inv, col[:, None] * inv], axis=-1).astype(np.float32)
    cos, sin = np.cos(ang).astype(np.float32), np.sin(ang).astype(np.float32)
    return np.tile(cos, (1, 4)), np.tile(np.concatenate([-sin, sin], axis=-1), (1, 2))


def _pair_lanes(rows):
    lane = lax.broadcasted_iota(jnp.int32, (rows, 2 * HEAD_DIM), 1)
    return lane < HEAD_DIM, (lane % HEAD_DIM) < HEAD_DIM // 2


def _rope_pair(x, cos, sin_signed, first_half):
    partner = jnp.where(first_half, pltpu.roll(x, 3 * HEAD_DIM // 2, 1), pltpu.roll(x, HEAD_DIM // 2, 1))
    return x * cos + partner * sin_signed


def _rms_pair(x, g, lo):
    x2 = x * x
    s_lo = jnp.sum(jnp.where(lo, x2, 0.0), axis=-1, keepdims=True)
    s_hi = jnp.sum(jnp.where(lo, 0.0, x2), axis=-1, keepdims=True)
    ms = jnp.where(lo, s_lo, s_hi) * (1.0 / HEAD_DIM)
    return x * lax.rsqrt(ms + EPS) * g


def _attn_core(q, k_bf, v_bf, lo, *, mask=None, ctx=None, sinks=None):
    tq = q.shape[0]
    pair = 2 * HEAD_DIM
    scale = HEAD_DIM ** -0.5
    qa, qb = q[:, :pair] * scale, q[:, pair:] * scale
    qs = jnp.concatenate([jnp.where(lo, qa, 0.0), jnp.where(lo, 0.0, qa),
                          jnp.where(lo, qb, 0.0), jnp.where(lo, 0.0, qb)], axis=0).astype(BF16)
    s = _bdot_nt(qs, k_bf)
    if ctx is not None:
        s_c = _bdot_nt(qs, ctx[0])
    ps, pcs, dens = [], [], []
    for j in range(N_HEADS):
        rows = slice(j * tq, (j + 1) * tq)
        sj = s[rows]
        if mask is not None:
            sj = jnp.where(mask, sj, -1e30)
        m = jnp.max(sj, axis=-1, keepdims=True)
        if ctx is not None:
            m = jnp.maximum(m, jnp.max(s_c[rows], axis=-1, keepdims=True))
        if sinks is not None:
            m = jnp.maximum(m, sinks[j])
        p = jnp.exp(sj - m)
        den = jnp.sum(p, axis=-1, keepdims=True)
        ps.append(p.astype(BF16))
        if ctx is not None:
            pc = jnp.exp(s_c[rows] - m)
            den = den + jnp.sum(pc, axis=-1, keepdims=True)
            pcs.append(pc.astype(BF16))
        if sinks is not None:
            den = den + jnp.exp(sinks[j] - m)
        dens.append(den)
    r = jnp.dot(jnp.concatenate(ps, axis=0), v_bf, preferred_element_type=F32)
    if ctx is not None:
        r = r + jnp.dot(jnp.concatenate(pcs, axis=0), ctx[1], preferred_element_type=F32)
    o = [r[j * tq:(j + 1) * tq] / dens[j] for j in range(N_HEADS)]
    return jnp.concatenate([jnp.where(lo, o[0], o[1]), jnp.where(lo, o[2], o[3])], axis=1)


def _attn_ctx_kernel(names, *refs):
    r = dict(zip(names, refs))
    lo, _ = _pair_lanes(SEQ)
    sinks = [r['sink'][h] for h in HEAD_ORDER] if 'sink' in r else None
    for bb in range(NB_CTX):
        rows = slice(bb * SEQ, (bb + 1) * SEQ)
        q, k, v = r['q'][rows, :], r['k'][rows, :], r['v'][rows, :]
        if 'qkg' in r:
            gq, gk = r['qkg'][0:1, :], r['qkg'][1:2, :]
            q = jnp.concatenate([_rms_pair(q[:, :KV_WIDTH], gq, lo), _rms_pair(q[:, KV_WIDTH:], gq, lo)], axis=1)
            k = _rms_pair(k, gk, lo)
        r['k_out'][bb] = k
        r['v_out'][bb] = v
        r['o'][rows, :] = _attn_core(q, k.astype(BF16), v.astype(BF16), lo, sinks=sinks)


def _attn_lat_kernel(names, window, *refs):
    r = dict(zip(names, refs))
    tq = TQ_ATTN
    i = pl.program_id(1)
    lo, first_half = _pair_lanes(tq)
    qk_norm = 'qkg' in r

    @pl.when(i == 0)
    def _():
        lo_k, first_half_k = _pair_lanes(DEC_SEQ)
        k = r['k'][...]
        if qk_norm:
            k = _rms_pair(k, r['qkg'][1:2, :], lo_k)
        r['k_scr'][...] = _rope_pair(k, r['cos_k'][...], r['sin_k'][...], first_half_k).astype(BF16)

    q = r['q'][...]
    halves = []
    for c0 in (0, KV_WIDTH):
        x = q[:, c0:c0 + KV_WIDTH]
        if qk_norm:
            x = _rms_pair(x, r['qkg'][0:1, :], lo)
        halves.append(_rope_pair(x, r['cos_q'][...], r['sin_q'][...], first_half))
    q = jnp.concatenate(halves, axis=1)
    sinks = [r['sink'][h] for h in HEAD_ORDER] if 'sink' in r else None
    ctx = (r['kc'][...].astype(BF16), r['vc'][...].astype(BF16))
    if window:
        span = tq + 2 * WINDOW
        start = pl.multiple_of(jnp.clip(i * tq - WINDOW, 0, DEC_SEQ - span), WINDOW)
        qpos = i * tq + lax.broadcasted_iota(jnp.int32, (tq, span), 0)
        kpos = start + lax.broadcasted_iota(jnp.int32, (tq, span), 1)
        mask = jnp.abs(qpos - kpos) <= WINDOW
        k_bf = r['k_scr'][pl.ds(start, span), :]
        v_bf = r['v'][pl.ds(start, span), :].astype(BF16)
    else:
        mask = None
        k_bf = r['k_scr'][...]
        v_bf = r['v'][...].astype(BF16)
    r['o'][...] = _attn_core(q, k_bf, v_bf, lo, mask=mask, ctx=ctx, sinks=sinks)


def _attention_context(u, layer, cache_prev, *, col_q, col_k, col_v, sink=None, qkg2=None):
    rows = NB_CTX * SEQ
    qb, kb, vb = col_q // GROUP_WIDTH, col_k // KV_WIDTH, col_v // KV_WIDTH
    names = ['q', 'k', 'v']
    args = [u, u, u]
    in_specs = [pl.BlockSpec((rows, GROUP_WIDTH), lambda b: (b, qb)),
                pl.BlockSpec((rows, KV_WIDTH), lambda b: (b, kb)),
                pl.BlockSpec((rows, KV_WIDTH), lambda b: (b, vb))]
    if sink is not None:
        names.append('sink'); args.append(sink)
        in_specs.append(pl.BlockSpec(memory_space=pltpu.SMEM))
    if qkg2 is not None:
        names.append('qkg'); args.append(qkg2)
        in_specs.append(pl.BlockSpec((2, KV_WIDTH), lambda b: (0, 0)))
    aliases = {}
    if cache_prev is not None:
        for j, prev in enumerate(cache_prev):
            aliases[len(args)] = 1 + j
            names.append(f'prev{j}'); args.append(prev); in_specs.append(_any_spec())
    names += ['o', 'k_out', 'v_out']
    cache_shape = jax.ShapeDtypeStruct((BATCH, DEPTH, SEQ, KV_WIDTH), F32)
    cache_spec = pl.BlockSpec((NB_CTX, None, SEQ, KV_WIDTH), lambda b: (b, layer, 0, 0))
    return pl.pallas_call(
        functools.partial(_attn_ctx_kernel, tuple(names)),
        grid=(BATCH // NB_CTX,), in_specs=in_specs,
        out_specs=(pl.BlockSpec((rows, GROUP_WIDTH), lambda b: (b, 0)), cache_spec, cache_spec),
        out_shape=(jax.ShapeDtypeStruct((N_TOK, GROUP_WIDTH), F32), cache_shape, cache_shape),
        input_output_aliases=aliases,
        compiler_params=pltpu.CompilerParams(dimension_semantics=("parallel",)),
        name="attn_context",
    )(*args)


def _attention_latent(u, prev, *, col_q, col_k, col_v, cache_k, cache_v, window=False, sink=None, qkg2=None):
    tq = TQ_ATTN
    nq = DEC_SEQ // tq
    row0_q, row0_k = N_CTX // tq, N_CTX // DEC_SEQ
    qb, kb, vb = col_q // GROUP_WIDTH, col_k // KV_WIDTH, col_v // KV_WIDTH
    cos, sin = _rope_tables(DEC_SEQ)
    const = lambda b, i: (0, 0)
    names = ['q', 'k', 'v', 'kc', 'vc', 'cos_q', 'sin_q', 'cos_k', 'sin_k']
    args = [u, u, u, cache_k, cache_v, cos, sin, cos, sin]
    in_specs = [
        pl.BlockSpec((tq, GROUP_WIDTH), lambda b, i: (row0_q + b * nq + i, qb)),
        pl.BlockSpec((DEC_SEQ, KV_WIDTH), lambda b, i: (row0_k + b, kb)),
        pl.BlockSpec((DEC_SEQ, KV_WIDTH), lambda b, i: (row0_k + b, vb)),
        pl.BlockSpec((None, PAST_LEN, KV_WIDTH), lambda b, i: (b, 0, 0)),
        pl.BlockSpec((None, PAST_LEN, KV_WIDTH), lambda b, i: (b, 0, 0)),
        pl.BlockSpec((tq, KV_WIDTH), lambda b, i: (i, 0)),
        pl.BlockSpec((tq, KV_WIDTH), lambda b, i: (i, 0)),
        pl.BlockSpec((DEC_SEQ, KV_WIDTH), const),
        pl.BlockSpec((DEC_SEQ, KV_WIDTH), const),
    ]
    if sink is not None:
        names.append('sink'); args.append(sink)
        in_specs.append(pl.BlockSpec(memory_space=pltpu.SMEM))
    if qkg2 is not None:
        names.append('qkg'); args.append(qkg2)
        in_specs.append(pl.BlockSpec((2, KV_WIDTH), const))
    aliases = {len(args): 0}
    names.append('prev'); args.append(prev); in_specs.append(_any_spec())
    names += ['o', 'k_scr']
    return pl.pallas_call(
        functools.partial(_attn_lat_kernel, tuple(names), window),
        grid=(DEC_BATCH, nq), in_specs=in_specs,
        out_specs=pl.BlockSpec((tq, GROUP_WIDTH), lambda b, i: (row0_q + b * nq + i, 0)),
        out_shape=jax.ShapeDtypeStruct((N_TOK, GROUP_WIDTH), F32),
        scratch_shapes=[pltpu.VMEM((DEC_SEQ, KV_WIDTH), BF16)],
        input_output_aliases=aliases,
        compiler_params=pltpu.CompilerParams(dimension_semantics=("parallel", "arbitrary")),
        name="attn_latent",
    )(*args)


def _dft_matrices(length):
    n = length
    k = np.arange(n, dtype=np.int64)[:, None]
    s = np.arange(n, dtype=np.int64)[None, :]
    ang = np.pi * ((k * s) % (2 * n)).astype(np.float64) / n
    fwd_cos = np.cos(ang)
    fwd_sin = -np.sin(ang)
    fwd_sin[0, :] = 1.0 - 2.0 * (np.arange(n) % 2)
    fwd = np.concatenate([fwd_cos, fwd_sin], axis=0)
    wk = np.full((n,), 2.0)
    wk[0] = 1.0
    inv_cos = (np.cos(ang) * wk[:, None]).T / (2 * n)
    inv_sin = (-2.0 * np.sin(ang)).T / (2 * n)
    inv_sin[:, 0] = (1.0 - 2.0 * (np.arange(n) % 2)) / (2 * n)
    inv = np.concatenate([inv_cos, inv_sin], axis=1)
    return fwd.astype(np.float32), inv.astype(np.float32)


def _filter_features(length):
    t = np.linspace(0.0, 1.0, length, dtype=np.float32)[:, None]
    w = (np.float32(2.0 * math.pi / length) * np.arange(length, dtype=np.float32))[:, None]
    bands = np.linspace(1e-4, HY_BANDS - 1, HY_BANDS, dtype=np.float32)[None, :]
    z = np.concatenate([t, np.cos(bands * w), -np.sin(bands * w)], axis=-1).astype(np.float32)
    zp = np.zeros((length, 128), np.float32)
    zp[:, :HY_EMB] = z
    deltas = np.linspace(math.log(HY_TARGET) / HY_SLOW, math.log(HY_TARGET) / HY_FAST, HY_CH, dtype=np.float32)
    decay = np.exp(-t * np.abs(deltas)).astype(np.float32)
    return zp, decay


def _filter_kernel(z_ref, w1_ref, b1_ref, w2_ref, b2_ref, w3_ref, fr_ref, dec_ref, f_ref, kr_ref, ki_ref):
    n = z_ref.shape[0]
    h = jnp.sin(fr_ref[0:1, :] * (_fdot(z_ref[...], w1_ref[...]) + b1_ref[...]))
    h = jnp.sin(fr_ref[1:2, :] * (_fdot(h, w2_ref[...]) + b2_ref[...]))
    h = _fdot(h, w3_ref[...])
    dec = dec_ref[...]
    half = HY_ORDER * HY_CH
    pos = jnp.concatenate([h[:, j * HY_CH:(j + 1) * HY_CH] * dec for j in range(HY_ORDER)], axis=-1)
    neg = jnp.concatenate([h[:, half + j * HY_CH:half + (j + 1) * HY_CH] * dec for j in range(HY_ORDER)], axis=-1)
    row = lax.broadcasted_iota(jnp.int32, (n, half), 0)
    neg = jnp.where(row == 0, 0.0, neg)
    fwd = f_ref[...]
    a = jnp.dot(fwd, (pos + neg).astype(BF16), preferred_element_type=F32)
    b = jnp.dot(fwd, (pos - neg).astype(BF16), preferred_element_type=F32)
    kr_ref[...] = a[:n]
    ki_ref[...] = jnp.where(row == 0, a[n:], b[n:])


def _hyena_filters(length, fwd, w1p, b1, w2, b2, w3, freq):
    zp, decay = _filter_features(length)
    half = HY_ORDER * HY_CH
    return pl.pallas_call(
        _filter_kernel,
        out_shape=(jax.ShapeDtypeStruct((length, half), F32), jax.ShapeDtypeStruct((length, half), F32)),
        name="hyena_filters",
    )(zp, w1p, b1.reshape(1, HY_FH), w2, b2.reshape(1, HY_FH), w3, freq, decay, fwd)


def _hyena_kernel(z_ref, cw_ref, cb_ref, f_ref, g_ref, kr_ref, ki_ref, bias_ref, o_ref, *, n):
    nb = z_ref.shape[0] // n
    row = lax.broadcasted_iota(jnp.int32, (n, 3 * HY_CH), 0)
    row0 = lax.broadcasted_iota(jnp.int32, (n, HY_CH), 0) == 0
    vs, x1s, x2s = [], [], []
    for bb in range(nb):
        z = z_ref[bb * n:(bb + 1) * n, :]
        prev = jnp.where(row == 0, 0.0, pltpu.roll(z, 1, 0))
        nxt = jnp.where(row == n - 1, 0.0, pltpu.roll(z, n - 1, 0))
        z = prev * cw_ref[0:1, :] + z * cw_ref[1:2, :] + nxt * cw_ref[2:3, :] + cb_ref[...]
        vs.append(z[:, :HY_CH])
        x1s.append(z[:, HY_CH:2 * HY_CH])
        x2s.append(z[:, 2 * HY_CH:])

    def long_conv(xs, order):
        cols = slice(order * HY_CH, (order + 1) * HY_CH)
        kr, ki = kr_ref[:, cols], ki_ref[:, cols]
        spec = jnp.dot(f_ref[...], jnp.concatenate([x.astype(BF16) for x in xs], axis=1),
                       preferred_element_type=F32)
        prods = []
        for bb in range(nb):
            ur, ui = spec[:n, bb * HY_CH:(bb + 1) * HY_CH], spec[n:, bb * HY_CH:(bb + 1) * HY_CH]
            yr = ur * kr - jnp.where(row0, 0.0, ui * ki)
            yi = jnp.where(row0, ui * ki, ur * ki + ui * kr)
            prods.append(jnp.concatenate([yr, yi], axis=0).astype(BF16))
        y = jnp.dot(g_ref[...], jnp.concatenate(prods, axis=1), preferred_element_type=F32)
        return [y[:, bb * HY_CH:(bb + 1) * HY_CH] + xs[bb] * bias_ref[order:order + 1, :] for bb in range(nb)]

    ys = long_conv(vs, 0)
    ys = long_conv([x1s[bb] * ys[bb] for bb in range(nb)], 1)
    for bb in range(nb):
        o_ref[bb * n:(bb + 1) * n, :] = x2s[bb] * ys[bb]


def _hyena(u, prev, *, latent, fwd, inv, kr, ki, conv_w, conv_b, bias):
    n = DEC_SEQ if latent else SEQ
    nb = 1 if latent else NB_HYENA
    steps = DEC_BATCH if latent else BATCH // nb
    row0 = N_CTX // n if latent else 0
    half = HY_ORDER * HY_CH
    const = lambda b: (0, 0)
    args = [u, conv_w, conv_b.reshape(1, 3 * HY_CH), fwd, inv, kr, ki, bias]
    in_specs = [
        pl.BlockSpec((nb * n, 3 * HY_CH), lambda b: (row0 + b, COL_HY // (3 * HY_CH))),
        pl.BlockSpec((3, 3 * HY_CH), const),
        pl.BlockSpec((1, 3 * HY_CH), const),
        pl.BlockSpec((2 * n, n), const),
        pl.BlockSpec((n, 2 * n), const),
        pl.BlockSpec((n, half), const),
        pl.BlockSpec((n, half), const),
        pl.BlockSpec((HY_ORDER, HY_CH), const),
    ]
    aliases = {}
    if latent:
        aliases = {len(args): 0}
        args.append(prev)
        in_specs.append(_any_spec())

    def body(*refs):
        if latent:
            refs = refs[:len(args) - 1] + refs[len(args):]
        _hyena_kernel(*refs, n=n)

    return pl.pallas_call(
        body, grid=(steps,), in_specs=in_specs,
        out_specs=pl.BlockSpec((nb * n, HY_CH), lambda b: (row0 + b, 0)),
        out_shape=jax.ShapeDtypeStruct((N_TOK, HY_CH), F32),
        input_output_aliases=aliases,
        compiler_params=pltpu.CompilerParams(dimension_semantics=("parallel",)),
        name="hyena_latent" if latent else "hyena_context",
    )(*args)


def _gla_masks():
    t = np.arange(ROWS_GROUP)[:, None]
    s = np.arange(ROWS_GROUP)[None, :]
    same = (t // GLA_CHUNK) == (s // GLA_CHUNK)
    return np.stack([same & (s <= t), same & (s >= t), same]).astype(np.float32)


def _gla_group(r, rows, z, st):
    c = GLA_CHUNK
    hk = H_GLA * DK_GLA
    n_chunks = ROWS_GROUP // c
    logits = _bdot(r['gl'][rows, :], r['gw'][z]) + r['gb'][z:z + 1, :]
    g = (jnp.minimum(logits, 0.0) - jnp.log1p(jnp.exp(-jnp.abs(logits)))) / GLA_NORM
    causal = r['masks'][z]
    b = _split_dot(causal.astype(BF16), g)
    tot = _split_dot(r['masks'][2].astype(BF16), g)
    k = r['k'][rows, :]
    q_t = r['q'][rows, :] * jnp.exp(b) * (DK_GLA ** -0.5)
    k_t = (k * jnp.exp(-b)).astype(BF16)
    k_e = (k * jnp.exp(tot - b)).astype(BF16)
    dec = jnp.exp(tot)
    v = r['v'][rows, :].astype(BF16)

    klane = lax.broadcasted_iota(jnp.int32, (c, hk), 1) // DK_GLA
    vlane = lax.broadcasted_iota(jnp.int32, (c, GROUP_WIDTH), 1) // DV_GLA
    causal_c = jnp.concatenate([causal[0:c, 0:c]] * H_GLA, axis=1) != 0.0
    bd = (lax.broadcasted_iota(jnp.int32, (GROUP_WIDTH, hk), 0) // DV_GLA
          == lax.broadcasted_iota(jnp.int32, (GROUP_WIDTH, hk), 1) // DK_GLA)
    q_bf = q_t.astype(BF16)
    zero_k = jnp.zeros((c, hk), BF16)
    zero_v = jnp.zeros((c, GROUP_WIDTH), BF16)
    o_intra = [None] * n_chunks
    for ci in range(n_chunks):
        cr = slice(ci * c, (ci + 1) * c)
        k_stack = jnp.concatenate([jnp.where(klane == h, k_t[cr], zero_k) for h in range(H_GLA)], axis=0)
        v_diag = jnp.concatenate([jnp.where(vlane == h, v[cr], zero_v) for h in range(H_GLA)], axis=0)
        s = lax.dot_general(q_bf[cr], k_stack, (((1,), (1,)), ((), ())), preferred_element_type=F32)
        p = jnp.where(causal_c, s, 0.0).astype(BF16)
        o_intra[ci] = jnp.dot(p, v_diag, preferred_element_type=F32)

    o = [None] * n_chunks
    for step in range(n_chunks):
        ci = step if z == 0 else n_chunks - 1 - step
        cr = slice(ci * c, (ci + 1) * c)
        o[ci] = o_intra[ci] + _bdot_nt(q_bf[cr], st)
        st = st * dec[ci * c:ci * c + 1, :] + jnp.where(bd, _bdot_tn(v[cr], k_e[cr]), 0.0)
    return jnp.concatenate(o, axis=0), st


def _head_rms(o, g):
    hb = (lax.broadcasted_iota(jnp.int32, (GROUP_WIDTH, GROUP_WIDTH), 0) // DV_GLA
          == lax.broadcasted_iota(jnp.int32, (GROUP_WIDTH, GROUP_WIDTH), 1) // DV_GLA)
    ms = _split_dot_rhs(o * o, hb.astype(BF16)) * (1.0 / DV_GLA)
    return o * lax.rsqrt(ms + EPS) * g


def _split_dot_rhs(x, m):
    hi = x.astype(BF16)
    lo = (x - hi.astype(F32)).astype(BF16)
    return jnp.dot(hi, m, preferred_element_type=F32) + jnp.dot(lo, m, preferred_element_type=F32)


def _expand_state(s):
    hk = H_GLA * DK_GLA
    bd = (lax.broadcasted_iota(jnp.int32, (GROUP_WIDTH, hk), 0) // DV_GLA
          == lax.broadcasted_iota(jnp.int32, (GROUP_WIDTH, hk), 1) // DK_GLA)
    return jnp.where(bd, jnp.concatenate([s] * H_GLA, axis=0), 0.0)


def _compact_state(st):
    out = st[0:DV_GLA]
    for h in range(1, H_GLA):
        out = out + st[h * DV_GLA:(h + 1) * DV_GLA]
    return out


def _gla_ctx_kernel(names, *refs):
    r = dict(zip(names, refs))
    hk = H_GLA * DK_GLA
    for bb in range(NB_CTX):
        rows = slice(bb * SEQ, (bb + 1) * SEQ)
        zero = jnp.zeros((GROUP_WIDTH, hk), F32)
        o_f, st_f = _gla_group(r, rows, 0, zero)
        o_b, st_b = _gla_group(r, rows, 1, zero)
        r['o'][rows, :] = _head_rms(o_f + o_b, r['mg'][...])
        r['sf'][bb, 0] = _compact_state(st_f)
        r['sf'][bb, 1] = _compact_state(st_b)


def _gla_lat_kernel(names, *refs):
    r = dict(zip(names, refs))
    n_groups = DEC_SEQ // ROWS_GROUP

    def run(z):
        r['st'][...] = _expand_state(r['s0'][z])

        def body(step, carry):
            gi = step if z == 0 else n_groups - 1 - step
            rows = pl.ds(pl.multiple_of(gi * ROWS_GROUP, ROWS_GROUP), ROWS_GROUP)
            o, st = _gla_group(r, rows, z, r['st'][...])
            r['st'][...] = st
            if z == 0:
                r['o'][rows, :] = o
            else:
                r['o'][rows, :] = _head_rms(r['o'][rows, :] + o, r['mg'][...])
            return carry

        lax.fori_loop(0, n_groups, body, 0)

    run(0)
    run(1)


def _gla(u, prev, *, latent, gate_w, gate_b, mix_g_d, s0=None):
    n = DEC_SEQ if latent else NB_CTX * SEQ
    steps = DEC_BATCH if latent else BATCH // NB_CTX
    row0 = N_CTX // n if latent else 0
    hk = H_GLA * DK_GLA
    const2 = lambda b: (0, 0)
    names = ['q', 'k', 'v', 'gl', 'gw', 'gb', 'mg', 'masks']
    args = [u, u, u, u, gate_w, gate_b, mix_g_d, _gla_masks()]
    in_specs = [
        pl.BlockSpec((n, hk), lambda b: (row0 + b, COL_QD // hk)),
        pl.BlockSpec((n, hk), lambda b: (row0 + b, COL_KD // hk)),
        pl.BlockSpec((n, GROUP_WIDTH), lambda b: (row0 + b, COL_VD // GROUP_WIDTH)),
        pl.BlockSpec((n, 128), lambda b: (row0 + b, COL_GD // 128)),
        pl.BlockSpec((2, 128, hk), lambda b: (0, 0, 0)),
        pl.BlockSpec((2, hk), const2),
        pl.BlockSpec((1, GROUP_WIDTH), const2),
        pl.BlockSpec((3, ROWS_GROUP, ROWS_GROUP), lambda b: (0, 0, 0)),
    ]
    out_shape = jax.ShapeDtypeStruct((N_TOK, GROUP_WIDTH), F32)
    out_specs = pl.BlockSpec((n, GROUP_WIDTH), lambda b: (row0 + b, 0))
    aliases, scratch = {}, []
    if latent:
        names.append('s0'); args.append(s0)
        in_specs.append(pl.BlockSpec((None, 2, DV_GLA, hk), lambda b: (b, 0, 0, 0)))
        aliases = {len(args): 0}
        names.append('prev'); args.append(prev); in_specs.append(_any_spec())
        names += ['o', 'st']
        scratch = [pltpu.VMEM((GROUP_WIDTH, hk), F32)]
        body = functools.partial(_gla_lat_kernel, tuple(names))
    else:
        names += ['o', 'sf']
        out_shape = (out_shape, jax.ShapeDtypeStruct((BATCH, 2, DV_GLA, hk), F32))
        out_specs = (out_specs, pl.BlockSpec((NB_CTX, 2, DV_GLA, hk), lambda b: (b, 0, 0, 0)))
        body = functools.partial(_gla_ctx_kernel, tuple(names))
    return pl.pallas_call(
        body, grid=(steps,), in_specs=in_specs, out_specs=out_specs, out_shape=out_shape,
        scratch_shapes=scratch, input_output_aliases=aliases,
        compiler_params=pltpu.CompilerParams(dimension_semantics=("parallel",)),
        name="gla_latent" if latent else "gla_context",
    )(*args)


def _merge_kernel(h_ref, oa_ref, ob_ref, oc_ref, od_ref, rd_ref, gt_ref, mg_ref, wo_ref, o_ref):
    gw = GROUP_WIDTH
    ys = [
        _rms(oa_ref[...], mg_ref[:, 0:gw]),
        _rms(ob_ref[...], mg_ref[:, gw:2 * gw]),
        _rms(oc_ref[...], mg_ref[:, 2 * gw:3 * gw]),
        od_ref[...] * _silu(rd_ref[:, COL_RD - COL_GD:COL_RD - COL_GD + gw]),
    ]
    mix = _bdot(ys[0], wo_ref[0:gw, :])
    for j in range(1, 4):
        mix = mix + _bdot(ys[j], wo_ref[j * gw:(j + 1) * gw, :])
    o_ref[...] = h_ref[...] + gt_ref[...] * mix


def _merge(h, oa, ob, oc, od, u, mod, mix_g, w_out, layer):
    tm = TM_PROJ
    grp = pl.BlockSpec((tm, GROUP_WIDTH), lambda i: (i, 0))
    return pl.pallas_call(
        _merge_kernel,
        grid=(N_TOK // tm,),
        in_specs=[
            pl.BlockSpec((tm, D_MODEL), lambda i: (i, 0)),
            grp, grp, grp, grp,
            pl.BlockSpec((tm, COL_RD_BLOCK), lambda i: (i, COL_GD // COL_RD_BLOCK)),
            _mod_spec(tm, 5, 1),
            pl.BlockSpec((None, 1, D_MODEL), lambda i: (layer, 0, 0)),
            pl.BlockSpec((None, D_MODEL, D_MODEL), lambda i: (layer, 0, 0)),
        ],
        out_specs=pl.BlockSpec((tm, D_MODEL), lambda i: (i, 0)),
        out_shape=jax.ShapeDtypeStruct((N_TOK, D_MODEL), F32),
        compiler_params=pltpu.CompilerParams(dimension_semantics=("parallel",)),
        name="merge",
    )(h, oa, ob, oc, od, u, mod, mix_g, w_out)


def _mix_order(x, axis):
    blocks = []
    for g in range(4):
        order = HEAD_ORDER if g in (0, 2) else range(N_HEADS)
        for h in order:
            start = g * GROUP_WIDTH + h * HEAD_DIM
            blocks.append(lax.slice_in_dim(x, start, start + HEAD_DIM, axis=axis))
    return jnp.concatenate(blocks, axis=axis)


def _gate_weights(gate_w):
    out = jnp.zeros((2, 128, H_GLA * DK_GLA), gate_w.dtype)
    out = out.at[0, 0:GLA_RANK].set(gate_w[0])
    return out.at[1, GLA_RANK:2 * GLA_RANK].set(gate_w[1])


def _states_to_kernel(st):
    b = st.shape[0]
    return st.transpose(0, 1, 4, 2, 3).reshape(b, 2, DV_GLA, H_GLA * DK_GLA)


def _states_from_kernel(st):
    b = st.shape[0]
    return st.reshape(b, 2, DV_GLA, H_GLA, DK_GLA).transpose(0, 1, 3, 4, 2)


def kernel(x_prompt, x_sample, cache_swa_k, cache_swa_v, cache_gqa_k, cache_gqa_v, state_gla, c, c_ctx, w_mod, b_mod, norm_g, ffn_w_in, ffn_w_out, w_in, w_out, mix_g, swa_sink, qk_norm_g, hy_conv_w, hy_conv_b, hy_w1, hy_b1, hy_w2, hy_b2, hy_w3, hy_freq, hy_bias, gla_gate_w, gla_gate_b, final_g):
    cond = jnp.zeros((MOD_ROWS, D_MODEL), F32).at[0].set(c_ctx).at[1:1 + DEC_BATCH].set(c)
    mod_all = _modulation(cond, w_mod, b_mod).reshape(DEPTH, MOD_ROWS, N_MOD, 1, D_MODEL)

    dft = {n: tuple(jnp.asarray(m).astype(BF16) for m in _dft_matrices(n)) for n in (SEQ, DEC_SEQ)}

    xs = [x_prompt.reshape(N_CTX, D_MODEL), x_sample.reshape(N_LAT, D_MODEL)]
    ffn_in = ffn_w_in.astype(BF16)
    ffn_out = ffn_w_out.astype(BF16)
    ng = norm_g.reshape(DEPTH, 3, 1, D_MODEL)
    proj_w = w_in.astype(BF16)
    out_w = _mix_order(w_out.astype(BF16), 1)
    mix_gp = _mix_order(mix_g, 1).reshape(DEPTH, 1, D_MODEL)
    cache_a = cache_c = None
    sts = []
    for l in range(DEPTH):
        mod = mod_all[l]
        h = _ffn(xs, mod, 0, ng, ffn_in, ffn_out, l, 0)
        u = _in_proj(h, mod, ng, proj_w, l)

        ck_a = cache_swa_k[:, l].reshape(DEC_BATCH, PAST_LEN, KV_WIDTH)
        cv_a = cache_swa_v[:, l].reshape(DEC_BATCH, PAST_LEN, KV_WIDTH)
        ck_c = cache_gqa_k[:, l].reshape(DEC_BATCH, PAST_LEN, KV_WIDTH)
        cv_c = cache_gqa_v[:, l].reshape(DEC_BATCH, PAST_LEN, KV_WIDTH)
        qkg2 = jnp.tile(qk_norm_g[l], (1, N_KV))

        oa, *cache_a = _attention_context(u, l, cache_a, col_q=COL_QA, col_k=COL_KA, col_v=COL_VA,
                                          sink=swa_sink[l])
        oa = _attention_latent(u, oa, col_q=COL_QA, col_k=COL_KA, col_v=COL_VA, cache_k=ck_a, cache_v=cv_a,
                               window=True, sink=swa_sink[l])
        oc, *cache_c = _attention_context(u, l, cache_c, col_q=COL_QC, col_k=COL_KC, col_v=COL_VC, qkg2=qkg2)
        oc = _attention_latent(u, oc, col_q=COL_QC, col_k=COL_KC, col_v=COL_VC, cache_k=ck_c, cache_v=cv_c,
                               qkg2=qkg2)

        w1p = jnp.zeros((128, HY_FH), F32).at[:HY_EMB].set(hy_w1[l])
        ob = None
        for latent, n in ((False, SEQ), (True, DEC_SEQ)):
            fwd, inv = dft[n]
            kr, ki = _hyena_filters(n, fwd, w1p, hy_b1[l], hy_w2[l], hy_b2[l], hy_w3[l], hy_freq[l])
            ob = _hyena(u, ob, latent=latent, fwd=fwd, inv=inv, kr=kr, ki=ki, conv_w=hy_conv_w[l],
                        conv_b=hy_conv_b[l], bias=hy_bias[l])

        gw = _gate_weights(gla_gate_w[l]).astype(BF16)
        mg_d = mix_g[l, 3 * GROUP_WIDTH:].reshape(1, GROUP_WIDTH)
        od, st = _gla(u, None, latent=False, gate_w=gw, gate_b=gla_gate_b[l], mix_g_d=mg_d)
        od = _gla(u, od, latent=True, gate_w=gw, gate_b=gla_gate_b[l], mix_g_d=mg_d,
                  s0=_states_to_kernel(state_gla[:, l]))

        h = _merge(h, oa, ob, oc, od, u, mod, mix_gp, out_w, l)
        xs = [_ffn([h], mod, 6, ng, ffn_in, ffn_out, l, 1)] if l + 1 < DEPTH else None
        if xs is None:
            y_prompt, y_sample = _ffn([h], mod, 6, ng, ffn_in, ffn_out, l, 1, final_g=final_g)
        sts.append(_states_from_kernel(st))

    y_prompt = y_prompt.reshape(BATCH, SEQ, D_MODEL)
    y_sample = y_sample.reshape(DEC_BATCH, DEC_SEQ, D_MODEL)
    caches = [x.reshape(BATCH, DEPTH, SEQ, N_KV, HEAD_DIM) for x in (*cache_a, *cache_c)]
    return (y_prompt, y_sample, *caches, jnp.stack(sts, axis=1))
```

```python
import functools
import math

import numpy as np
import jax
import jax.numpy as jnp
from jax import lax
from jax.experimental import pallas as pl
from jax.experimental.pallas import tpu as pltpu

F32 = jnp.float32
BF16 = jnp.bfloat16

D_MODEL = 1024
BATCH = 32
SEQ = 256
DEPTH = 2
DEC_BATCH = 2
DEC_SEQ = 1024
PAST_LEN = 512
GRID_W = 64
HEAD_DIM = 64
GROUP_WIDTH = 256
N_HEADS = 4
N_KV = 2
KV_WIDTH = N_KV * HEAD_DIM
WINDOW = 128
HY_CH = 256
HY_ORDER = 2
HY_BANDS = 16
HY_EMB = 2 * HY_BANDS + 1
HY_FH = 64
HY_TARGET = 1e-2
HY_FAST = 0.3
HY_SLOW = 1.5
H_GLA = 4
DV_GLA = 64
DK_GLA = 32
GLA_RANK = 16
GLA_CHUNK = 64
GLA_NORM = 16.0
D_FF = 2816
N_MOD = 9
ROPE_THETA = 10000.0
EPS = 1e-6

N_CTX = BATCH * SEQ
N_LAT = DEC_BATCH * DEC_SEQ
N_TOK = N_CTX + N_LAT
MOD_ROWS = 8

COL_HY, COL_QA, COL_KA, COL_VA = 0, 768, 1024, 1152
COL_QC, COL_KC, COL_VC = 1280, 1536, 1664
COL_QD, COL_KD, COL_VD, COL_GD, COL_RD = 1792, 1920, 2048, 2304, 2336
PROJ_WIDTH = 2688
REF_QA, REF_KA, REF_HY, REF_QC, REF_KC, REF_END = 0, 256, 512, 1280, 1536, 2592
COL_RD_BLOCK = 384
HEAD_ORDER = (0, 2, 1, 3)
_HEAD_PERM = np.concatenate([np.arange(HEAD_DIM) + HEAD_DIM * h for h in HEAD_ORDER])

TM_FFN = 1024
TK_FFN = 256
FFN_VMEM_BYTES = 60 * 1024 * 1024
TM_PROJ = 1024
TQ_ATTN = 256
ROWS_GROUP = 256
NB_CTX = 2
NB_HYENA = 4


def _bdot(a, b):
    return jnp.dot(a.astype(BF16), b.astype(BF16), preferred_element_type=F32)


def _bdot_nt(a, b):
    return lax.dot_general(a.astype(BF16), b.astype(BF16), (((1,), (1,)), ((), ())),
                           preferred_element_type=F32)


def _bdot_tn(a, b):
    return lax.dot_general(a.astype(BF16), b.astype(BF16), (((0,), (0,)), ((), ())),
                           preferred_element_type=F32)


def _fdot(a, b):
    a_hi, b_hi = a.astype(BF16), b.astype(BF16)
    a_lo = (a - a_hi.astype(F32)).astype(BF16)
    b_lo = (b - b_hi.astype(F32)).astype(BF16)
    dot = functools.partial(jnp.dot, preferred_element_type=F32)
    return dot(a_hi, b_hi) + (dot(a_hi, b_lo) + dot(a_lo, b_hi) + dot(a_lo, b_lo))


def _split_dot(m, x):
    hi = x.astype(BF16)
    lo = (x - hi.astype(F32)).astype(BF16)
    return jnp.dot(m, hi, preferred_element_type=F32) + jnp.dot(m, lo, preferred_element_type=F32)


def _rms(x, g):
    return x * lax.rsqrt(jnp.mean(x * x, axis=-1, keepdims=True) + EPS) * g


def _silu(x):
    return x * jax.nn.sigmoid(x)


def _mod_row(i, tm):
    n_ctx_tiles = N_CTX // tm
    per_batch = DEC_SEQ // tm
    return jnp.where(i < n_ctx_tiles, 0, 1 + (i - n_ctx_tiles) // per_batch)


def _mod_spec(tm, j, grid_rank):
    if grid_rank == 1:
        return pl.BlockSpec((None, None, 1, D_MODEL), lambda i: (_mod_row(i, tm), j, 0, 0))
    return pl.BlockSpec((None, None, 1, D_MODEL), lambda i, k: (_mod_row(i, tm), j, 0, 0))


def _any_spec():
    return pl.BlockSpec(memory_space=pl.ANY)


def _mod_kernel(cond_ref, w_ref, b_ref, o_ref):
    o_ref[...] = _bdot(_silu(cond_ref[...]), w_ref[...]) + b_ref[...]


def _modulation(cond, w_mod, b_mod):
    tn = D_MODEL
    return pl.pallas_call(
        _mod_kernel,
        grid=(DEPTH, N_MOD * D_MODEL // tn),
        in_specs=[
            pl.BlockSpec((MOD_ROWS, D_MODEL), lambda l, j: (0, 0)),
            pl.BlockSpec((None, D_MODEL, tn), lambda l, j: (l, 0, j)),
            pl.BlockSpec((None, 1, tn), lambda l, j: (l, 0, j)),
        ],
        out_specs=pl.BlockSpec((None, MOD_ROWS, tn), lambda l, j: (l, 0, j)),
        out_shape=jax.ShapeDtypeStruct((DEPTH, MOD_ROWS, N_MOD * D_MODEL), F32),
        name="modulation",
    )(cond, w_mod, b_mod.reshape(DEPTH, 1, N_MOD * D_MODEL))


def _ffn_kernel(split_input, final, *refs):
    is_ctx = pl.program_id(0) < N_CTX // TM_FFN
    if split_input:
        xa_ref, xb_ref, *refs = refs
        x = jnp.where(is_ctx, xa_ref[...], xb_ref[...])
    else:
        x_ref, *refs = refs
        x = x_ref[...]
    if final:
        sh_ref, sc_ref, gt_ref, ng_ref, wi_ref, wo_ref, fg_ref, yc_ref, yl_ref, xm_ref, act_ref = refs
    else:
        sh_ref, sc_ref, gt_ref, ng_ref, wi_ref, wo_ref, o_ref, xm_ref, act_ref = refs
    xn = _rms(x, ng_ref[...])
    xm_ref[...] = (xn * (1.0 + sc_ref[...]) + sh_ref[...]).astype(BF16)
    tk = TK_FFN
    for j in range(D_FF // tk):
        xm = xm_ref[...]
        a = jnp.dot(xm, wi_ref[:, j * tk:(j + 1) * tk], preferred_element_type=F32)
        b = jnp.dot(xm, wi_ref[:, D_FF + j * tk:D_FF + (j + 1) * tk], preferred_element_type=F32)
        act_ref[:, j * tk:(j + 1) * tk] = (_silu(a) * b).astype(BF16)
    y = jnp.dot(act_ref[...], wo_ref[...], preferred_element_type=F32)
    out = x + 0.5 * gt_ref[...] * y
    if not final:
        o_ref[...] = out
        return
    out = _rms(out, fg_ref[...])

    @pl.when(is_ctx)
    def _():
        yc_ref[...] = out

    @pl.when(jnp.logical_not(is_ctx))
    def _():
        yl_ref[...] = out


def _ffn(xs, mod, j0, ng, w_in, w_out, layer, which, final_g=None):
    tm = TM_FFN
    n_ctx_tiles = N_CTX // tm
    split_specs = [pl.BlockSpec((tm, D_MODEL), lambda i: (jnp.minimum(i, n_ctx_tiles - 1), 0)),
                   pl.BlockSpec((tm, D_MODEL), lambda i: (jnp.maximum(i - n_ctx_tiles, 0), 0))]
    whole_spec = pl.BlockSpec((tm, D_MODEL), lambda i: (i, 0))
    x_specs = split_specs if len(xs) == 2 else [whole_spec]
    resident = dict(pipeline_mode=pl.Buffered(1))
    final = final_g is not None
    extra_args, extra_specs = [], []
    if final:
        extra_args = [final_g.reshape(1, D_MODEL)]
        extra_specs = [pl.BlockSpec((1, D_MODEL), lambda i: (0, 0))]
        out_specs = tuple(split_specs)
        out_shape = (jax.ShapeDtypeStruct((N_CTX, D_MODEL), F32), jax.ShapeDtypeStruct((N_LAT, D_MODEL), F32))
    else:
        out_specs = whole_spec
        out_shape = jax.ShapeDtypeStruct((N_TOK, D_MODEL), F32)
    return pl.pallas_call(
        functools.partial(_ffn_kernel, len(xs) == 2, final),
        grid=(N_TOK // tm,),
        in_specs=x_specs + [
            _mod_spec(tm, j0, 1), _mod_spec(tm, j0 + 1, 1), _mod_spec(tm, j0 + 2, 1),
            pl.BlockSpec((None, None, 1, D_MODEL), lambda i: (layer, j0 // 3, 0, 0)),
            pl.BlockSpec((None, None, D_MODEL, 2 * D_FF), lambda i: (layer, which, 0, 0), **resident),
            pl.BlockSpec((None, None, D_FF, D_MODEL), lambda i: (layer, which, 0, 0), **resident),
        ] + extra_specs,
        out_specs=out_specs,
        out_shape=out_shape,
        scratch_shapes=[pltpu.VMEM((tm, D_MODEL), BF16), pltpu.VMEM((tm, D_FF), BF16)],
        compiler_params=pltpu.CompilerParams(dimension_semantics=("arbitrary",),
                                             vmem_limit_bytes=FFN_VMEM_BYTES),
        name="ffn",
    )(*xs, mod, mod, mod, ng, w_in, w_out, *extra_args)


def _proj_kernel(x_ref, sh_ref, sc_ref, ng_ref, w_ref, o_ref):
    xn = _rms(x_ref[...], ng_ref[...])
    xm = (xn * (1.0 + sc_ref[...]) + sh_ref[...]).astype(BF16)

    def put(col, w):
        o_ref[:, col:col + w.shape[1]] = jnp.dot(xm, w, preferred_element_type=F32)

    def queries(base):
        return jnp.concatenate([w_ref[:, base + HEAD_DIM * h:base + HEAD_DIM * (h + 1)] for h in HEAD_ORDER], axis=1)

    put(COL_HY, w_ref[:, REF_HY:REF_QC])
    put(COL_QA, queries(REF_QA))
    put(COL_KA, w_ref[:, REF_KA:REF_HY])
    put(COL_QC, queries(REF_QC))
    put(COL_KC, w_ref[:, REF_KC:COL_GD])
    put(COL_GD, w_ref[:, COL_GD:REF_END])
    o_ref[:, REF_END:PROJ_WIDTH] = jnp.zeros((x_ref.shape[0], PROJ_WIDTH - REF_END), F32)


def _in_proj(h, mod, ng, w, layer):
    tm = TM_PROJ
    return pl.pallas_call(
        _proj_kernel,
        grid=(N_TOK // tm,),
        in_specs=[
            pl.BlockSpec((tm, D_MODEL), lambda i: (i, 0)),
            _mod_spec(tm, 3, 1), _mod_spec(tm, 4, 1),
            pl.BlockSpec((None, None, 1, D_MODEL), lambda i: (layer, 1, 0, 0)),
            pl.BlockSpec((None, D_MODEL, REF_END), lambda i: (layer, 0, 0)),
        ],
        out_specs=pl.BlockSpec((tm, PROJ_WIDTH), lambda i: (i, 0)),
        out_shape=jax.ShapeDtypeStruct((N_TOK, PROJ_WIDTH), F32),
        compiler_params=pltpu.CompilerParams(dimension_semantics=("parallel",)),
        name="in_proj",
    )(h, mod, mod, ng, w)


def _rope_tables(length):
    rows = length // GRID_W
    r = np.repeat(np.arange(rows, dtype=np.float32), GRID_W)
    col = np.tile(np.arange(GRID_W, dtype=np.float32), rows)
    nf = HEAD_DIM // 4
    inv = (np.float32(ROPE_THETA) ** (-np.arange(nf, dtype=np.float32) / nf)).astype(np.float32)
    ang = np.concatenate([r[:, None] * inv, col[:, None] * inv], axis=-1).astype(np.float32)
    cos, sin = np.cos(ang).astype(np.float32), np.sin(ang).astype(np.float32)
    return np.tile(cos, (1, 4)), np.tile(np.concatenate([-sin, sin], axis=-1), (1, 2))


def _pair_lanes(rows):
    lane = lax.broadcasted_iota(jnp.int32, (rows, 2 * HEAD_DIM), 1)
    return lane < HEAD_DIM, (lane % HEAD_DIM) < HEAD_DIM // 2


def _rope_pair(x, cos, sin_signed, first_half):
    partner = jnp.where(first_half, pltpu.roll(x, 3 * HEAD_DIM // 2, 1), pltpu.roll(x, HEAD_DIM // 2, 1))
    return x * cos + partner * sin_signed


def _rms_pair(x, g, lo):
    x2 = x * x
    s_lo = jnp.sum(jnp.where(lo, x2, 0.0), axis=-1, keepdims=True)
    s_hi = jnp.sum(jnp.where(lo, 0.0, x2), axis=-1, keepdims=True)
    ms = jnp.where(lo, s_lo, s_hi) * (1.0 / HEAD_DIM)
    return x * lax.rsqrt(ms + EPS) * g


def _attn_core(q, k_bf, v_bf, lo, *, mask=None, ctx=None, sinks=None):
    tq = q.shape[0]
    pair = 2 * HEAD_DIM
    scale = HEAD_DIM ** -0.5
    qa, qb = q[:, :pair] * scale, q[:, pair:] * scale
    qs = jnp.concatenate([jnp.where(lo, qa, 0.0), jnp.where(lo, 0.0, qa),
                          jnp.where(lo, qb, 0.0), jnp.where(lo, 0.0, qb)], axis=0).astype(BF16)
    s = _bdot_nt(qs, k_bf)
    if ctx is not None:
        s_c = _bdot_nt(qs, ctx[0])
    ps, pcs, dens = [], [], []
    for j in range(N_HEADS):
        rows = slice(j * tq, (j + 1) * tq)
        sj = s[rows]
        if mask is not None:
            sj = jnp.where(mask, sj, -1e30)
        m = jnp.max(sj, axis=-1, keepdims=True)
        if ctx is not None:
            m = jnp.maximum(m, jnp.max(s_c[rows], axis=-1, keepdims=True))
        if sinks is not None:
            m = jnp.maximum(m, sinks[j])
        p = jnp.exp(sj - m)
        den = jnp.sum(p, axis=-1, keepdims=True)
        ps.append(p.astype(BF16))
        if ctx is not None:
            pc = jnp.exp(s_c[rows] - m)
            den = den + jnp.sum(pc, axis=-1, keepdims=True)
            pcs.append(pc.astype(BF16))
        if sinks is not None:
            den = den + jnp.exp(sinks[j] - m)
        dens.append(den)
    r = jnp.dot(jnp.concatenate(ps, axis=0), v_bf, preferred_element_type=F32)
    if ctx is not None:
        r = r + jnp.dot(jnp.concatenate(pcs, axis=0), ctx[1], preferred_element_type=F32)
    o = [r[j * tq:(j + 1) * tq] / dens[j] for j in range(N_HEADS)]
    return jnp.concatenate([jnp.where(lo, o[0], o[1]), jnp.where(lo, o[2], o[3])], axis=1)


def _attn_ctx_kernel(names, *refs):
    r = dict(zip(names, refs))
    lo, _ = _pair_lanes(SEQ)
    sinks = [r['sink'][h] for h in HEAD_ORDER] if 'sink' in r else None
    for bb in range(NB_CTX):
        rows = slice(bb * SEQ, (bb + 1) * SEQ)
        q, k, v = r['q'][rows, :], r['k'][rows, :], r['v'][rows, :]
        if 'qkg' in r:
            gq, gk = r['qkg'][0:1, :], r['qkg'][1:2, :]
            q = jnp.concatenate([_rms_pair(q[:, :KV_WIDTH], gq, lo), _rms_pair(q[:, KV_WIDTH:], gq, lo)], axis=1)
            k = _rms_pair(k, gk, lo)
        r['k_out'][bb] = k
        r['v_out'][bb] = v
        r['o'][rows, :] = _attn_core(q, k.astype(BF16), v.astype(BF16), lo, sinks=sinks)


def _attn_lat_kernel(names, window, *refs):
    r = dict(zip(names, refs))
    tq = TQ_ATTN
    i = pl.program_id(1)
    lo, first_half = _pair_lanes(tq)
    qk_norm = 'qkg' in r

    @pl.when(i == 0)
    def _():
        lo_k, first_half_k = _pair_lanes(DEC_SEQ)
        k = r['k'][...]
        if qk_norm:
            k = _rms_pair(k, r['qkg'][1:2, :], lo_k)
        r['k_scr'][...] = _rope_pair(k, r['cos_k'][...], r['sin_k'][...], first_half_k).astype(BF16)

    q = r['q'][...]
    halves = []
    for c0 in (0, KV_WIDTH):
        x = q[:, c0:c0 + KV_WIDTH]
        if qk_norm:
            x = _rms_pair(x, r['qkg'][0:1, :], lo)
        halves.append(_rope_pair(x, r['cos_q'][...], r['sin_q'][...], first_half))
    q = jnp.concatenate(halves, axis=1)
    sinks = [r['sink'][h] for h in HEAD_ORDER] if 'sink' in r else None
    ctx = (r['kc'][...].astype(BF16), r['vc'][...].astype(BF16))
    if window:
        span = tq + 2 * WINDOW
        start = pl.multiple_of(jnp.clip(i * tq - WINDOW, 0, DEC_SEQ - span), WINDOW)
        qpos = i * tq + lax.broadcasted_iota(jnp.int32, (tq, span), 0)
        kpos = start + lax.broadcasted_iota(jnp.int32, (tq, span), 1)
        mask = jnp.abs(qpos - kpos) <= WINDOW
        k_bf = r['k_scr'][pl.ds(start, span), :]
        v_bf = r['v'][pl.ds(start, span), :].astype(BF16)
    else:
        mask = None
        k_bf = r['k_scr'][...]
        v_bf = r['v'][...].astype(BF16)
    r['o'][...] = _attn_core(q, k_bf, v_bf, lo, mask=mask, ctx=ctx, sinks=sinks)


def _attention_context(u, layer, cache_prev, *, col_q, col_k, col_v, sink=None, qkg2=None):
    rows = NB_CTX * SEQ
    qb, kb, vb = col_q // GROUP_WIDTH, col_k // KV_WIDTH, col_v // KV_WIDTH
    names = ['q', 'k', 'v']
    args = [u, u, u]
    in_specs = [pl.BlockSpec((rows, GROUP_WIDTH), lambda b: (b, qb)),
                pl.BlockSpec((rows, KV_WIDTH), lambda b: (b, kb)),
                pl.BlockSpec((rows, KV_WIDTH), lambda b: (b, vb))]
    if sink is not None:
        names.append('sink'); args.append(sink)
        in_specs.append(pl.BlockSpec(memory_space=pltpu.SMEM))
    if qkg2 is not None:
        names.append('qkg'); args.append(qkg2)
        in_specs.append(pl.BlockSpec((2, KV_WIDTH), lambda b: (0, 0)))
    aliases = {}
    if cache_prev is not None:
        for j, prev in enumerate(cache_prev):
            aliases[len(args)] = 1 + j
            names.append(f'prev{j}'); args.append(prev); in_specs.append(_any_spec())
    names += ['o', 'k_out', 'v_out']
    cache_shape = jax.ShapeDtypeStruct((BATCH, DEPTH, SEQ, KV_WIDTH), F32)
    cache_spec = pl.BlockSpec((NB_CTX, None, SEQ, KV_WIDTH), lambda b: (b, layer, 0, 0))
    return pl.pallas_call(
        functools.partial(_attn_ctx_kernel, tuple(names)),
        grid=(BATCH // NB_CTX,), in_specs=in_specs,
        out_specs=(pl.BlockSpec((rows, GROUP_WIDTH), lambda b: (b, 0)), cache_spec, cache_spec),
        out_shape=(jax.ShapeDtypeStruct((N_TOK, GROUP_WIDTH), F32), cache_shape, cache_shape),
        input_output_aliases=aliases,
        compiler_params=pltpu.CompilerParams(dimension_semantics=("parallel",)),
        name="attn_context",
    )(*args)


def _attention_latent(u, prev, *, col_q, col_k, col_v, cache_k, cache_v, window=False, sink=None, qkg2=None):
    tq = TQ_ATTN
    nq = DEC_SEQ // tq
    row0_q, row0_k = N_CTX // tq, N_CTX // DEC_SEQ
    qb, kb, vb = col_q // GROUP_WIDTH, col_k // KV_WIDTH, col_v // KV_WIDTH
    cos, sin = _rope_tables(DEC_SEQ)
    const = lambda b, i: (0, 0)
    names = ['q', 'k', 'v', 'kc', 'vc', 'cos_q', 'sin_q', 'cos_k', 'sin_k']
    args = [u, u, u, cache_k, cache_v, cos, sin, cos, sin]
    in_specs = [
        pl.BlockSpec((tq, GROUP_WIDTH), lambda b, i: (row0_q + b * nq + i, qb)),
        pl.BlockSpec((DEC_SEQ, KV_WIDTH), lambda b, i: (row0_k + b, kb)),
        pl.BlockSpec((DEC_SEQ, KV_WIDTH), lambda b, i: (row0_k + b, vb)),
        pl.BlockSpec((None, PAST_LEN, KV_WIDTH), lambda b, i: (b, 0, 0)),
        pl.BlockSpec((None, PAST_LEN, KV_WIDTH), lambda b, i: (b, 0, 0)),
        pl.BlockSpec((tq, KV_WIDTH), lambda b, i: (i, 0)),
        pl.BlockSpec((tq, KV_WIDTH), lambda b, i: (i, 0)),
        pl.BlockSpec((DEC_SEQ, KV_WIDTH), const),
        pl.BlockSpec((DEC_SEQ, KV_WIDTH), const),
    ]
    if sink is not None:
        names.append('sink'); args.append(sink)
        in_specs.append(pl.BlockSpec(memory_space=pltpu.SMEM))
    if qkg2 is not None:
        names.append('qkg'); args.append(qkg2)
        in_specs.append(pl.BlockSpec((2, KV_WIDTH), const))
    aliases = {len(args): 0}
    names.append('prev'); args.append(prev); in_specs.append(_any_spec())
    names += ['o', 'k_scr']
    return pl.pallas_call(
        functools.partial(_attn_lat_kernel, tuple(names), window),
        grid=(DEC_BATCH, nq), in_specs=in_specs,
        out_specs=pl.BlockSpec((tq, GROUP_WIDTH), lambda b, i: (row0_q + b * nq + i, 0)),
        out_shape=jax.ShapeDtypeStruct((N_TOK, GROUP_WIDTH), F32),
        scratch_shapes=[pltpu.VMEM((DEC_SEQ, KV_WIDTH), BF16)],
        input_output_aliases=aliases,
        compiler_params=pltpu.CompilerParams(dimension_semantics=("parallel", "arbitrary")),
        name="attn_latent",
    )(*args)


def _dft_matrices(length):
    n = length
    k = np.arange(n, dtype=np.int64)[:, None]
    s = np.arange(n, dtype=np.int64)[None, :]
    ang = np.pi * ((k * s) % (2 * n)).astype(np.float64) / n
    fwd_cos = np.cos(ang)
    fwd_sin = -np.sin(ang)
    fwd_sin[0, :] = 1.0 - 2.0 * (np.arange(n) % 2)
    fwd = np.concatenate([fwd_cos, fwd_sin], axis=0)
    wk = np.full((n,), 2.0)
    wk[0] = 1.0
    inv_cos = (np.cos(ang) * wk[:, None]).T / (2 * n)
    inv_sin = (-2.0 * np.sin(ang)).T / (2 * n)
    inv_sin[:, 0] = (1.0 - 2.0 * (np.arange(n) % 2)) / (2 * n)
    inv = np.concatenate([inv_cos, inv_sin], axis=1)
    return fwd.astype(np.float32), inv.astype(np.float32)


def _filter_features(length):
    t = np.linspace(0.0, 1.0, length, dtype=np.float32)[:, None]
    w = (np.float32(2.0 * math.pi / length) * np.arange(length, dtype=np.float32))[:, None]
    bands = np.linspace(1e-4, HY_BANDS - 1, HY_BANDS, dtype=np.float32)[None, :]
    z = np.concatenate([t, np.cos(bands * w), -np.sin(bands * w)], axis=-1).astype(np.float32)
    zp = np.zeros((length, 128), np.float32)
    zp[:, :HY_EMB] = z
    deltas = np.linspace(math.log(HY_TARGET) / HY_SLOW, math.log(HY_TARGET) / HY_FAST, HY_CH, dtype=np.float32)
    decay = np.exp(-t * np.abs(deltas)).astype(np.float32)
    return zp, decay


def _filter_kernel(z_ref, w1_ref, b1_ref, w2_ref, b2_ref, w3_ref, fr_ref, dec_ref, f_ref, kr_ref, ki_ref):
    n = z_ref.shape[0]
    h = jnp.sin(fr_ref[0:1, :] * (_fdot(z_ref[...], w1_ref[...]) + b1_ref[...]))
    h = jnp.sin(fr_ref[1:2, :] * (_fdot(h, w2_ref[...]) + b2_ref[...]))
    h = _fdot(h, w3_ref[...])
    dec = dec_ref[...]
    half = HY_ORDER * HY_CH
    pos = jnp.concatenate([h[:, j * HY_CH:(j + 1) * HY_CH] * dec for j in range(HY_ORDER)], axis=-1)
    neg = jnp.concatenate([h[:, half + j * HY_CH:half + (j + 1) * HY_CH] * dec for j in range(HY_ORDER)], axis=-1)
    row = lax.broadcasted_iota(jnp.int32, (n, half), 0)
    neg = jnp.where(row == 0, 0.0, neg)
    fwd = f_ref[...]
    a = jnp.dot(fwd, (pos + neg).astype(BF16), preferred_element_type=F32)
    b = jnp.dot(fwd, (pos - neg).astype(BF16), preferred_element_type=F32)
    kr_ref[...] = a[:n]
    ki_ref[...] = jnp.where(row == 0, a[n:], b[n:])


def _hyena_filters(length, fwd, w1p, b1, w2, b2, w3, freq):
    zp, decay = _filter_features(length)
    half = HY_ORDER * HY_CH
    return pl.pallas_call(
        _filter_kernel,
        out_shape=(jax.ShapeDtypeStruct((length, half), F32), jax.ShapeDtypeStruct((length, half), F32)),
        name="hyena_filters",
    )(zp, w1p, b1.reshape(1, HY_FH), w2, b2.reshape(1, HY_FH), w3, freq, decay, fwd)


def _hyena_kernel(z_ref, cw_ref, cb_ref, f_ref, g_ref, kr_ref, ki_ref, bias_ref, o_ref, *, n):
    nb = z_ref.shape[0] // n
    row = lax.broadcasted_iota(jnp.int32, (n, 3 * HY_CH), 0)
    row0 = lax.broadcasted_iota(jnp.int32, (n, HY_CH), 0) == 0
    vs, x1s, x2s = [], [], []
    for bb in range(nb):
        z = z_ref[bb * n:(bb + 1) * n, :]
        prev = jnp.where(row == 0, 0.0, pltpu.roll(z, 1, 0))
        nxt = jnp.where(row == n - 1, 0.0, pltpu.roll(z, n - 1, 0))
        z = prev * cw_ref[0:1, :] + z * cw_ref[1:2, :] + nxt * cw_ref[2:3, :] + cb_ref[...]
        vs.append(z[:, :HY_CH])
        x1s.append(z[:, HY_CH:2 * HY_CH])
        x2s.append(z[:, 2 * HY_CH:])

    def long_conv(xs, order):
        cols = slice(order * HY_CH, (order + 1) * HY_CH)
        kr, ki = kr_ref[:, cols], ki_ref[:, cols]
        spec = jnp.dot(f_ref[...], jnp.concatenate([x.astype(BF16) for x in xs], axis=1),
                       preferred_element_type=F32)
        prods = []
        for bb in range(nb):
            ur, ui = spec[:n, bb * HY_CH:(bb + 1) * HY_CH], spec[n:, bb * HY_CH:(bb + 1) * HY_CH]
            yr = ur * kr - jnp.where(row0, 0.0, ui * ki)
            yi = jnp.where(row0, ui * ki, ur * ki + ui * kr)
            prods.append(jnp.concatenate([yr, yi], axis=0).astype(BF16))
        y = jnp.dot(g_ref[...], jnp.concatenate(prods, axis=1), preferred_element_type=F32)
        return [y[:, bb * HY_CH:(bb + 1) * HY_CH] + xs[bb] * bias_ref[order:order + 1, :] for bb in range(nb)]

    ys = long_conv(vs, 0)
    ys = long_conv([x1s[bb] * ys[bb] for bb in range(nb)], 1)
    for bb in range(nb):
        o_ref[bb * n:(bb + 1) * n, :] = x2s[bb] * ys[bb]


def _hyena(u, prev, *, latent, fwd, inv, kr, ki, conv_w, conv_b, bias):
    n = DEC_SEQ if latent else SEQ
    nb = DEC_BATCH if latent else NB_HYENA
    steps = 1 if latent else BATCH // nb
    row0 = N_CTX // (nb * n) if latent else 0
    half = HY_ORDER * HY_CH
    const = lambda b: (0, 0)
    args = [u, conv_w, conv_b.reshape(1, 3 * HY_CH), fwd, inv, kr, ki, bias]
    in_specs = [
        pl.BlockSpec((nb * n, 3 * HY_CH), lambda b: (row0 + b, COL_HY // (3 * HY_CH))),
        pl.BlockSpec((3, 3 * HY_CH), const),
        pl.BlockSpec((1, 3 * HY_CH), const),
        pl.BlockSpec((2 * n, n), const),
        pl.BlockSpec((n, 2 * n), const),
        pl.BlockSpec((n, half), const),
        pl.BlockSpec((n, half), const),
        pl.BlockSpec((HY_ORDER, HY_CH), const),
    ]
    aliases = {}
    if latent:
        aliases = {len(args): 0}
        args.append(prev)
        in_specs.append(_any_spec())

    def body(*refs):
        if latent:
            refs = refs[:len(args) - 1] + refs[len(args):]
        _hyena_kernel(*refs, n=n)

    return pl.pallas_call(
        body, grid=(steps,), in_specs=in_specs,
        out_specs=pl.BlockSpec((nb * n, HY_CH), lambda b: (row0 + b, 0)),
        out_shape=jax.ShapeDtypeStruct((N_TOK, HY_CH), F32),
        input_output_aliases=aliases,
        compiler_params=pltpu.CompilerParams(dimension_semantics=("parallel",)),
        name="hyena_latent" if latent else "hyena_context",
    )(*args)


def _gla_masks():
    t = np.arange(ROWS_GROUP)[:, None]
    s = np.arange(ROWS_GROUP)[None, :]
    same = (t // GLA_CHUNK) == (s // GLA_CHUNK)
    return np.stack([same & (s <= t), same & (s >= t), same]).astype(np.float32)


def _gla_group(r, rows, z, st):
    c = GLA_CHUNK
    hk = H_GLA * DK_GLA
    n_chunks = ROWS_GROUP // c
    logits = _bdot(r['gl'][rows, :], r['gw'][z]) + r['gb'][z:z + 1, :]
    g = (jnp.minimum(logits, 0.0) - jnp.log1p(jnp.exp(-jnp.abs(logits)))) / GLA_NORM
    causal = r['masks'][z]
    b = _split_dot(causal.astype(BF16), g)
    tot = _split_dot(r['masks'][2].astype(BF16), g)
    k = r['k'][rows, :]
    q_t = r['q'][rows, :] * jnp.exp(b) * (DK_GLA ** -0.5)
    k_t = (k * jnp.exp(-b)).astype(BF16)
    k_e = (k * jnp.exp(tot - b)).astype(BF16)
    dec = jnp.exp(tot)
    v = r['v'][rows, :].astype(BF16)

    klane = lax.broadcasted_iota(jnp.int32, (c, hk), 1) // DK_GLA
    vlane = lax.broadcasted_iota(jnp.int32, (c, GROUP_WIDTH), 1) // DV_GLA
    causal_c = jnp.concatenate([causal[0:c, 0:c]] * H_GLA, axis=1) != 0.0
    bd = (lax.broadcasted_iota(jnp.int32, (GROUP_WIDTH, hk), 0) // DV_GLA
          == lax.broadcasted_iota(jnp.int32, (GROUP_WIDTH, hk), 1) // DK_GLA)
    q_bf = q_t.astype(BF16)
    zero_k = jnp.zeros((c, hk), BF16)
    zero_v = jnp.zeros((c, GROUP_WIDTH), BF16)
    o_intra = [None] * n_chunks
    for ci in range(n_chunks):
        cr = slice(ci * c, (ci + 1) * c)
        k_stack = jnp.concatenate([jnp.where(klane == h, k_t[cr], zero_k) for h in range(H_GLA)], axis=0)
        v_diag = jnp.concatenate([jnp.where(vlane == h, v[cr], zero_v) for h in range(H_GLA)], axis=0)
        s = lax.dot_general(q_bf[cr], k_stack, (((1,), (1,)), ((), ())), preferred_element_type=F32)
        p = jnp.where(causal_c, s, 0.0).astype(BF16)
        o_intra[ci] = jnp.dot(p, v_diag, preferred_element_type=F32)

    o = [None] * n_chunks
    for step in range(n_chunks):
        ci = step if z == 0 else n_chunks - 1 - step
        cr = slice(ci * c, (ci + 1) * c)
        o[ci] = o_intra[ci] + _bdot_nt(q_bf[cr], st)
        st = st * dec[ci * c:ci * c + 1, :] + jnp.where(bd, _bdot_tn(v[cr], k_e[cr]), 0.0)
    return jnp.concatenate(o, axis=0), st


def _head_rms(o, g):
    hb = (lax.broadcasted_iota(jnp.int32, (GROUP_WIDTH, GROUP_WIDTH), 0) // DV_GLA
          == lax.broadcasted_iota(jnp.int32, (GROUP_WIDTH, GROUP_WIDTH), 1) // DV_GLA)
    ms = _split_dot_rhs(o * o, hb.astype(BF16)) * (1.0 / DV_GLA)
    return o * lax.rsqrt(ms + EPS) * g


def _split_dot_rhs(x, m):
    hi = x.astype(BF16)
    lo = (x - hi.astype(F32)).astype(BF16)
    return jnp.dot(hi, m, preferred_element_type=F32) + jnp.dot(lo, m, preferred_element_type=F32)


def _expand_state(s):
    hk = H_GLA * DK_GLA
    bd = (lax.broadcasted_iota(jnp.int32, (GROUP_WIDTH, hk), 0) // DV_GLA
          == lax.broadcasted_iota(jnp.int32, (GROUP_WIDTH, hk), 1) // DK_GLA)
    return jnp.where(bd, jnp.concatenate([s] * H_GLA, axis=0), 0.0)


def _compact_state(st):
    out = st[0:DV_GLA]
    for h in range(1, H_GLA):
        out = out + st[h * DV_GLA:(h + 1) * DV_GLA]
    return out


def _gla_ctx_kernel(names, *refs):
    r = dict(zip(names, refs))
    hk = H_GLA * DK_GLA
    for bb in range(NB_CTX):
        rows = slice(bb * SEQ, (bb + 1) * SEQ)
        zero = jnp.zeros((GROUP_WIDTH, hk), F32)
        o_f, st_f = _gla_group(r, rows, 0, zero)
        o_b, st_b = _gla_group(r, rows, 1, zero)
        r['o'][rows, :] = _head_rms(o_f + o_b, r['mg'][...])
        r['sf'][bb, 0] = _compact_state(st_f)
        r['sf'][bb, 1] = _compact_state(st_b)


def _gla_lat_kernel(names, *refs):
    r = dict(zip(names, refs))
    n_groups = DEC_SEQ // ROWS_GROUP

    def run(z):
        r['st'][...] = _expand_state(r['s0'][z])

        def body(step, carry):
            gi = step if z == 0 else n_groups - 1 - step
            rows = pl.ds(pl.multiple_of(gi * ROWS_GROUP, ROWS_GROUP), ROWS_GROUP)
            o, st = _gla_group(r, rows, z, r['st'][...])
            r['st'][...] = st
            if z == 0:
                r['o'][rows, :] = o
            else:
                r['o'][rows, :] = _head_rms(r['o'][rows, :] + o, r['mg'][...])
            return carry

        lax.fori_loop(0, n_groups, body, 0)

    run(0)
    run(1)


def _gla(u, prev, *, latent, gate_w, gate_b, mix_g_d, s0=None):
    n = DEC_SEQ if latent else NB_CTX * SEQ
    steps = DEC_BATCH if latent else BATCH // NB_CTX
    row0 = N_CTX // n if latent else 0
    hk = H_GLA * DK_GLA
    const2 = lambda b: (0, 0)
    names = ['q', 'k', 'v', 'gl', 'gw', 'gb', 'mg', 'masks']
    args = [u, u, u, u, gate_w, gate_b, mix_g_d, _gla_masks()]
    in_specs = [
        pl.BlockSpec((n, hk), lambda b: (row0 + b, COL_QD // hk)),
        pl.BlockSpec((n, hk), lambda b: (row0 + b, COL_KD // hk)),
        pl.BlockSpec((n, GROUP_WIDTH), lambda b: (row0 + b, COL_VD // GROUP_WIDTH)),
        pl.BlockSpec((n, 128), lambda b: (row0 + b, COL_GD // 128)),
        pl.BlockSpec((2, 128, hk), lambda b: (0, 0, 0)),
        pl.BlockSpec((2, hk), const2),
        pl.BlockSpec((1, GROUP_WIDTH), const2),
        pl.BlockSpec((3, ROWS_GROUP, ROWS_GROUP), lambda b: (0, 0, 0)),
    ]
    out_shape = jax.ShapeDtypeStruct((N_TOK, GROUP_WIDTH), F32)
    out_specs = pl.BlockSpec((n, GROUP_WIDTH), lambda b: (row0 + b, 0))
    aliases, scratch = {}, []
    if latent:
        names.append('s0'); args.append(s0)
        in_specs.append(pl.BlockSpec((None, 2, DV_GLA, hk), lambda b: (b, 0, 0, 0)))
        aliases = {len(args): 0}
        names.append('prev'); args.append(prev); in_specs.append(_any_spec())
        names += ['o', 'st']
        scratch = [pltpu.VMEM((GROUP_WIDTH, hk), F32)]
        body = functools.partial(_gla_lat_kernel, tuple(names))
    else:
        names += ['o', 'sf']
        out_shape = (out_shape, jax.ShapeDtypeStruct((BATCH, 2, DV_GLA, hk), F32))
        out_specs = (out_specs, pl.BlockSpec((NB_CTX, 2, DV_GLA, hk), lambda b: (b, 0, 0, 0)))
        body = functools.partial(_gla_ctx_kernel, tuple(names))
    return pl.pallas_call(
        body, grid=(steps,), in_specs=in_specs, out_specs=out_specs, out_shape=out_shape,
        scratch_shapes=scratch, input_output_aliases=aliases,
        compiler_params=pltpu.CompilerParams(dimension_semantics=("parallel",)),
        name="gla_latent" if latent else "gla_context",
    )(*args)


def _merge_kernel(h_ref, oa_ref, ob_ref, oc_ref, od_ref, rd_ref, gt_ref, mg_ref, wo_ref, o_ref):
    gw = GROUP_WIDTH

    def gain(g, order):
        return jnp.concatenate([mg_ref[:, g * gw + h * HEAD_DIM:g * gw + (h + 1) * HEAD_DIM] for h in order], axis=1)

    def weight(g, order):
        return jnp.concatenate([wo_ref[g * gw + h * HEAD_DIM:g * gw + (h + 1) * HEAD_DIM, :] for h in order], axis=0)

    natural = range(N_HEADS)
    ys = [
        _rms(oa_ref[...], gain(0, HEAD_ORDER)),
        _rms(ob_ref[...], gain(1, natural)),
        _rms(oc_ref[...], gain(2, HEAD_ORDER)),
        od_ref[...] * _silu(rd_ref[:, COL_RD - COL_GD:COL_RD - COL_GD + gw]),
    ]
    orders = [HEAD_ORDER, natural, HEAD_ORDER, natural]
    mix = jnp.dot(ys[0].astype(BF16), weight(0, orders[0]), preferred_element_type=F32)
    for j in range(1, 4):
        mix = mix + jnp.dot(ys[j].astype(BF16), weight(j, orders[j]), preferred_element_type=F32)
    o_ref[...] = h_ref[...] + gt_ref[...] * mix


def _merge(h, oa, ob, oc, od, u, mod, mix_g, w_out, layer):
    tm = TM_PROJ
    grp = pl.BlockSpec((tm, GROUP_WIDTH), lambda i: (i, 0))
    return pl.pallas_call(
        _merge_kernel,
        grid=(N_TOK // tm,),
        in_specs=[
            pl.BlockSpec((tm, D_MODEL), lambda i: (i, 0)),
            grp, grp, grp, grp,
            pl.BlockSpec((tm, COL_RD_BLOCK), lambda i: (i, COL_GD // COL_RD_BLOCK)),
            _mod_spec(tm, 5, 1),
            pl.BlockSpec((None, 1, D_MODEL), lambda i: (layer, 0, 0)),
            pl.BlockSpec((None, D_MODEL, D_MODEL), lambda i: (layer, 0, 0)),
        ],
        out_specs=pl.BlockSpec((tm, D_MODEL), lambda i: (i, 0)),
        out_shape=jax.ShapeDtypeStruct((N_TOK, D_MODEL), F32),
        compiler_params=pltpu.CompilerParams(dimension_semantics=("parallel",)),
        name="merge",
    )(h, oa, ob, oc, od, u, mod, mix_g, w_out)


def _gate_weights(gate_w):
    out = jnp.zeros((2, 128, H_GLA * DK_GLA), gate_w.dtype)
    out = out.at[0, 0:GLA_RANK].set(gate_w[0])
    return out.at[1, GLA_RANK:2 * GLA_RANK].set(gate_w[1])


def _states_to_kernel(st):
    b = st.shape[0]
    return st.transpose(0, 1, 4, 2, 3).reshape(b, 2, DV_GLA, H_GLA * DK_GLA)


def _states_from_kernel(st):
    b = st.shape[0]
    return st.reshape(b, 2, DV_GLA, H_GLA, DK_GLA).transpose(0, 1, 3, 4, 2)


def kernel(x_prompt, x_sample, cache_swa_k, cache_swa_v, cache_gqa_k, cache_gqa_v, state_gla, c, c_ctx, w_mod, b_mod, norm_g, ffn_w_in, ffn_w_out, w_in, w_out, mix_g, swa_sink, qk_norm_g, hy_conv_w, hy_conv_b, hy_w1, hy_b1, hy_w2, hy_b2, hy_w3, hy_freq, hy_bias, gla_gate_w, gla_gate_b, final_g):
    cond = jnp.zeros((MOD_ROWS, D_MODEL), F32).at[0].set(c_ctx).at[1:1 + DEC_BATCH].set(c)
    mod_all = _modulation(cond, w_mod, b_mod).reshape(DEPTH, MOD_ROWS, N_MOD, 1, D_MODEL)

    dft = {n: tuple(jnp.asarray(m).astype(BF16) for m in _dft_matrices(n)) for n in (SEQ, DEC_SEQ)}

    xs = [x_prompt.reshape(N_CTX, D_MODEL), x_sample.reshape(N_LAT, D_MODEL)]
    ffn_in = ffn_w_in.astype(BF16)
    ffn_out = ffn_w_out.astype(BF16)
    ng = norm_g.reshape(DEPTH, 3, 1, D_MODEL)
    proj_w = w_in.astype(BF16)
    out_w = w_out.astype(BF16)
    mix_gp = mix_g.reshape(DEPTH, 1, D_MODEL)
    cache_a = cache_c = None
    sts = []
    for l in range(DEPTH):
        mod = mod_all[l]
        h = _ffn(xs, mod, 0, ng, ffn_in, ffn_out, l, 0)
        u = _in_proj(h, mod, ng, proj_w, l)

        ck_a = cache_swa_k[:, l].reshape(DEC_BATCH, PAST_LEN, KV_WIDTH)
        cv_a = cache_swa_v[:, l].reshape(DEC_BATCH, PAST_LEN, KV_WIDTH)
        ck_c = cache_gqa_k[:, l].reshape(DEC_BATCH, PAST_LEN, KV_WIDTH)
        cv_c = cache_gqa_v[:, l].reshape(DEC_BATCH, PAST_LEN, KV_WIDTH)
        qkg2 = jnp.tile(qk_norm_g[l], (1, N_KV))

        oa, *cache_a = _attention_context(u, l, cache_a, col_q=COL_QA, col_k=COL_KA, col_v=COL_VA,
                                          sink=swa_sink[l])
        oa = _attention_latent(u, oa, col_q=COL_QA, col_k=COL_KA, col_v=COL_VA, cache_k=ck_a, cache_v=cv_a,
                               window=True, sink=swa_sink[l])
        oc, *cache_c = _attention_context(u, l, cache_c, col_q=COL_QC, col_k=COL_KC, col_v=COL_VC, qkg2=qkg2)
        oc = _attention_latent(u, oc, col_q=COL_QC, col_k=COL_KC, col_v=COL_VC, cache_k=ck_c, cache_v=cv_c,
                               qkg2=qkg2)

        w1p = jnp.zeros((128, HY_FH), F32).at[:HY_EMB].set(hy_w1[l])
        ob = None
        for latent, n in ((False, SEQ), (True, DEC_SEQ)):
            fwd, inv = dft[n]
            kr, ki = _hyena_filters(n, fwd, w1p, hy_b1[l], hy_w2[l], hy_b2[l], hy_w3[l], hy_freq[l])
            ob = _hyena(u, ob, latent=latent, fwd=fwd, inv=inv, kr=kr, ki=ki, conv_w=hy_conv_w[l],
                        conv_b=hy_conv_b[l], bias=hy_bias[l])

        gw = _gate_weights(gla_gate_w[l]).astype(BF16)
        mg_d = mix_g[l, 3 * GROUP_WIDTH:].reshape(1, GROUP_WIDTH)
        od, st = _gla(u, None, latent=False, gate_w=gw, gate_b=gla_gate_b[l], mix_g_d=mg_d)
        od = _gla(u, od, latent=True, gate_w=gw, gate_b=gla_gate_b[l], mix_g_d=mg_d,
                  s0=_states_to_kernel(state_gla[:, l]))

        h = _merge(h, oa, ob, oc, od, u, mod, mix_gp, out_w, l)
        xs = [_ffn([h], mod, 6, ng, ffn_in, ffn_out, l, 1)] if l + 1 < DEPTH else None
        if xs is None:
            y_prompt, y_sample = _ffn([h], mod, 6, ng, ffn_in, ffn_out, l, 1, final_g=final_g)
        sts.append(_states_from_kernel(st))

    y_prompt = y_prompt.reshape(BATCH, SEQ, D_MODEL)
    y_sample = y_sample.reshape(DEC_BATCH, DEC_SEQ, D_MODEL)
    caches = [x.reshape(BATCH, DEPTH, SEQ, N_KV, HEAD_DIM) for x in (*cache_a, *cache_c)]
    return (y_prompt, y_sample, *caches, jnp.stack(sts, axis=1))
```

```python
import functools
import math

import numpy as np
import jax
import jax.numpy as jnp
from jax import lax
from jax.experimental import pallas as pl
from jax.experimental.pallas import tpu as pltpu

F32 = jnp.float32
BF16 = jnp.bfloat16

D_MODEL = 1024
BATCH = 32
SEQ = 256
DEPTH = 2
DEC_BATCH = 2
DEC_SEQ = 1024
PAST_LEN = 512
GRID_W = 64
HEAD_DIM = 64
GROUP_WIDTH = 256
N_HEADS = 4
N_KV = 2
KV_WIDTH = N_KV * HEAD_DIM
WINDOW = 128
HY_CH = 256
HY_ORDER = 2
HY_BANDS = 16
HY_EMB = 2 * HY_BANDS + 1
HY_FH = 64
HY_TARGET = 1e-2
HY_FAST = 0.3
HY_SLOW = 1.5
H_GLA = 4
DV_GLA = 64
DK_GLA = 32
GLA_RANK = 16
GLA_CHUNK = 64
GLA_NORM = 16.0
D_FF = 2816
N_MOD = 9
ROPE_THETA = 10000.0
EPS = 1e-6

N_CTX = BATCH * SEQ
N_LAT = DEC_BATCH * DEC_SEQ
N_TOK = N_CTX + N_LAT
MOD_ROWS = 8

COL_HY, COL_QA, COL_KA, COL_VA = 0, 768, 1024, 1152
COL_QC, COL_KC, COL_VC = 1280, 1536, 1664
COL_QD, COL_KD, COL_VD, COL_GD, COL_RD = 1792, 1920, 2048, 2304, 2336
PROJ_WIDTH = 2688
REF_QA, REF_KA, REF_HY, REF_QC, REF_KC, REF_END = 0, 256, 512, 1280, 1536, 2592
COL_RD_BLOCK = 384
HEAD_ORDER = (0, 2, 1, 3)
_HEAD_PERM = np.concatenate([np.arange(HEAD_DIM) + HEAD_DIM * h for h in HEAD_ORDER])

TM_FFN = 1024
TK_FFN = 256
FFN_VMEM_BYTES = 60 * 1024 * 1024
TM_PROJ = 1024
TM_IN_PROJ = 512
TQ_ATTN = 256
ROWS_GROUP = 256
NB_CTX = 2
NB_HYENA = 4


def _bdot(a, b):
    return jnp.dot(a.astype(BF16), b.astype(BF16), preferred_element_type=F32)


def _bdot_nt(a, b):
    return lax.dot_general(a.astype(BF16), b.astype(BF16), (((1,), (1,)), ((), ())),
                           preferred_element_type=F32)


def _bdot_tn(a, b):
    return lax.dot_general(a.astype(BF16), b.astype(BF16), (((0,), (0,)), ((), ())),
                           preferred_element_type=F32)


def _fdot(a, b):
    a_hi, b_hi = a.astype(BF16), b.astype(BF16)
    a_lo = (a - a_hi.astype(F32)).astype(BF16)
    b_lo = (b - b_hi.astype(F32)).astype(BF16)
    dot = functools.partial(jnp.dot, preferred_element_type=F32)
    return dot(a_hi, b_hi) + (dot(a_hi, b_lo) + dot(a_lo, b_hi) + dot(a_lo, b_lo))


def _split_dot(m, x):
    hi = x.astype(BF16)
    lo = (x - hi.astype(F32)).astype(BF16)
    return jnp.dot(m, hi, preferred_element_type=F32) + jnp.dot(m, lo, preferred_element_type=F32)


def _rms(x, g):
    return x * lax.rsqrt(jnp.mean(x * x, axis=-1, keepdims=True) + EPS) * g


def _silu(x):
    return x * jax.nn.sigmoid(x)


def _mod_row(i, tm):
    n_ctx_tiles = N_CTX // tm
    per_batch = DEC_SEQ // tm
    return jnp.where(i < n_ctx_tiles, 0, 1 + (i - n_ctx_tiles) // per_batch)


def _mod_spec(tm, j, grid_rank):
    if grid_rank == 1:
        return pl.BlockSpec((None, None, 1, D_MODEL), lambda i: (_mod_row(i, tm), j, 0, 0))
    return pl.BlockSpec((None, None, 1, D_MODEL), lambda i, k: (_mod_row(i, tm), j, 0, 0))


def _any_spec():
    return pl.BlockSpec(memory_space=pl.ANY)


def _mod_kernel(cond_ref, w_ref, b_ref, o_ref):
    o_ref[...] = _bdot(_silu(cond_ref[...]), w_ref[...]) + b_ref[...]


def _modulation(cond, w_mod, b_mod):
    tn = D_MODEL
    return pl.pallas_call(
        _mod_kernel,
        grid=(DEPTH, N_MOD * D_MODEL // tn),
        in_specs=[
            pl.BlockSpec((MOD_ROWS, D_MODEL), lambda l, j: (0, 0)),
            pl.BlockSpec((None, D_MODEL, tn), lambda l, j: (l, 0, j)),
            pl.BlockSpec((None, 1, tn), lambda l, j: (l, 0, j)),
        ],
        out_specs=pl.BlockSpec((None, MOD_ROWS, tn), lambda l, j: (l, 0, j)),
        out_shape=jax.ShapeDtypeStruct((DEPTH, MOD_ROWS, N_MOD * D_MODEL), F32),
        name="modulation",
    )(cond, w_mod, b_mod.reshape(DEPTH, 1, N_MOD * D_MODEL))


def _ffn_kernel(split_input, final, *refs):
    is_ctx = pl.program_id(0) < N_CTX // TM_FFN
    if split_input:
        xa_ref, xb_ref, *refs = refs
        x = jnp.where(is_ctx, xa_ref[...], xb_ref[...])
    else:
        x_ref, *refs = refs
        x = x_ref[...]
    if final:
        sh_ref, sc_ref, gt_ref, ng_ref, wi_ref, wo_ref, fg_ref, yc_ref, yl_ref, xm_ref, act_ref = refs
    else:
        sh_ref, sc_ref, gt_ref, ng_ref, wi_ref, wo_ref, o_ref, xm_ref, act_ref = refs
    xn = _rms(x, ng_ref[...])
    xm_ref[...] = (xn * (1.0 + sc_ref[...]) + sh_ref[...]).astype(BF16)
    for c0 in range(0, D_FF, TK_FFN):
        c1 = min(c0 + TK_FFN, D_FF)
        xm = xm_ref[...]
        a = jnp.dot(xm, wi_ref[:, c0:c1], preferred_element_type=F32)
        b = jnp.dot(xm, wi_ref[:, D_FF + c0:D_FF + c1], preferred_element_type=F32)
        act_ref[:, c0:c1] = (_silu(a) * b).astype(BF16)
    y = jnp.dot(act_ref[...], wo_ref[...], preferred_element_type=F32)
    out = x + 0.5 * gt_ref[...] * y
    if not final:
        o_ref[...] = out
        return
    out = _rms(out, fg_ref[...])

    @pl.when(is_ctx)
    def _():
        yc_ref[...] = out

    @pl.when(jnp.logical_not(is_ctx))
    def _():
        yl_ref[...] = out


def _ffn(xs, mod, j0, ng, w_in, w_out, layer, which, final_g=None):
    tm = TM_FFN
    n_ctx_tiles = N_CTX // tm
    split_specs = [pl.BlockSpec((tm, D_MODEL), lambda i: (jnp.minimum(i, n_ctx_tiles - 1), 0)),
                   pl.BlockSpec((tm, D_MODEL), lambda i: (jnp.maximum(i - n_ctx_tiles, 0), 0))]
    whole_spec = pl.BlockSpec((tm, D_MODEL), lambda i: (i, 0))
    x_specs = split_specs if len(xs) == 2 else [whole_spec]
    resident = dict(pipeline_mode=pl.Buffered(1))
    final = final_g is not None
    extra_args, extra_specs = [], []
    if final:
        extra_args = [final_g.reshape(1, D_MODEL)]
        extra_specs = [pl.BlockSpec((1, D_MODEL), lambda i: (0, 0))]
        out_specs = tuple(split_specs)
        out_shape = (jax.ShapeDtypeStruct((N_CTX, D_MODEL), F32), jax.ShapeDtypeStruct((N_LAT, D_MODEL), F32))
    else:
        out_specs = whole_spec
        out_shape = jax.ShapeDtypeStruct((N_TOK, D_MODEL), F32)
    return pl.pallas_call(
        functools.partial(_ffn_kernel, len(xs) == 2, final),
        grid=(N_TOK // tm,),
        in_specs=x_specs + [
            _mod_spec(tm, j0, 1), _mod_spec(tm, j0 + 1, 1), _mod_spec(tm, j0 + 2, 1),
            pl.BlockSpec((None, None, 1, D_MODEL), lambda i: (layer, j0 // 3, 0, 0)),
            pl.BlockSpec((None, None, D_MODEL, 2 * D_FF), lambda i: (layer, which, 0, 0), **resident),
            pl.BlockSpec((None, None, D_FF, D_MODEL), lambda i: (layer, which, 0, 0), **resident),
        ] + extra_specs,
        out_specs=out_specs,
        out_shape=out_shape,
        scratch_shapes=[pltpu.VMEM((tm, D_MODEL), BF16), pltpu.VMEM((tm, D_FF), BF16)],
        compiler_params=pltpu.CompilerParams(dimension_semantics=("arbitrary",),
                                             vmem_limit_bytes=FFN_VMEM_BYTES),
        name="ffn",
    )(*xs, mod, mod, mod, ng, w_in, w_out, *extra_args)


def _proj_kernel(x_ref, sh_ref, sc_ref, ng_ref, wf_ref, o_ref, w_ref):
    @pl.when(pl.program_id(0) == 0)
    def _():
        w_ref[...] = wf_ref[...].astype(BF16)

    xn = _rms(x_ref[...], ng_ref[...])
    xm = (xn * (1.0 + sc_ref[...]) + sh_ref[...]).astype(BF16)

    def put(col, w):
        o_ref[:, col:col + w.shape[1]] = jnp.dot(xm, w, preferred_element_type=F32)

    def queries(base):
        return jnp.concatenate([w_ref[:, base + HEAD_DIM * h:base + HEAD_DIM * (h + 1)] for h in HEAD_ORDER], axis=1)

    put(COL_HY, w_ref[:, REF_HY:REF_QC])
    put(COL_QA, queries(REF_QA))
    put(COL_KA, w_ref[:, REF_KA:REF_HY])
    put(COL_QC, queries(REF_QC))
    put(COL_KC, w_ref[:, REF_KC:COL_GD])
    put(COL_GD, w_ref[:, COL_GD:REF_END])
    o_ref[:, REF_END:PROJ_WIDTH] = jnp.zeros((x_ref.shape[0], PROJ_WIDTH - REF_END), F32)


def _in_proj(h, mod, ng, w, layer):
    tm = TM_IN_PROJ
    return pl.pallas_call(
        _proj_kernel,
        grid=(N_TOK // tm,),
        in_specs=[
            pl.BlockSpec((tm, D_MODEL), lambda i: (i, 0)),
            _mod_spec(tm, 3, 1), _mod_spec(tm, 4, 1),
            pl.BlockSpec((None, None, 1, D_MODEL), lambda i: (layer, 1, 0, 0)),
            pl.BlockSpec((None, D_MODEL, REF_END), lambda i: (layer, 0, 0), pipeline_mode=pl.Buffered(1)),
        ],
        out_specs=pl.BlockSpec((tm, PROJ_WIDTH), lambda i: (i, 0)),
        out_shape=jax.ShapeDtypeStruct((N_TOK, PROJ_WIDTH), F32),
        scratch_shapes=[pltpu.VMEM((D_MODEL, REF_END), BF16)],
        compiler_params=pltpu.CompilerParams(dimension_semantics=("arbitrary",)),
        name="in_proj",
    )(h, mod, mod, ng, w)


def _rope_tables(length):
    rows = length // GRID_W
    r = np.repeat(np.arange(rows, dtype=np.float32), GRID_W)
    col = np.tile(np.arange(GRID_W, dtype=np.float32), rows)
    nf = HEAD_DIM // 4
    inv = (np.float32(ROPE_THETA) ** (-np.arange(nf, dtype=np.float32) / nf)).astype(np.float32)
    ang = np.concatenate([r[:, None] * inv, col[:, None] * inv], axis=-1).astype(np.float32)
    cos, sin = np.cos(ang).astype(np.float32), np.sin(ang).astype(np.float32)
    return np.tile(cos, (1, 4)), np.tile(np.concatenate([-sin, sin], axis=-1), (1, 2))


def _pair_lanes(rows):
    lane = lax.broadcasted_iota(jnp.int32, (rows, 2 * HEAD_DIM), 1)
    return lane < HEAD_DIM, (lane % HEAD_DIM) < HEAD_DIM // 2


def _rope_pair(x, cos, sin_signed, first_half):
    partner = jnp.where(first_half, pltpu.roll(x, 3 * HEAD_DIM // 2, 1), pltpu.roll(x, HEAD_DIM // 2, 1))
    return x * cos + partner * sin_signed


def _rms_pair(x, g, lo):
    x2 = x * x
    s_lo = jnp.sum(jnp.where(lo, x2, 0.0), axis=-1, keepdims=True)
    s_hi = jnp.sum(jnp.where(lo, 0.0, x2), axis=-1, keepdims=True)
    ms = jnp.where(lo, s_lo, s_hi) * (1.0 / HEAD_DIM)
    return x * lax.rsqrt(ms + EPS) * g


def _attn_core(q, k_bf, v_bf, lo, *, mask=None, ctx=None, sinks=None):
    tq = q.shape[0]
    pair = 2 * HEAD_DIM
    scale = HEAD_DIM ** -0.5
    qa, qb = q[:, :pair] * scale, q[:, pair:] * scale
    qs = jnp.concatenate([jnp.where(lo, qa, 0.0), jnp.where(lo, 0.0, qa),
                          jnp.where(lo, qb, 0.0), jnp.where(lo, 0.0, qb)], axis=0).astype(BF16)
    s = _bdot_nt(qs, k_bf)
    if ctx is not None:
        s_c = _bdot_nt(qs, ctx[0])
    ps, pcs, dens = [], [], []
    for j in range(N_HEADS):
        rows = slice(j * tq, (j + 1) * tq)
        sj = s[rows]
        if mask is not None:
            sj = jnp.where(mask, sj, -1e30)
        m = jnp.max(sj, axis=-1, keepdims=True)
        if ctx is not None:
            m = jnp.maximum(m, jnp.max(s_c[rows], axis=-1, keepdims=True))
        if sinks is not None:
            m = jnp.maximum(m, sinks[j])
        p = jnp.exp(sj - m)
        den = jnp.sum(p, axis=-1, keepdims=True)
        ps.append(p.astype(BF16))
        if ctx is not None:
            pc = jnp.exp(s_c[rows] - m)
            den = den + jnp.sum(pc, axis=-1, keepdims=True)
            pcs.append(pc.astype(BF16))
        if sinks is not None:
            den = den + jnp.exp(sinks[j] - m)
        dens.append(den)
    r = jnp.dot(jnp.concatenate(ps, axis=0), v_bf, preferred_element_type=F32)
    if ctx is not None:
        r = r + jnp.dot(jnp.concatenate(pcs, axis=0), ctx[1], preferred_element_type=F32)
    o = [r[j * tq:(j + 1) * tq] / dens[j] for j in range(N_HEADS)]
    return jnp.concatenate([jnp.where(lo, o[0], o[1]), jnp.where(lo, o[2], o[3])], axis=1)


def _attn_ctx_kernel(names, *refs):
    r = dict(zip(names, refs))
    lo, _ = _pair_lanes(SEQ)
    sinks = [r['sink'][h] for h in HEAD_ORDER] if 'sink' in r else None
    for bb in range(NB_CTX):
        rows = slice(bb * SEQ, (bb + 1) * SEQ)
        q, k, v = r['q'][rows, :], r['k'][rows, :], r['v'][rows, :]
        if 'qkg' in r:
            gq, gk = r['qkg'][0:1, :], r['qkg'][1:2, :]
            q = jnp.concatenate([_rms_pair(q[:, :KV_WIDTH], gq, lo), _rms_pair(q[:, KV_WIDTH:], gq, lo)], axis=1)
            k = _rms_pair(k, gk, lo)
        r['k_out'][bb] = k
        r['v_out'][bb] = v
        r['o'][rows, :] = _attn_core(q, k.astype(BF16), v.astype(BF16), lo, sinks=sinks)


def _attn_lat_kernel(names, window, *refs):
    r = dict(zip(names, refs))
    tq = TQ_ATTN
    i = pl.program_id(1)
    lo, first_half = _pair_lanes(tq)
    qk_norm = 'qkg' in r

    @pl.when(i == 0)
    def _():
        lo_k, first_half_k = _pair_lanes(DEC_SEQ)
        k = r['k'][...]
        if qk_norm:
            k = _rms_pair(k, r['qkg'][1:2, :], lo_k)
        r['k_scr'][...] = _rope_pair(k, r['cos_k'][...], r['sin_k'][...], first_half_k).astype(BF16)

    q = r['q'][...]
    halves = []
    for c0 in (0, KV_WIDTH):
        x = q[:, c0:c0 + KV_WIDTH]
        if qk_norm:
            x = _rms_pair(x, r['qkg'][0:1, :], lo)
        halves.append(_rope_pair(x, r['cos_q'][...], r['sin_q'][...], first_half))
    q = jnp.concatenate(halves, axis=1)
    sinks = [r['sink'][h] for h in HEAD_ORDER] if 'sink' in r else None
    ctx = (r['kc'][...].astype(BF16), r['vc'][...].astype(BF16))
    if window:
        span = tq + 2 * WINDOW
        start = pl.multiple_of(jnp.clip(i * tq - WINDOW, 0, DEC_SEQ - span), WINDOW)
        qpos = i * tq + lax.broadcasted_iota(jnp.int32, (tq, span), 0)
        kpos = start + lax.broadcasted_iota(jnp.int32, (tq, span), 1)
        mask = jnp.abs(qpos - kpos) <= WINDOW
        k_bf = r['k_scr'][pl.ds(start, span), :]
        v_bf = r['v'][pl.ds(start, span), :].astype(BF16)
    else:
        mask = None
        k_bf = r['k_scr'][...]
        v_bf = r['v'][...].astype(BF16)
    r['o'][...] = _attn_core(q, k_bf, v_bf, lo, mask=mask, ctx=ctx, sinks=sinks)


def _attention_context(u, layer, cache_prev, *, col_q, col_k, col_v, sink=None, qkg2=None):
    rows = NB_CTX * SEQ
    qb, kb, vb = col_q // GROUP_WIDTH, col_k // KV_WIDTH, col_v // KV_WIDTH
    names = ['q', 'k', 'v']
    args = [u, u, u]
    in_specs = [pl.BlockSpec((rows, GROUP_WIDTH), lambda b: (b, qb)),
                pl.BlockSpec((rows, KV_WIDTH), lambda b: (b, kb)),
                pl.BlockSpec((rows, KV_WIDTH), lambda b: (b, vb))]
    if sink is not None:
        names.append('sink'); args.append(sink)
        in_specs.append(pl.BlockSpec(memory_space=pltpu.SMEM))
    if qkg2 is not None:
        names.append('qkg'); args.append(qkg2)
        in_specs.append(pl.BlockSpec((2, KV_WIDTH), lambda b: (0, 0)))
    aliases = {}
    if cache_prev is not None:
        for j, prev in enumerate(cache_prev):
            aliases[len(args)] = 1 + j
            names.append(f'prev{j}'); args.append(prev); in_specs.append(_any_spec())
    names += ['o', 'k_out', 'v_out']
    cache_shape = jax.ShapeDtypeStruct((BATCH, DEPTH, SEQ, KV_WIDTH), F32)
    cache_spec = pl.BlockSpec((NB_CTX, None, SEQ, KV_WIDTH), lambda b: (b, layer, 0, 0))
    return pl.pallas_call(
        functools.partial(_attn_ctx_kernel, tuple(names)),
        grid=(BATCH // NB_CTX,), in_specs=in_specs,
        out_specs=(pl.BlockSpec((rows, GROUP_WIDTH), lambda b: (b, 0)), cache_spec, cache_spec),
        out_shape=(jax.ShapeDtypeStruct((N_TOK, GROUP_WIDTH), F32), cache_shape, cache_shape),
        input_output_aliases=aliases,
        compiler_params=pltpu.CompilerParams(dimension_semantics=("parallel",)),
        name="attn_context",
    )(*args)


def _attention_latent(u, prev, layer, *, col_q, col_k, col_v, cache_k, cache_v, window=False, sink=None,
                      qkg2=None):
    tq = TQ_ATTN
    nq = DEC_SEQ // tq
    row0_q, row0_k = N_CTX // tq, N_CTX // DEC_SEQ
    qb, kb, vb = col_q // GROUP_WIDTH, col_k // KV_WIDTH, col_v // KV_WIDTH
    cos, sin = _rope_tables(DEC_SEQ)
    const = lambda b, i: (0, 0)
    names = ['q', 'k', 'v', 'kc', 'vc', 'cos_q', 'sin_q', 'cos_k', 'sin_k']
    args = [u, u, u, cache_k, cache_v, cos, sin, cos, sin]
    in_specs = [
        pl.BlockSpec((tq, GROUP_WIDTH), lambda b, i: (row0_q + b * nq + i, qb)),
        pl.BlockSpec((DEC_SEQ, KV_WIDTH), lambda b, i: (row0_k + b, kb)),
        pl.BlockSpec((DEC_SEQ, KV_WIDTH), lambda b, i: (row0_k + b, vb)),
        pl.BlockSpec((None, None, PAST_LEN, KV_WIDTH), lambda b, i: (b, layer, 0, 0)),
        pl.BlockSpec((None, None, PAST_LEN, KV_WIDTH), lambda b, i: (b, layer, 0, 0)),
        pl.BlockSpec((tq, KV_WIDTH), lambda b, i: (i, 0)),
        pl.BlockSpec((tq, KV_WIDTH), lambda b, i: (i, 0)),
        pl.BlockSpec((DEC_SEQ, KV_WIDTH), const),
        pl.BlockSpec((DEC_SEQ, KV_WIDTH), const),
    ]
    if sink is not None:
        names.append('sink'); args.append(sink)
        in_specs.append(pl.BlockSpec(memory_space=pltpu.SMEM))
    if qkg2 is not None:
        names.append('qkg'); args.append(qkg2)
        in_specs.append(pl.BlockSpec((2, KV_WIDTH), const))
    aliases = {len(args): 0}
    names.append('prev'); args.append(prev); in_specs.append(_any_spec())
    names += ['o', 'k_scr']
    return pl.pallas_call(
        functools.partial(_attn_lat_kernel, tuple(names), window),
        grid=(DEC_BATCH, nq), in_specs=in_specs,
        out_specs=pl.BlockSpec((tq, GROUP_WIDTH), lambda b, i: (row0_q + b * nq + i, 0)),
        out_shape=jax.ShapeDtypeStruct((N_TOK, GROUP_WIDTH), F32),
        scratch_shapes=[pltpu.VMEM((DEC_SEQ, KV_WIDTH), BF16)],
        input_output_aliases=aliases,
        compiler_params=pltpu.CompilerParams(dimension_semantics=("parallel", "arbitrary")),
        name="attn_latent",
    )(*args)


def _dft_matrices(length):
    n = length
    k = np.arange(n, dtype=np.int64)[:, None]
    s = np.arange(n, dtype=np.int64)[None, :]
    ang = np.pi * ((k * s) % (2 * n)).astype(np.float64) / n
    fwd_cos = np.cos(ang)
    fwd_sin = -np.sin(ang)
    fwd_sin[0, :] = 1.0 - 2.0 * (np.arange(n) % 2)
    fwd = np.concatenate([fwd_cos, fwd_sin], axis=0)
    wk = np.full((n,), 2.0)
    wk[0] = 1.0
    inv_cos = (np.cos(ang) * wk[:, None]).T / (2 * n)
    inv_sin = (-2.0 * np.sin(ang)).T / (2 * n)
    inv_sin[:, 0] = (1.0 - 2.0 * (np.arange(n) % 2)) / (2 * n)
    inv = np.concatenate([inv_cos, inv_sin], axis=1)
    return fwd.astype(np.float32), inv.astype(np.float32)


def _filter_features(length):
    t = np.linspace(0.0, 1.0, length, dtype=np.float32)[:, None]
    w = (np.float32(2.0 * math.pi / length) * np.arange(length, dtype=np.float32))[:, None]
    bands = np.linspace(1e-4, HY_BANDS - 1, HY_BANDS, dtype=np.float32)[None, :]
    z = np.concatenate([t, np.cos(bands * w), -np.sin(bands * w)], axis=-1).astype(np.float32)
    zp = np.zeros((length, 128), np.float32)
    zp[:, :HY_EMB] = z
    deltas = np.linspace(math.log(HY_TARGET) / HY_SLOW, math.log(HY_TARGET) / HY_FAST, HY_CH, dtype=np.float32)
    decay = np.exp(-t * np.abs(deltas)).astype(np.float32)
    return zp, decay


def _filter_kernel(z_ref, w1_ref, b1_ref, w2_ref, b2_ref, w3_ref, fr_ref, dec_ref, f_ref, kr_ref, ki_ref):
    n = z_ref.shape[0]
    h = jnp.sin(fr_ref[0:1, :] * (_fdot(z_ref[...], w1_ref[...]) + b1_ref[...]))
    h = jnp.sin(fr_ref[1:2, :] * (_fdot(h, w2_ref[...]) + b2_ref[...]))
    h = _fdot(h, w3_ref[...])
    dec = dec_ref[...]
    half = HY_ORDER * HY_CH
    pos = jnp.concatenate([h[:, j * HY_CH:(j + 1) * HY_CH] * dec for j in range(HY_ORDER)], axis=-1)
    neg = jnp.concatenate([h[:, half + j * HY_CH:half + (j + 1) * HY_CH] * dec for j in range(HY_ORDER)], axis=-1)
    row = lax.broadcasted_iota(jnp.int32, (n, half), 0)
    neg = jnp.where(row == 0, 0.0, neg)
    fwd = f_ref[...]
    a = jnp.dot(fwd, (pos + neg).astype(BF16), preferred_element_type=F32)
    b = jnp.dot(fwd, (pos - neg).astype(BF16), preferred_element_type=F32)
    kr_ref[...] = a[:n]
    ki_ref[...] = jnp.where(row == 0, a[n:], b[n:])


def _hyena_filters(length, fwd, w1p, b1, w2, b2, w3, freq):
    zp, decay = _filter_features(length)
    half = HY_ORDER * HY_CH
    return pl.pallas_call(
        _filter_kernel,
        out_shape=(jax.ShapeDtypeStruct((length, half), F32), jax.ShapeDtypeStruct((length, half), F32)),
        name="hyena_filters",
    )(zp, w1p, b1.reshape(1, HY_FH), w2, b2.reshape(1, HY_FH), w3, freq, decay, fwd)


def _hyena_kernel(z_ref, cw_ref, cb_ref, f_ref, g_ref, kr_ref, ki_ref, bias_ref, o_ref, *, n):
    nb = z_ref.shape[0] // n
    row = lax.broadcasted_iota(jnp.int32, (n, 3 * HY_CH), 0)
    row0 = lax.broadcasted_iota(jnp.int32, (n, HY_CH), 0) == 0
    vs, x1s, x2s = [], [], []
    for bb in range(nb):
        z = z_ref[bb * n:(bb + 1) * n, :]
        prev = jnp.where(row == 0, 0.0, pltpu.roll(z, 1, 0))
        nxt = jnp.where(row == n - 1, 0.0, pltpu.roll(z, n - 1, 0))
        z = prev * cw_ref[0:1, :] + z * cw_ref[1:2, :] + nxt * cw_ref[2:3, :] + cb_ref[...]
        vs.append(z[:, :HY_CH])
        x1s.append(z[:, HY_CH:2 * HY_CH])
        x2s.append(z[:, 2 * HY_CH:])

    def long_conv(xs, order):
        cols = slice(order * HY_CH, (order + 1) * HY_CH)
        kr, ki = kr_ref[:, cols], ki_ref[:, cols]
        spec = jnp.dot(f_ref[...], jnp.concatenate([x.astype(BF16) for x in xs], axis=1),
                       preferred_element_type=F32)
        prods = []
        for bb in range(nb):
            ur, ui = spec[:n, bb * HY_CH:(bb + 1) * HY_CH], spec[n:, bb * HY_CH:(bb + 1) * HY_CH]
            yr = ur * kr - jnp.where(row0, 0.0, ui * ki)
            yi = jnp.where(row0, ui * ki, ur * ki + ui * kr)
            prods.append(jnp.concatenate([yr, yi], axis=0).astype(BF16))
        y = jnp.dot(g_ref[...], jnp.concatenate(prods, axis=1), preferred_element_type=F32)
        return [y[:, bb * HY_CH:(bb + 1) * HY_CH] + xs[bb] * bias_ref[order:order + 1, :] for bb in range(nb)]

    ys = long_conv(vs, 0)
    ys = long_conv([x1s[bb] * ys[bb] for bb in range(nb)], 1)
    for bb in range(nb):
        o_ref[bb * n:(bb + 1) * n, :] = x2s[bb] * ys[bb]


def _hyena(u, prev, *, latent, fwd, inv, kr, ki, conv_w, conv_b, bias):
    n = DEC_SEQ if latent else SEQ
    nb = DEC_BATCH if latent else NB_HYENA
    steps = 1 if latent else BATCH // nb
    row0 = N_CTX // (nb * n) if latent else 0
    half = HY_ORDER * HY_CH
    const = lambda b: (0, 0)
    args = [u, conv_w, conv_b.reshape(1, 3 * HY_CH), fwd, inv, kr, ki, bias]
    in_specs = [
        pl.BlockSpec((nb * n, 3 * HY_CH), lambda b: (row0 + b, COL_HY // (3 * HY_CH))),
        pl.BlockSpec((3, 3 * HY_CH), const),
        pl.BlockSpec((1, 3 * HY_CH), const),
        pl.BlockSpec((2 * n, n), const),
        pl.BlockSpec((n, 2 * n), const),
        pl.BlockSpec((n, half), const),
        pl.BlockSpec((n, half), const),
        pl.BlockSpec((HY_ORDER, HY_CH), const),
    ]
    aliases = {}
    if latent:
        aliases = {len(args): 0}
        args.append(prev)
        in_specs.append(_any_spec())

    def body(*refs):
        if latent:
            refs = refs[:len(args) - 1] + refs[len(args):]
        _hyena_kernel(*refs, n=n)

    return pl.pallas_call(
        body, grid=(steps,), in_specs=in_specs,
        out_specs=pl.BlockSpec((nb * n, HY_CH), lambda b: (row0 + b, 0)),
        out_shape=jax.ShapeDtypeStruct((N_TOK, HY_CH), F32),
        input_output_aliases=aliases,
        compiler_params=pltpu.CompilerParams(dimension_semantics=("parallel",)),
        name="hyena_latent" if latent else "hyena_context",
    )(*args)


def _gla_masks():
    t = np.arange(ROWS_GROUP)[:, None]
    s = np.arange(ROWS_GROUP)[None, :]
    same = (t // GLA_CHUNK) == (s // GLA_CHUNK)
    return np.stack([same & (s <= t), same & (s >= t), same]).astype(np.float32)


def _gla_group(r, rows, z, st):
    c = GLA_CHUNK
    hk = H_GLA * DK_GLA
    n_chunks = ROWS_GROUP // c
    logits = _bdot(r['gl'][rows, :], r['gw'][z]) + r['gb'][z:z + 1, :]
    g = (jnp.minimum(logits, 0.0) - jnp.log1p(jnp.exp(-jnp.abs(logits)))) / GLA_NORM
    causal = r['masks'][z]
    b = _split_dot(causal.astype(BF16), g)
    tot = _split_dot(r['masks'][2].astype(BF16), g)
    k = r['k'][rows, :]
    q_t = r['q'][rows, :] * jnp.exp(b) * (DK_GLA ** -0.5)
    k_t = (k * jnp.exp(-b)).astype(BF16)
    k_e = (k * jnp.exp(tot - b)).astype(BF16)
    dec = jnp.exp(tot)
    v = r['v'][rows, :].astype(BF16)

    klane = lax.broadcasted_iota(jnp.int32, (c, hk), 1) // DK_GLA
    vlane = lax.broadcasted_iota(jnp.int32, (c, GROUP_WIDTH), 1) // DV_GLA
    causal_c = jnp.concatenate([causal[0:c, 0:c]] * H_GLA, axis=1) != 0.0
    bd = (lax.broadcasted_iota(jnp.int32, (GROUP_WIDTH, hk), 0) // DV_GLA
          == lax.broadcasted_iota(jnp.int32, (GROUP_WIDTH, hk), 1) // DK_GLA)
    q_bf = q_t.astype(BF16)
    zero_k = jnp.zeros((c, hk), BF16)
    zero_v = jnp.zeros((c, GROUP_WIDTH), BF16)
    o_intra = [None] * n_chunks
    for ci in range(n_chunks):
        cr = slice(ci * c, (ci + 1) * c)
        k_stack = jnp.concatenate([jnp.where(klane == h, k_t[cr], zero_k) for h in range(H_GLA)], axis=0)
        v_diag = jnp.concatenate([jnp.where(vlane == h, v[cr], zero_v) for h in range(H_GLA)], axis=0)
        s = lax.dot_general(q_bf[cr], k_stack, (((1,), (1,)), ((), ())), preferred_element_type=F32)
        p = jnp.where(causal_c, s, 0.0).astype(BF16)
        o_intra[ci] = jnp.dot(p, v_diag, preferred_element_type=F32)

    o = [None] * n_chunks
    for step in range(n_chunks):
        ci = step if z == 0 else n_chunks - 1 - step
        cr = slice(ci * c, (ci + 1) * c)
        o[ci] = o_intra[ci] + _bdot_nt(q_bf[cr], st)
        st = st * dec[ci * c:ci * c + 1, :] + jnp.where(bd, _bdot_tn(v[cr], k_e[cr]), 0.0)
    return jnp.concatenate(o, axis=0), st


def _head_rms(o, g):
    hb = (lax.broadcasted_iota(jnp.int32, (GROUP_WIDTH, GROUP_WIDTH), 0) // DV_GLA
          == lax.broadcasted_iota(jnp.int32, (GROUP_WIDTH, GROUP_WIDTH), 1) // DV_GLA)
    ms = _split_dot_rhs(o * o, hb.astype(BF16)) * (1.0 / DV_GLA)
    return o * lax.rsqrt(ms + EPS) * g


def _split_dot_rhs(x, m):
    hi = x.astype(BF16)
    lo = (x - hi.astype(F32)).astype(BF16)
    return jnp.dot(hi, m, preferred_element_type=F32) + jnp.dot(lo, m, preferred_element_type=F32)


def _expand_state(s):
    hk = H_GLA * DK_GLA
    bd = (lax.broadcasted_iota(jnp.int32, (GROUP_WIDTH, hk), 0) // DV_GLA
          == lax.broadcasted_iota(jnp.int32, (GROUP_WIDTH, hk), 1) // DK_GLA)
    return jnp.where(bd, jnp.concatenate([s] * H_GLA, axis=0), 0.0)


def _compact_state(st):
    out = st[0:DV_GLA]
    for h in range(1, H_GLA):
        out = out + st[h * DV_GLA:(h + 1) * DV_GLA]
    return out


def _gla_ctx_kernel(names, *refs):
    r = dict(zip(names, refs))
    hk = H_GLA * DK_GLA
    for bb in range(NB_CTX):
        rows = slice(bb * SEQ, (bb + 1) * SEQ)
        zero = jnp.zeros((GROUP_WIDTH, hk), F32)
        o_f, st_f = _gla_group(r, rows, 0, zero)
        o_b, st_b = _gla_group(r, rows, 1, zero)
        r['o'][rows, :] = _head_rms(o_f + o_b, r['mg'][...])
        r['sf'][bb, 0] = _compact_state(st_f)
        r['sf'][bb, 1] = _compact_state(st_b)


def _gla_lat_kernel(names, *refs):
    r = dict(zip(names, refs))
    n_groups = DEC_SEQ // ROWS_GROUP

    def run(z):
        r['st'][...] = _expand_state(r['s0'][z])

        def body(step, carry):
            gi = step if z == 0 else n_groups - 1 - step
            rows = pl.ds(pl.multiple_of(gi * ROWS_GROUP, ROWS_GROUP), ROWS_GROUP)
            o, st = _gla_group(r, rows, z, r['st'][...])
            r['st'][...] = st
            if z == 0:
                r['o'][rows, :] = o
            else:
                r['o'][rows, :] = _head_rms(r['o'][rows, :] + o, r['mg'][...])
            return carry

        lax.fori_loop(0, n_groups, body, 0)

    run(0)
    run(1)


def _gla(u, prev, *, latent, gate_w, gate_b, mix_g_d, s0=None):
    n = DEC_SEQ if latent else NB_CTX * SEQ
    steps = DEC_BATCH if latent else BATCH // NB_CTX
    row0 = N_CTX // n if latent else 0
    hk = H_GLA * DK_GLA
    const2 = lambda b: (0, 0)
    names = ['q', 'k', 'v', 'gl', 'gw', 'gb', 'mg', 'masks']
    args = [u, u, u, u, gate_w, gate_b, mix_g_d, _gla_masks()]
    in_specs = [
        pl.BlockSpec((n, hk), lambda b: (row0 + b, COL_QD // hk)),
        pl.BlockSpec((n, hk), lambda b: (row0 + b, COL_KD // hk)),
        pl.BlockSpec((n, GROUP_WIDTH), lambda b: (row0 + b, COL_VD // GROUP_WIDTH)),
        pl.BlockSpec((n, 128), lambda b: (row0 + b, COL_GD // 128)),
        pl.BlockSpec((2, 128, hk), lambda b: (0, 0, 0)),
        pl.BlockSpec((2, hk), const2),
        pl.BlockSpec((1, GROUP_WIDTH), const2),
        pl.BlockSpec((3, ROWS_GROUP, ROWS_GROUP), lambda b: (0, 0, 0)),
    ]
    out_shape = jax.ShapeDtypeStruct((N_TOK, GROUP_WIDTH), F32)
    out_specs = pl.BlockSpec((n, GROUP_WIDTH), lambda b: (row0 + b, 0))
    aliases, scratch = {}, []
    if latent:
        names.append('s0'); args.append(s0)
        in_specs.append(pl.BlockSpec((None, 2, DV_GLA, hk), lambda b: (b, 0, 0, 0)))
        aliases = {len(args): 0}
        names.append('prev'); args.append(prev); in_specs.append(_any_spec())
        names += ['o', 'st']
        scratch = [pltpu.VMEM((GROUP_WIDTH, hk), F32)]
        body = functools.partial(_gla_lat_kernel, tuple(names))
    else:
        names += ['o', 'sf']
        out_shape = (out_shape, jax.ShapeDtypeStruct((BATCH, 2, DV_GLA, hk), F32))
        out_specs = (out_specs, pl.BlockSpec((NB_CTX, 2, DV_GLA, hk), lambda b: (b, 0, 0, 0)))
        body = functools.partial(_gla_ctx_kernel, tuple(names))
    return pl.pallas_call(
        body, grid=(steps,), in_specs=in_specs, out_specs=out_specs, out_shape=out_shape,
        scratch_shapes=scratch, input_output_aliases=aliases,
        compiler_params=pltpu.CompilerParams(dimension_semantics=("parallel",)),
        name="gla_latent" if latent else "gla_context",
    )(*args)


def _merge_kernel(h_ref, oa_ref, ob_ref, oc_ref, od_ref, rd_ref, gt_ref, mg_ref, wo_ref, o_ref):
    gw = GROUP_WIDTH

    def gain(g, order):
        return jnp.concatenate([mg_ref[:, g * gw + h * HEAD_DIM:g * gw + (h + 1) * HEAD_DIM] for h in order], axis=1)

    def weight(g, order):
        return jnp.concatenate([wo_ref[g * gw + h * HEAD_DIM:g * gw + (h + 1) * HEAD_DIM, :] for h in order], axis=0)

    natural = range(N_HEADS)
    ys = [
        _rms(oa_ref[...], gain(0, HEAD_ORDER)),
        _rms(ob_ref[...], gain(1, natural)),
        _rms(oc_ref[...], gain(2, HEAD_ORDER)),
        od_ref[...] * _silu(rd_ref[:, COL_RD - COL_GD:COL_RD - COL_GD + gw]),
    ]
    orders = [HEAD_ORDER, natural, HEAD_ORDER, natural]
    mix = jnp.dot(ys[0].astype(BF16), weight(0, orders[0]), preferred_element_type=F32)
    for j in range(1, 4):
        mix = mix + jnp.dot(ys[j].astype(BF16), weight(j, orders[j]), preferred_element_type=F32)
    o_ref[...] = h_ref[...] + gt_ref[...] * mix


def _merge(h, oa, ob, oc, od, u, mod, mix_g, w_out, layer):
    tm = TM_PROJ
    grp = pl.BlockSpec((tm, GROUP_WIDTH), lambda i: (i, 0))
    return pl.pallas_call(
        _merge_kernel,
        grid=(N_TOK // tm,),
        in_specs=[
            pl.BlockSpec((tm, D_MODEL), lambda i: (i, 0)),
            grp, grp, grp, grp,
            pl.BlockSpec((tm, COL_RD_BLOCK), lambda i: (i, COL_GD // COL_RD_BLOCK)),
            _mod_spec(tm, 5, 1),
            pl.BlockSpec((None, 1, D_MODEL), lambda i: (layer, 0, 0)),
            pl.BlockSpec((None, D_MODEL, D_MODEL), lambda i: (layer, 0, 0)),
        ],
        out_specs=pl.BlockSpec((tm, D_MODEL), lambda i: (i, 0)),
        out_shape=jax.ShapeDtypeStruct((N_TOK, D_MODEL), F32),
        compiler_params=pltpu.CompilerParams(dimension_semantics=("parallel",)),
        name="merge",
    )(h, oa, ob, oc, od, u, mod, mix_g, w_out)


def _gate_weights(gate_w):
    out = jnp.zeros((2, 128, H_GLA * DK_GLA), gate_w.dtype)
    out = out.at[0, 0:GLA_RANK].set(gate_w[0])
    return out.at[1, GLA_RANK:2 * GLA_RANK].set(gate_w[1])


def _states_to_kernel(st):
    b = st.shape[0]
    return st.transpose(0, 1, 4, 2, 3).reshape(b, 2, DV_GLA, H_GLA * DK_GLA)


def _states_from_kernel(st):
    b = st.shape[0]
    return st.reshape(b, 2, DV_GLA, H_GLA, DK_GLA).transpose(0, 1, 3, 4, 2)


def kernel(x_prompt, x_sample, cache_swa_k, cache_swa_v, cache_gqa_k, cache_gqa_v, state_gla, c, c_ctx, w_mod, b_mod, norm_g, ffn_w_in, ffn_w_out, w_in, w_out, mix_g, swa_sink, qk_norm_g, hy_conv_w, hy_conv_b, hy_w1, hy_b1, hy_w2, hy_b2, hy_w3, hy_freq, hy_bias, gla_gate_w, gla_gate_b, final_g):
    cond = jnp.zeros((MOD_ROWS, D_MODEL), F32).at[0].set(c_ctx).at[1:1 + DEC_BATCH].set(c)
    mod_all = _modulation(cond, w_mod, b_mod).reshape(DEPTH, MOD_ROWS, N_MOD, 1, D_MODEL)

    dft = {n: tuple(jnp.asarray(m).astype(BF16) for m in _dft_matrices(n)) for n in (SEQ, DEC_SEQ)}

    xs = [x_prompt.reshape(N_CTX, D_MODEL), x_sample.reshape(N_LAT, D_MODEL)]
    ffn_in = ffn_w_in.astype(BF16)
    ffn_out = ffn_w_out.astype(BF16)
    ng = norm_g.reshape(DEPTH, 3, 1, D_MODEL)
    out_w = w_out.astype(BF16)
    mix_gp = mix_g.reshape(DEPTH, 1, D_MODEL)
    cache_shape = (DEC_BATCH, DEPTH, PAST_LEN, KV_WIDTH)
    ck_a, cv_a = cache_swa_k.reshape(cache_shape), cache_swa_v.reshape(cache_shape)
    ck_c, cv_c = cache_gqa_k.reshape(cache_shape), cache_gqa_v.reshape(cache_shape)
    cache_a = cache_c = None
    sts = []
    for l in range(DEPTH):
        mod = mod_all[l]
        h = _ffn(xs, mod, 0, ng, ffn_in, ffn_out, l, 0)
        u = _in_proj(h, mod, ng, w_in, l)

        qkg2 = jnp.tile(qk_norm_g[l], (1, N_KV))

        oa, *cache_a = _attention_context(u, l, cache_a, col_q=COL_QA, col_k=COL_KA, col_v=COL_VA,
                                          sink=swa_sink[l])
        oa = _attention_latent(u, oa, l, col_q=COL_QA, col_k=COL_KA, col_v=COL_VA, cache_k=ck_a, cache_v=cv_a,
                               window=True, sink=swa_sink[l])
        oc, *cache_c = _attention_context(u, l, cache_c, col_q=COL_QC, col_k=COL_KC, col_v=COL_VC, qkg2=qkg2)
        oc = _attention_latent(u, oc, l, col_q=COL_QC, col_k=COL_KC, col_v=COL_VC, cache_k=ck_c, cache_v=cv_c,
                               qkg2=qkg2)

        w1p = jnp.zeros((128, HY_FH), F32).at[:HY_EMB].set(hy_w1[l])
        ob = None
        for latent, n in ((False, SEQ), (True, DEC_SEQ)):
            fwd, inv = dft[n]
            kr, ki = _hyena_filters(n, fwd, w1p, hy_b1[l], hy_w2[l], hy_b2[l], hy_w3[l], hy_freq[l])
            ob = _hyena(u, ob, latent=latent, fwd=fwd, inv=inv, kr=kr, ki=ki, conv_w=hy_conv_w[l],
                        conv_b=hy_conv_b[l], bias=hy_bias[l])

        gw = _gate_weights(gla_gate_w[l]).astype(BF16)
        mg_d = mix_g[l, 3 * GROUP_WIDTH:].reshape(1, GROUP_WIDTH)
        od, st = _gla(u, None, latent=False, gate_w=gw, gate_b=gla_gate_b[l], mix_g_d=mg_d)
        od = _gla(u, od, latent=True, gate_w=gw, gate_b=gla_gate_b[l], mix_g_d=mg_d,
                  s0=_states_to_kernel(state_gla[:, l]))

        h = _merge(h, oa, ob, oc, od, u, mod, mix_gp, out_w, l)
        xs = [_ffn([h], mod, 6, ng, ffn_in, ffn_out, l, 1)] if l + 1 < DEPTH else None
        if xs is None:
            y_prompt, y_sample = _ffn([h], mod, 6, ng, ffn_in, ffn_out, l, 1, final_g=final_g)
        sts.append(_states_from_kernel(st))

    y_prompt = y_prompt.reshape(BATCH, SEQ, D_MODEL)
    y_sample = y_sample.reshape(DEC_BATCH, DEC_SEQ, D_MODEL)
    caches = [x.reshape(BATCH, DEPTH, SEQ, N_KV, HEAD_DIM) for x in (*cache_a, *cache_c)]
    return (y_prompt, y_sample, *caches, jnp.stack(sts, axis=1))
```

```python
import functools
import math

import numpy as np
import jax
import jax.numpy as jnp
from jax import lax
from jax.experimental import pallas as pl
from jax.experimental.pallas import tpu as pltpu

F32 = jnp.float32
BF16 = jnp.bfloat16

D_MODEL = 1024
BATCH = 32
SEQ = 256
DEPTH = 2
DEC_BATCH = 2
DEC_SEQ = 1024
PAST_LEN = 512
GRID_W = 64
HEAD_DIM = 64
GROUP_WIDTH = 256
N_HEADS = 4
N_KV = 2
KV_WIDTH = N_KV * HEAD_DIM
WINDOW = 128
HY_CH = 256
HY_ORDER = 2
HY_BANDS = 16
HY_EMB = 2 * HY_BANDS + 1
HY_FH = 64
HY_TARGET = 1e-2
HY_FAST = 0.3
HY_SLOW = 1.5
H_GLA = 4
DV_GLA = 64
DK_GLA = 32
GLA_RANK = 16
GLA_CHUNK = 64
GLA_NORM = 16.0
D_FF = 2816
N_MOD = 9
ROPE_THETA = 10000.0
EPS = 1e-6

N_CTX = BATCH * SEQ
N_LAT = DEC_BATCH * DEC_SEQ
N_TOK = N_CTX + N_LAT
MOD_ROWS = 8

COL_HY, COL_QA, COL_KA, COL_VA = 0, 768, 1024, 1152
COL_QC, COL_KC, COL_VC = 1280, 1536, 1664
COL_QD, COL_KD, COL_VD, COL_GD, COL_RD = 1792, 1920, 2048, 2304, 2336
PROJ_WIDTH = 2688
REF_QA, REF_KA, REF_HY, REF_QC, REF_KC, REF_END = 0, 256, 512, 1280, 1536, 2592
COL_RD_BLOCK = 384
HEAD_ORDER = (0, 2, 1, 3)
_HEAD_PERM = np.concatenate([np.arange(HEAD_DIM) + HEAD_DIM * h for h in HEAD_ORDER])

TM_FFN = 1024
TK_FFN = 256
FFN_VMEM_BYTES = 60 * 1024 * 1024
TM_PROJ = 1024
TQ_ATTN = 256
ROWS_GROUP = 256
NB_CTX = 2
NB_HYENA = 4


def _bdot(a, b):
    return jnp.dot(a.astype(BF16), b.astype(BF16), preferred_element_type=F32)


def _bdot_nt(a, b):
    return lax.dot_general(a.astype(BF16), b.astype(BF16), (((1,), (1,)), ((), ())),
                           preferred_element_type=F32)


def _bdot_tn(a, b):
    return lax.dot_general(a.astype(BF16), b.astype(BF16), (((0,), (0,)), ((), ())),
                           preferred_element_type=F32)


def _fdot(a, b):
    a_hi, b_hi = a.astype(BF16), b.astype(BF16)
    a_lo = (a - a_hi.astype(F32)).astype(BF16)
    b_lo = (b - b_hi.astype(F32)).astype(BF16)
    dot = functools.partial(jnp.dot, preferred_element_type=F32)
    return dot(a_hi, b_hi) + (dot(a_hi, b_lo) + dot(a_lo, b_hi) + dot(a_lo, b_lo))


def _split_dot(m, x):
    hi = x.astype(BF16)
    lo = (x - hi.astype(F32)).astype(BF16)
    return jnp.dot(m, hi, preferred_element_type=F32) + jnp.dot(m, lo, preferred_element_type=F32)


def _rms(x, g):
    return x * lax.rsqrt(jnp.mean(x * x, axis=-1, keepdims=True) + EPS) * g


def _silu(x):
    return x * jax.nn.sigmoid(x)


def _mod_row(i, tm):
    n_ctx_tiles = N_CTX // tm
    per_batch = DEC_SEQ // tm
    return jnp.where(i < n_ctx_tiles, 0, 1 + (i - n_ctx_tiles) // per_batch)


def _mod_spec(tm, j, grid_rank):
    if grid_rank == 1:
        return pl.BlockSpec((None, None, 1, D_MODEL), lambda i: (_mod_row(i, tm), j, 0, 0))
    return pl.BlockSpec((None, None, 1, D_MODEL), lambda i, k: (_mod_row(i, tm), j, 0, 0))


def _any_spec():
    return pl.BlockSpec(memory_space=pl.ANY)


def _mod_kernel(cond_ref, w_ref, b_ref, o_ref):
    o_ref[...] = _bdot(_silu(cond_ref[...]), w_ref[...]) + b_ref[...]


def _modulation(cond, w_mod, b_mod):
    tn = D_MODEL
    return pl.pallas_call(
        _mod_kernel,
        grid=(DEPTH, N_MOD * D_MODEL // tn),
        in_specs=[
            pl.BlockSpec((MOD_ROWS, D_MODEL), lambda l, j: (0, 0)),
            pl.BlockSpec((None, D_MODEL, tn), lambda l, j: (l, 0, j)),
            pl.BlockSpec((None, 1, tn), lambda l, j: (l, 0, j)),
        ],
        out_specs=pl.BlockSpec((None, MOD_ROWS, tn), lambda l, j: (l, 0, j)),
        out_shape=jax.ShapeDtypeStruct((DEPTH, MOD_ROWS, N_MOD * D_MODEL), F32),
        name="modulation",
    )(cond, w_mod, b_mod.reshape(DEPTH, 1, N_MOD * D_MODEL))


def _ffn_kernel(split_input, final, *refs):
    is_ctx = pl.program_id(0) < N_CTX // TM_FFN
    if split_input:
        xa_ref, xb_ref, *refs = refs
        x = jnp.where(is_ctx, xa_ref[...], xb_ref[...])
    else:
        x_ref, *refs = refs
        x = x_ref[...]
    if final:
        sh_ref, sc_ref, gt_ref, ng_ref, wi_ref, wo_ref, fg_ref, yc_ref, yl_ref, xm_ref, act_ref = refs
    else:
        sh_ref, sc_ref, gt_ref, ng_ref, wi_ref, wo_ref, o_ref, xm_ref, act_ref = refs
    xn = _rms(x, ng_ref[...])
    xm_ref[...] = (xn * (1.0 + sc_ref[...]) + sh_ref[...]).astype(BF16)
    for c0 in range(0, D_FF, TK_FFN):
        c1 = min(c0 + TK_FFN, D_FF)
        xm = xm_ref[...]
        a = jnp.dot(xm, wi_ref[:, c0:c1], preferred_element_type=F32)
        b = jnp.dot(xm, wi_ref[:, D_FF + c0:D_FF + c1], preferred_element_type=F32)
        act_ref[:, c0:c1] = (_silu(a) * b).astype(BF16)
    y = jnp.dot(act_ref[...], wo_ref[...], preferred_element_type=F32)
    out = x + 0.5 * gt_ref[...] * y
    if not final:
        o_ref[...] = out
        return
    out = _rms(out, fg_ref[...])

    @pl.when(is_ctx)
    def _():
        yc_ref[...] = out

    @pl.when(jnp.logical_not(is_ctx))
    def _():
        yl_ref[...] = out


def _ffn(xs, mod, j0, ng, w_in, w_out, layer, which, final_g=None):
    tm = TM_FFN
    n_ctx_tiles = N_CTX // tm
    split_specs = [pl.BlockSpec((tm, D_MODEL), lambda i: (jnp.minimum(i, n_ctx_tiles - 1), 0)),
                   pl.BlockSpec((tm, D_MODEL), lambda i: (jnp.maximum(i - n_ctx_tiles, 0), 0))]
    whole_spec = pl.BlockSpec((tm, D_MODEL), lambda i: (i, 0))
    x_specs = split_specs if len(xs) == 2 else [whole_spec]
    resident = dict(pipeline_mode=pl.Buffered(1))
    final = final_g is not None
    extra_args, extra_specs = [], []
    if final:
        extra_args = [final_g.reshape(1, D_MODEL)]
        extra_specs = [pl.BlockSpec((1, D_MODEL), lambda i: (0, 0))]
        out_specs = tuple(split_specs)
        out_shape = (jax.ShapeDtypeStruct((N_CTX, D_MODEL), F32), jax.ShapeDtypeStruct((N_LAT, D_MODEL), F32))
    else:
        out_specs = whole_spec
        out_shape = jax.ShapeDtypeStruct((N_TOK, D_MODEL), F32)
    return pl.pallas_call(
        functools.partial(_ffn_kernel, len(xs) == 2, final),
        grid=(N_TOK // tm,),
        in_specs=x_specs + [
            _mod_spec(tm, j0, 1), _mod_spec(tm, j0 + 1, 1), _mod_spec(tm, j0 + 2, 1),
            pl.BlockSpec((None, None, 1, D_MODEL), lambda i: (layer, j0 // 3, 0, 0)),
            pl.BlockSpec((None, None, D_MODEL, 2 * D_FF), lambda i: (layer, which, 0, 0), **resident),
            pl.BlockSpec((None, None, D_FF, D_MODEL), lambda i: (layer, which, 0, 0), **resident),
        ] + extra_specs,
        out_specs=out_specs,
        out_shape=out_shape,
        scratch_shapes=[pltpu.VMEM((tm, D_MODEL), BF16), pltpu.VMEM((tm, D_FF), BF16)],
        compiler_params=pltpu.CompilerParams(dimension_semantics=("arbitrary",),
                                             vmem_limit_bytes=FFN_VMEM_BYTES),
        name="ffn",
    )(*xs, mod, mod, mod, ng, w_in, w_out, *extra_args)


def _proj_kernel(x_ref, sh_ref, sc_ref, ng_ref, w_ref, o_ref):
    xn = _rms(x_ref[...], ng_ref[...])
    xm = (xn * (1.0 + sc_ref[...]) + sh_ref[...]).astype(BF16)

    def put(col, w):
        o_ref[:, col:col + w.shape[1]] = jnp.dot(xm, w, preferred_element_type=F32)

    def queries(base):
        return jnp.concatenate([w_ref[:, base + HEAD_DIM * h:base + HEAD_DIM * (h + 1)] for h in HEAD_ORDER], axis=1)

    put(COL_HY, w_ref[:, REF_HY:REF_QC])
    put(COL_QA, queries(REF_QA))
    put(COL_KA, w_ref[:, REF_KA:REF_HY])
    put(COL_QC, queries(REF_QC))
    put(COL_KC, w_ref[:, REF_KC:COL_GD])
    put(COL_GD, w_ref[:, COL_GD:REF_END])
    o_ref[:, REF_END:PROJ_WIDTH] = jnp.zeros((x_ref.shape[0], PROJ_WIDTH - REF_END), F32)


def _in_proj(h, mod, ng, w, layer):
    tm = TM_PROJ
    return pl.pallas_call(
        _proj_kernel,
        grid=(N_TOK // tm,),
        in_specs=[
            pl.BlockSpec((tm, D_MODEL), lambda i: (i, 0)),
            _mod_spec(tm, 3, 1), _mod_spec(tm, 4, 1),
            pl.BlockSpec((None, None, 1, D_MODEL), lambda i: (layer, 1, 0, 0)),
            pl.BlockSpec((None, D_MODEL, PROJ_WIDTH), lambda i: (layer, 0, 0)),
        ],
        out_specs=pl.BlockSpec((tm, PROJ_WIDTH), lambda i: (i, 0)),
        out_shape=jax.ShapeDtypeStruct((N_TOK, PROJ_WIDTH), F32),
        compiler_params=pltpu.CompilerParams(dimension_semantics=("parallel",)),
        name="in_proj",
    )(h, mod, mod, ng, w)


def _rope_tables(length):
    rows = length // GRID_W
    r = np.repeat(np.arange(rows, dtype=np.float32), GRID_W)
    col = np.tile(np.arange(GRID_W, dtype=np.float32), rows)
    nf = HEAD_DIM // 4
    inv = (np.float32(ROPE_THETA) ** (-np.arange(nf, dtype=np.float32) / nf)).astype(np.float32)
    ang = np.concatenate([r[:, None] * inv, col[:, None] * inv], axis=-1).astype(np.float32)
    cos, sin = np.cos(ang).astype(np.float32), np.sin(ang).astype(np.float32)
    return np.tile(cos, (1, 4)), np.tile(np.concatenate([-sin, sin], axis=-1), (1, 2))


def _pair_lanes(rows):
    lane = lax.broadcasted_iota(jnp.int32, (rows, 2 * HEAD_DIM), 1)
    return lane < HEAD_DIM, (lane % HEAD_DIM) < HEAD_DIM // 2


def _rope_pair(x, cos, sin_signed, first_half):
    partner = jnp.where(first_half, pltpu.roll(x, 3 * HEAD_DIM // 2, 1), pltpu.roll(x, HEAD_DIM // 2, 1))
    return x * cos + partner * sin_signed


def _rms_pair(x, g, lo):
    x2 = x * x
    s_lo = jnp.sum(jnp.where(lo, x2, 0.0), axis=-1, keepdims=True)
    s_hi = jnp.sum(jnp.where(lo, 0.0, x2), axis=-1, keepdims=True)
    ms = jnp.where(lo, s_lo, s_hi) * (1.0 / HEAD_DIM)
    return x * lax.rsqrt(ms + EPS) * g


def _attn_core(q, k_bf, v_bf, lo, *, mask=None, ctx=None, sinks=None):
    tq = q.shape[0]
    pair = 2 * HEAD_DIM
    scale = HEAD_DIM ** -0.5
    qa, qb = q[:, :pair] * scale, q[:, pair:] * scale
    qs = jnp.concatenate([jnp.where(lo, qa, 0.0), jnp.where(lo, 0.0, qa),
                          jnp.where(lo, qb, 0.0), jnp.where(lo, 0.0, qb)], axis=0).astype(BF16)
    s = _bdot_nt(qs, k_bf)
    if ctx is not None:
        s_c = _bdot_nt(qs, ctx[0])
    ps, pcs, dens = [], [], []
    for j in range(N_HEADS):
        rows = slice(j * tq, (j + 1) * tq)
        sj = s[rows]
        if mask is not None:
            sj = jnp.where(mask, sj, -1e30)
        m = jnp.max(sj, axis=-1, keepdims=True)
        if ctx is not None:
            m = jnp.maximum(m, jnp.max(s_c[rows], axis=-1, keepdims=True))
        if sinks is not None:
            m = jnp.maximum(m, sinks[j])
        p = jnp.exp(sj - m)
        den = jnp.sum(p, axis=-1, keepdims=True)
        ps.append(p.astype(BF16))
        if ctx is not None:
            pc = jnp.exp(s_c[rows] - m)
            den = den + jnp.sum(pc, axis=-1, keepdims=True)
            pcs.append(pc.astype(BF16))
        if sinks is not None:
            den = den + jnp.exp(sinks[j] - m)
        dens.append(den)
    r = jnp.dot(jnp.concatenate(ps, axis=0), v_bf, preferred_element_type=F32)
    if ctx is not None:
        r = r + jnp.dot(jnp.concatenate(pcs, axis=0), ctx[1], preferred_element_type=F32)
    o = [r[j * tq:(j + 1) * tq] / dens[j] for j in range(N_HEADS)]
    return jnp.concatenate([jnp.where(lo, o[0], o[1]), jnp.where(lo, o[2], o[3])], axis=1)


def _attn_ctx_kernel(names, *refs):
    r = dict(zip(names, refs))
    lo, _ = _pair_lanes(SEQ)
    sinks = [r['sink'][h] for h in HEAD_ORDER] if 'sink' in r else None
    for bb in range(NB_CTX):
        rows = slice(bb * SEQ, (bb + 1) * SEQ)
        q, k, v = r['q'][rows, :], r['k'][rows, :], r['v'][rows, :]
        if 'qkg' in r:
            gq, gk = r['qkg'][0:1, :], r['qkg'][1:2, :]
            q = jnp.concatenate([_rms_pair(q[:, :KV_WIDTH], gq, lo), _rms_pair(q[:, KV_WIDTH:], gq, lo)], axis=1)
            k = _rms_pair(k, gk, lo)
        r['k_out'][bb] = k
        r['v_out'][bb] = v
        r['o'][rows, :] = _attn_core(q, k.astype(BF16), v.astype(BF16), lo, sinks=sinks)


def _attn_lat_kernel(names, window, *refs):
    r = dict(zip(names, refs))
    tq = TQ_ATTN
    i = pl.program_id(1)
    lo, first_half = _pair_lanes(tq)
    qk_norm = 'qkg' in r

    @pl.when(i == 0)
    def _():
        lo_k, first_half_k = _pair_lanes(DEC_SEQ)
        k = r['k'][...]
        if qk_norm:
            k = _rms_pair(k, r['qkg'][1:2, :], lo_k)
        r['k_scr'][...] = _rope_pair(k, r['cos_k'][...], r['sin_k'][...], first_half_k).astype(BF16)

    q = r['q'][...]
    halves = []
    for c0 in (0, KV_WIDTH):
        x = q[:, c0:c0 + KV_WIDTH]
        if qk_norm:
            x = _rms_pair(x, r['qkg'][0:1, :], lo)
        halves.append(_rope_pair(x, r['cos_q'][...], r['sin_q'][...], first_half))
    q = jnp.concatenate(halves, axis=1)
    sinks = [r['sink'][h] for h in HEAD_ORDER] if 'sink' in r else None
    ctx = (r['kc'][...].astype(BF16), r['vc'][...].astype(BF16))
    if window:
        span = tq + 2 * WINDOW
        start = pl.multiple_of(jnp.clip(i * tq - WINDOW, 0, DEC_SEQ - span), WINDOW)
        qpos = i * tq + lax.broadcasted_iota(jnp.int32, (tq, span), 0)
        kpos = start + lax.broadcasted_iota(jnp.int32, (tq, span), 1)
        mask = jnp.abs(qpos - kpos) <= WINDOW
        k_bf = r['k_scr'][pl.ds(start, span), :]
        v_bf = r['v'][pl.ds(start, span), :].astype(BF16)
    else:
        mask = None
        k_bf = r['k_scr'][...]
        v_bf = r['v'][...].astype(BF16)
    r['o'][...] = _attn_core(q, k_bf, v_bf, lo, mask=mask, ctx=ctx, sinks=sinks)


def _attention_context(u, layer, cache_prev, *, col_q, col_k, col_v, sink=None, qkg2=None):
    rows = NB_CTX * SEQ
    qb, kb, vb = col_q // GROUP_WIDTH, col_k // KV_WIDTH, col_v // KV_WIDTH
    names = ['q', 'k', 'v']
    args = [u, u, u]
    in_specs = [pl.BlockSpec((rows, GROUP_WIDTH), lambda b: (b, qb)),
                pl.BlockSpec((rows, KV_WIDTH), lambda b: (b, kb)),
                pl.BlockSpec((rows, KV_WIDTH), lambda b: (b, vb))]
    if sink is not None:
        names.append('sink'); args.append(sink)
        in_specs.append(pl.BlockSpec(memory_space=pltpu.SMEM))
    if qkg2 is not None:
        names.append('qkg'); args.append(qkg2)
        in_specs.append(pl.BlockSpec((2, KV_WIDTH), lambda b: (0, 0)))
    aliases = {}
    if cache_prev is not None:
        for j, prev in enumerate(cache_prev):
            aliases[len(args)] = 1 + j
            names.append(f'prev{j}'); args.append(prev); in_specs.append(_any_spec())
    names += ['o', 'k_out', 'v_out']
    cache_shape = jax.ShapeDtypeStruct((BATCH, DEPTH, SEQ, KV_WIDTH), F32)
    cache_spec = pl.BlockSpec((NB_CTX, None, SEQ, KV_WIDTH), lambda b: (b, layer, 0, 0))
    return pl.pallas_call(
        functools.partial(_attn_ctx_kernel, tuple(names)),
        grid=(BATCH // NB_CTX,), in_specs=in_specs,
        out_specs=(pl.BlockSpec((rows, GROUP_WIDTH), lambda b: (b, 0)), cache_spec, cache_spec),
        out_shape=(jax.ShapeDtypeStruct((N_TOK, GROUP_WIDTH), F32), cache_shape, cache_shape),
        input_output_aliases=aliases,
        compiler_params=pltpu.CompilerParams(dimension_semantics=("parallel",)),
        name="attn_context",
    )(*args)


def _attention_latent(u, prev, layer, *, col_q, col_k, col_v, cache_k, cache_v, window=False, sink=None,
                      qkg2=None):
    tq = TQ_ATTN
    nq = DEC_SEQ // tq
    row0_q, row0_k = N_CTX // tq, N_CTX // DEC_SEQ
    qb, kb, vb = col_q // GROUP_WIDTH, col_k // KV_WIDTH, col_v // KV_WIDTH
    cos, sin = _rope_tables(DEC_SEQ)
    const = lambda b, i: (0, 0)
    names = ['q', 'k', 'v', 'kc', 'vc', 'cos_q', 'sin_q', 'cos_k', 'sin_k']
    args = [u, u, u, cache_k, cache_v, cos, sin, cos, sin]
    in_specs = [
        pl.BlockSpec((tq, GROUP_WIDTH), lambda b, i: (row0_q + b * nq + i, qb)),
        pl.BlockSpec((DEC_SEQ, KV_WIDTH), lambda b, i: (row0_k + b, kb)),
        pl.BlockSpec((DEC_SEQ, KV_WIDTH), lambda b, i: (row0_k + b, vb)),
        pl.BlockSpec((None, None, PAST_LEN, KV_WIDTH), lambda b, i: (b, layer, 0, 0)),
        pl.BlockSpec((None, None, PAST_LEN, KV_WIDTH), lambda b, i: (b, layer, 0, 0)),
        pl.BlockSpec((tq, KV_WIDTH), lambda b, i: (i, 0)),
        pl.BlockSpec((tq, KV_WIDTH), lambda b, i: (i, 0)),
        pl.BlockSpec((DEC_SEQ, KV_WIDTH), const),
        pl.BlockSpec((DEC_SEQ, KV_WIDTH), const),
    ]
    if sink is not None:
        names.append('sink'); args.append(sink)
        in_specs.append(pl.BlockSpec(memory_space=pltpu.SMEM))
    if qkg2 is not None:
        names.append('qkg'); args.append(qkg2)
        in_specs.append(pl.BlockSpec((2, KV_WIDTH), const))
    aliases = {len(args): 0}
    names.append('prev'); args.append(prev); in_specs.append(_any_spec())
    names += ['o', 'k_scr']
    return pl.pallas_call(
        functools.partial(_attn_lat_kernel, tuple(names), window),
        grid=(DEC_BATCH, nq), in_specs=in_specs,
        out_specs=pl.BlockSpec((tq, GROUP_WIDTH), lambda b, i: (row0_q + b * nq + i, 0)),
        out_shape=jax.ShapeDtypeStruct((N_TOK, GROUP_WIDTH), F32),
        scratch_shapes=[pltpu.VMEM((DEC_SEQ, KV_WIDTH), BF16)],
        input_output_aliases=aliases,
        compiler_params=pltpu.CompilerParams(dimension_semantics=("parallel", "arbitrary")),
        name="attn_latent",
    )(*args)


def _dft_matrices(length):
    n = length
    k = np.arange(n, dtype=np.int64)[:, None]
    s = np.arange(n, dtype=np.int64)[None, :]
    ang = np.pi * ((k * s) % (2 * n)).astype(np.float64) / n
    fwd_cos = np.cos(ang)
    fwd_sin = -np.sin(ang)
    fwd_sin[0, :] = 1.0 - 2.0 * (np.arange(n) % 2)
    fwd = np.concatenate([fwd_cos, fwd_sin], axis=0)
    wk = np.full((n,), 2.0)
    wk[0] = 1.0
    inv_cos = (np.cos(ang) * wk[:, None]).T / (2 * n)
    inv_sin = (-2.0 * np.sin(ang)).T / (2 * n)
    inv_sin[:, 0] = (1.0 - 2.0 * (np.arange(n) % 2)) / (2 * n)
    inv = np.concatenate([inv_cos, inv_sin], axis=1)
    return fwd.astype(np.float32), inv.astype(np.float32)


def _filter_features(length):
    t = np.linspace(0.0, 1.0, length, dtype=np.float32)[:, None]
    w = (np.float32(2.0 * math.pi / length) * np.arange(length, dtype=np.float32))[:, None]
    bands = np.linspace(1e-4, HY_BANDS - 1, HY_BANDS, dtype=np.float32)[None, :]
    z = np.concatenate([t, np.cos(bands * w), -np.sin(bands * w)], axis=-1).astype(np.float32)
    zp = np.zeros((length, 128), np.float32)
    zp[:, :HY_EMB] = z
    deltas = np.linspace(math.log(HY_TARGET) / HY_SLOW, math.log(HY_TARGET) / HY_FAST, HY_CH, dtype=np.float32)
    decay = np.exp(-t * np.abs(deltas)).astype(np.float32)
    return zp, decay


def _filter_kernel(z_ref, w1_ref, b1_ref, w2_ref, b2_ref, w3_ref, fr_ref, dec_ref, f_ref, kr_ref, ki_ref):
    n = z_ref.shape[0]
    h = jnp.sin(fr_ref[0:1, :] * (_fdot(z_ref[...], w1_ref[...]) + b1_ref[...]))
    h = jnp.sin(fr_ref[1:2, :] * (_fdot(h, w2_ref[...]) + b2_ref[...]))
    h = _fdot(h, w3_ref[...])
    dec = dec_ref[...]
    half = HY_ORDER * HY_CH
    pos = jnp.concatenate([h[:, j * HY_CH:(j + 1) * HY_CH] * dec for j in range(HY_ORDER)], axis=-1)
    neg = jnp.concatenate([h[:, half + j * HY_CH:half + (j + 1) * HY_CH] * dec for j in range(HY_ORDER)], axis=-1)
    row = lax.broadcasted_iota(jnp.int32, (n, half), 0)
    neg = jnp.where(row == 0, 0.0, neg)
    fwd = f_ref[...]
    a = jnp.dot(fwd, (pos + neg).astype(BF16), preferred_element_type=F32)
    b = jnp.dot(fwd, (pos - neg).astype(BF16), preferred_element_type=F32)
    kr_ref[...] = a[:n]
    ki_ref[...] = jnp.where(row == 0, a[n:], b[n:])


def _hyena_filters(length, fwd, w1p, b1, w2, b2, w3, freq):
    zp, decay = _filter_features(length)
    half = HY_ORDER * HY_CH
    return pl.pallas_call(
        _filter_kernel,
        out_shape=(jax.ShapeDtypeStruct((length, half), F32), jax.ShapeDtypeStruct((length, half), F32)),
        name="hyena_filters",
    )(zp, w1p, b1.reshape(1, HY_FH), w2, b2.reshape(1, HY_FH), w3, freq, decay, fwd)


def _hyena_kernel(z_ref, cw_ref, cb_ref, f_ref, g_ref, kr_ref, ki_ref, bias_ref, o_ref, *, n):
    nb = z_ref.shape[0] // n
    row = lax.broadcasted_iota(jnp.int32, (n, 3 * HY_CH), 0)
    row0 = lax.broadcasted_iota(jnp.int32, (n, HY_CH), 0) == 0
    vs, x1s, x2s = [], [], []
    for bb in range(nb):
        z = z_ref[bb * n:(bb + 1) * n, :]
        prev = jnp.where(row == 0, 0.0, pltpu.roll(z, 1, 0))
        nxt = jnp.where(row == n - 1, 0.0, pltpu.roll(z, n - 1, 0))
        z = prev * cw_ref[0:1, :] + z * cw_ref[1:2, :] + nxt * cw_ref[2:3, :] + cb_ref[...]
        vs.append(z[:, :HY_CH])
        x1s.append(z[:, HY_CH:2 * HY_CH])
        x2s.append(z[:, 2 * HY_CH:])

    def long_conv(xs, order):
        cols = slice(order * HY_CH, (order + 1) * HY_CH)
        kr, ki = kr_ref[:, cols], ki_ref[:, cols]
        spec = jnp.dot(f_ref[...], jnp.concatenate([x.astype(BF16) for x in xs], axis=1),
                       preferred_element_type=F32)
        prods = []
        for bb in range(nb):
            ur, ui = spec[:n, bb * HY_CH:(bb + 1) * HY_CH], spec[n:, bb * HY_CH:(bb + 1) * HY_CH]
            yr = ur * kr - jnp.where(row0, 0.0, ui * ki)
            yi = jnp.where(row0, ui * ki, ur * ki + ui * kr)
            prods.append(jnp.concatenate([yr, yi], axis=0).astype(BF16))
        y = jnp.dot(g_ref[...], jnp.concatenate(prods, axis=1), preferred_element_type=F32)
        return [y[:, bb * HY_CH:(bb + 1) * HY_CH] + xs[bb] * bias_ref[order:order + 1, :] for bb in range(nb)]

    ys = long_conv(vs, 0)
    ys = long_conv([x1s[bb] * ys[bb] for bb in range(nb)], 1)
    for bb in range(nb):
        o_ref[bb * n:(bb + 1) * n, :] = x2s[bb] * ys[bb]


def _hyena(u, prev, *, latent, fwd, inv, kr, ki, conv_w, conv_b, bias):
    n = DEC_SEQ if latent else SEQ
    nb = DEC_BATCH if latent else NB_HYENA
    steps = 1 if latent else BATCH // nb
    row0 = N_CTX // (nb * n) if latent else 0
    half = HY_ORDER * HY_CH
    const = lambda b: (0, 0)
    args = [u, conv_w, conv_b.reshape(1, 3 * HY_CH), fwd, inv, kr, ki, bias]
    in_specs = [
        pl.BlockSpec((nb * n, 3 * HY_CH), lambda b: (row0 + b, COL_HY // (3 * HY_CH))),
        pl.BlockSpec((3, 3 * HY_CH), const),
        pl.BlockSpec((1, 3 * HY_CH), const),
        pl.BlockSpec((2 * n, n), const),
        pl.BlockSpec((n, 2 * n), const),
        pl.BlockSpec((n, half), const),
        pl.BlockSpec((n, half), const),
        pl.BlockSpec((HY_ORDER, HY_CH), const),
    ]
    aliases = {}
    if latent:
        aliases = {len(args): 0}
        args.append(prev)
        in_specs.append(_any_spec())

    def body(*refs):
        if latent:
            refs = refs[:len(args) - 1] + refs[len(args):]
        _hyena_kernel(*refs, n=n)

    return pl.pallas_call(
        body, grid=(steps,), in_specs=in_specs,
        out_specs=pl.BlockSpec((nb * n, HY_CH), lambda b: (row0 + b, 0)),
        out_shape=jax.ShapeDtypeStruct((N_TOK, HY_CH), F32),
        input_output_aliases=aliases,
        compiler_params=pltpu.CompilerParams(dimension_semantics=("parallel",)),
        name="hyena_latent" if latent else "hyena_context",
    )(*args)


def _gla_masks():
    t = np.arange(ROWS_GROUP)[:, None]
    s = np.arange(ROWS_GROUP)[None, :]
    same = (t // GLA_CHUNK) == (s // GLA_CHUNK)
    return np.stack([same & (s <= t), same & (s >= t), same]).astype(np.float32)


def _gla_group(r, rows, z, st):
    c = GLA_CHUNK
    hk = H_GLA * DK_GLA
    n_chunks = ROWS_GROUP // c
    logits = _bdot(r['gl'][rows, :], r['gw'][z]) + r['gb'][z:z + 1, :]
    g = (jnp.minimum(logits, 0.0) - jnp.log1p(jnp.exp(-jnp.abs(logits)))) / GLA_NORM
    causal = r['masks'][z]
    b = _split_dot(causal.astype(BF16), g)
    tot = _split_dot(r['masks'][2].astype(BF16), g)
    k = r['k'][rows, :]
    q_t = r['q'][rows, :] * jnp.exp(b) * (DK_GLA ** -0.5)
    k_t = (k * jnp.exp(-b)).astype(BF16)
    k_e = (k * jnp.exp(tot - b)).astype(BF16)
    dec = jnp.exp(tot)
    v = r['v'][rows, :].astype(BF16)

    klane = lax.broadcasted_iota(jnp.int32, (c, hk), 1) // DK_GLA
    vlane = lax.broadcasted_iota(jnp.int32, (c, GROUP_WIDTH), 1) // DV_GLA
    causal_c = jnp.concatenate([causal[0:c, 0:c]] * H_GLA, axis=1) != 0.0
    bd = (lax.broadcasted_iota(jnp.int32, (GROUP_WIDTH, hk), 0) // DV_GLA
          == lax.broadcasted_iota(jnp.int32, (GROUP_WIDTH, hk), 1) // DK_GLA)
    q_bf = q_t.astype(BF16)
    zero_k = jnp.zeros((c, hk), BF16)
    zero_v = jnp.zeros((c, GROUP_WIDTH), BF16)
    o_intra = [None] * n_chunks
    for ci in range(n_chunks):
        cr = slice(ci * c, (ci + 1) * c)
        k_stack = jnp.concatenate([jnp.where(klane == h, k_t[cr], zero_k) for h in range(H_GLA)], axis=0)
        v_diag = jnp.concatenate([jnp.where(vlane == h, v[cr], zero_v) for h in range(H_GLA)], axis=0)
        s = lax.dot_general(q_bf[cr], k_stack, (((1,), (1,)), ((), ())), preferred_element_type=F32)
        p = jnp.where(causal_c, s, 0.0).astype(BF16)
        o_intra[ci] = jnp.dot(p, v_diag, preferred_element_type=F32)

    o = [None] * n_chunks
    for step in range(n_chunks):
        ci = step if z == 0 else n_chunks - 1 - step
        cr = slice(ci * c, (ci + 1) * c)
        o[ci] = o_intra[ci] + _bdot_nt(q_bf[cr], st)
        st = st * dec[ci * c:ci * c + 1, :] + jnp.where(bd, _bdot_tn(v[cr], k_e[cr]), 0.0)
    return jnp.concatenate(o, axis=0), st


def _head_rms(o, g):
    hb = (lax.broadcasted_iota(jnp.int32, (GROUP_WIDTH, GROUP_WIDTH), 0) // DV_GLA
          == lax.broadcasted_iota(jnp.int32, (GROUP_WIDTH, GROUP_WIDTH), 1) // DV_GLA)
    ms = _split_dot_rhs(o * o, hb.astype(BF16)) * (1.0 / DV_GLA)
    return o * lax.rsqrt(ms + EPS) * g


def _split_dot_rhs(x, m):
    hi = x.astype(BF16)
    lo = (x - hi.astype(F32)).astype(BF16)
    return jnp.dot(hi, m, preferred_element_type=F32) + jnp.dot(lo, m, preferred_element_type=F32)


def _expand_state(s):
    hk = H_GLA * DK_GLA
    bd = (lax.broadcasted_iota(jnp.int32, (GROUP_WIDTH, hk), 0) // DV_GLA
          == lax.broadcasted_iota(jnp.int32, (GROUP_WIDTH, hk), 1) // DK_GLA)
    return jnp.where(bd, jnp.concatenate([s] * H_GLA, axis=0), 0.0)


def _compact_state(st):
    out = st[0:DV_GLA]
    for h in range(1, H_GLA):
        out = out + st[h * DV_GLA:(h + 1) * DV_GLA]
    return out


def _gla_ctx_kernel(names, *refs):
    r = dict(zip(names, refs))
    hk = H_GLA * DK_GLA
    for bb in range(NB_CTX):
        rows = slice(bb * SEQ, (bb + 1) * SEQ)
        zero = jnp.zeros((GROUP_WIDTH, hk), F32)
        o_f, st_f = _gla_group(r, rows, 0, zero)
        o_b, st_b = _gla_group(r, rows, 1, zero)
        r['o'][rows, :] = _head_rms(o_f + o_b, r['mg'][...])
        r['sf'][bb, 0] = _compact_state(st_f)
        r['sf'][bb, 1] = _compact_state(st_b)


def _gla_lat_kernel(names, *refs):
    r = dict(zip(names, refs))
    n_groups = DEC_SEQ // ROWS_GROUP

    def run(z):
        r['st'][...] = _expand_state(r['s0'][z])

        def body(step, carry):
            gi = step if z == 0 else n_groups - 1 - step
            rows = pl.ds(pl.multiple_of(gi * ROWS_GROUP, ROWS_GROUP), ROWS_GROUP)
            o, st = _gla_group(r, rows, z, r['st'][...])
            r['st'][...] = st
            if z == 0:
                r['o'][rows, :] = o
            else:
                r['o'][rows, :] = _head_rms(r['o'][rows, :] + o, r['mg'][...])
            return carry

        lax.fori_loop(0, n_groups, body, 0)

    run(0)
    run(1)


def _gla(u, prev, *, latent, gate_w, gate_b, mix_g_d, s0=None):
    n = DEC_SEQ if latent else NB_CTX * SEQ
    steps = DEC_BATCH if latent else BATCH // NB_CTX
    row0 = N_CTX // n if latent else 0
    hk = H_GLA * DK_GLA
    const2 = lambda b: (0, 0)
    names = ['q', 'k', 'v', 'gl', 'gw', 'gb', 'mg', 'masks']
    args = [u, u, u, u, gate_w, gate_b, mix_g_d, _gla_masks()]
    in_specs = [
        pl.BlockSpec((n, hk), lambda b: (row0 + b, COL_QD // hk)),
        pl.BlockSpec((n, hk), lambda b: (row0 + b, COL_KD // hk)),
        pl.BlockSpec((n, GROUP_WIDTH), lambda b: (row0 + b, COL_VD // GROUP_WIDTH)),
        pl.BlockSpec((n, 128), lambda b: (row0 + b, COL_GD // 128)),
        pl.BlockSpec((2, 128, hk), lambda b: (0, 0, 0)),
        pl.BlockSpec((2, hk), const2),
        pl.BlockSpec((1, GROUP_WIDTH), const2),
        pl.BlockSpec((3, ROWS_GROUP, ROWS_GROUP), lambda b: (0, 0, 0)),
    ]
    out_shape = jax.ShapeDtypeStruct((N_TOK, GROUP_WIDTH), F32)
    out_specs = pl.BlockSpec((n, GROUP_WIDTH), lambda b: (row0 + b, 0))
    aliases, scratch = {}, []
    if latent:
        names.append('s0'); args.append(s0)
        in_specs.append(pl.BlockSpec((None, 2, DV_GLA, hk), lambda b: (b, 0, 0, 0)))
        aliases = {len(args): 0}
        names.append('prev'); args.append(prev); in_specs.append(_any_spec())
        names += ['o', 'st']
        scratch = [pltpu.VMEM((GROUP_WIDTH, hk), F32)]
        body = functools.partial(_gla_lat_kernel, tuple(names))
    else:
        names += ['o', 'sf']
        out_shape = (out_shape, jax.ShapeDtypeStruct((BATCH, 2, DV_GLA, hk), F32))
        out_specs = (out_specs, pl.BlockSpec((NB_CTX, 2, DV_GLA, hk), lambda b: (b, 0, 0, 0)))
        body = functools.partial(_gla_ctx_kernel, tuple(names))
    return pl.pallas_call(
        body, grid=(steps,), in_specs=in_specs, out_specs=out_specs, out_shape=out_shape,
        scratch_shapes=scratch, input_output_aliases=aliases,
        compiler_params=pltpu.CompilerParams(dimension_semantics=("parallel",)),
        name="gla_latent" if latent else "gla_context",
    )(*args)


def _merge_kernel(h_ref, oa_ref, ob_ref, oc_ref, od_ref, rd_ref, gt_ref, mg_ref, wo_ref, o_ref):
    gw = GROUP_WIDTH

    def gain(g, order):
        return jnp.concatenate([mg_ref[:, g * gw + h * HEAD_DIM:g * gw + (h + 1) * HEAD_DIM] for h in order], axis=1)

    def weight(g, order):
        return jnp.concatenate([wo_ref[g * gw + h * HEAD_DIM:g * gw + (h + 1) * HEAD_DIM, :] for h in order], axis=0)

    natural = range(N_HEADS)
    ys = [
        _rms(oa_ref[...], gain(0, HEAD_ORDER)),
        _rms(ob_ref[...], gain(1, natural)),
        _rms(oc_ref[...], gain(2, HEAD_ORDER)),
        od_ref[...] * _silu(rd_ref[:, COL_RD - COL_GD:COL_RD - COL_GD + gw]),
    ]
    orders = [HEAD_ORDER, natural, HEAD_ORDER, natural]
    mix = jnp.dot(ys[0].astype(BF16), weight(0, orders[0]), preferred_element_type=F32)
    for j in range(1, 4):
        mix = mix + jnp.dot(ys[j].astype(BF16), weight(j, orders[j]), preferred_element_type=F32)
    o_ref[...] = h_ref[...] + gt_ref[...] * mix


def _merge(h, oa, ob, oc, od, u, mod, mix_g, w_out, layer):
    tm = TM_PROJ
    grp = pl.BlockSpec((tm, GROUP_WIDTH), lambda i: (i, 0))
    return pl.pallas_call(
        _merge_kernel,
        grid=(N_TOK // tm,),
        in_specs=[
            pl.BlockSpec((tm, D_MODEL), lambda i: (i, 0)),
            grp, grp, grp, grp,
            pl.BlockSpec((tm, COL_RD_BLOCK), lambda i: (i, COL_GD // COL_RD_BLOCK)),
            _mod_spec(tm, 5, 1),
            pl.BlockSpec((None, 1, D_MODEL), lambda i: (layer, 0, 0)),
            pl.BlockSpec((None, D_MODEL, D_MODEL), lambda i: (layer, 0, 0)),
        ],
        out_specs=pl.BlockSpec((tm, D_MODEL), lambda i: (i, 0)),
        out_shape=jax.ShapeDtypeStruct((N_TOK, D_MODEL), F32),
        compiler_params=pltpu.CompilerParams(dimension_semantics=("parallel",)),
        name="merge",
    )(h, oa, ob, oc, od, u, mod, mix_g, w_out)


def _gate_weights(gate_w):
    out = jnp.zeros((2, 128, H_GLA * DK_GLA), gate_w.dtype)
    out = out.at[0, 0:GLA_RANK].set(gate_w[0])
    return out.at[1, GLA_RANK:2 * GLA_RANK].set(gate_w[1])


def _states_to_kernel(st):
    b = st.shape[0]
    return st.transpose(0, 1, 4, 2, 3).reshape(b, 2, DV_GLA, H_GLA * DK_GLA)


def _states_from_kernel(st):
    b = st.shape[0]
    return st.reshape(b, 2, DV_GLA, H_GLA, DK_GLA).transpose(0, 1, 3, 4, 2)


def kernel(x_prompt, x_sample, cache_swa_k, cache_swa_v, cache_gqa_k, cache_gqa_v, state_gla, c, c_ctx, w_mod, b_mod, norm_g, ffn_w_in, ffn_w_out, w_in, w_out, mix_g, swa_sink, qk_norm_g, hy_conv_w, hy_conv_b, hy_w1, hy_b1, hy_w2, hy_b2, hy_w3, hy_freq, hy_bias, gla_gate_w, gla_gate_b, final_g):
    cond = jnp.zeros((MOD_ROWS, D_MODEL), F32).at[0].set(c_ctx).at[1:1 + DEC_BATCH].set(c)
    mod_all = _modulation(cond, w_mod, b_mod).reshape(DEPTH, MOD_ROWS, N_MOD, 1, D_MODEL)

    dft = {n: tuple(jnp.asarray(m).astype(BF16) for m in _dft_matrices(n)) for n in (SEQ, DEC_SEQ)}

    xs = [x_prompt.reshape(N_CTX, D_MODEL), x_sample.reshape(N_LAT, D_MODEL)]
    ffn_in = ffn_w_in.astype(BF16)
    ffn_out = ffn_w_out.astype(BF16)
    ng = norm_g.reshape(DEPTH, 3, 1, D_MODEL)
    proj_w = jnp.pad(w_in, ((0, 0), (0, 0), (0, PROJ_WIDTH - REF_END))).astype(BF16)
    out_w = w_out.astype(BF16)
    mix_gp = mix_g.reshape(DEPTH, 1, D_MODEL)
    cache_shape = (DEC_BATCH, DEPTH, PAST_LEN, KV_WIDTH)
    ck_a, cv_a = cache_swa_k.reshape(cache_shape), cache_swa_v.reshape(cache_shape)
    ck_c, cv_c = cache_gqa_k.reshape(cache_shape), cache_gqa_v.reshape(cache_shape)
    cache_a = cache_c = None
    sts = []
    for l in range(DEPTH):
        mod = mod_all[l]
        h = _ffn(xs, mod, 0, ng, ffn_in, ffn_out, l, 0)
        u = _in_proj(h, mod, ng, proj_w, l)

        qkg2 = jnp.tile(qk_norm_g[l], (1, N_KV))

        oa, *cache_a = _attention_context(u, l, cache_a, col_q=COL_QA, col_k=COL_KA, col_v=COL_VA,
                                          sink=swa_sink[l])
        oa = _attention_latent(u, oa, l, col_q=COL_QA, col_k=COL_KA, col_v=COL_VA, cache_k=ck_a, cache_v=cv_a,
                               window=True, sink=swa_sink[l])
        oc, *cache_c = _attention_context(u, l, cache_c, col_q=COL_QC, col_k=COL_KC, col_v=COL_VC, qkg2=qkg2)
        oc = _attention_latent(u, oc, l, col_q=COL_QC, col_k=COL_KC, col_v=COL_VC, cache_k=ck_c, cache_v=cv_c,
                               qkg2=qkg2)

        w1p = jnp.zeros((128, HY_FH), F32).at[:HY_EMB].set(hy_w1[l])
        ob = None
        for latent, n in ((False, SEQ), (True, DEC_SEQ)):
            fwd, inv = dft[n]
            kr, ki = _hyena_filters(n, fwd, w1p, hy_b1[l], hy_w2[l], hy_b2[l], hy_w3[l], hy_freq[l])
            ob = _hyena(u, ob, latent=latent, fwd=fwd, inv=inv, kr=kr, ki=ki, conv_w=hy_conv_w[l],
                        conv_b=hy_conv_b[l], bias=hy_bias[l])

        gw = _gate_weights(gla_gate_w[l]).astype(BF16)
        mg_d = mix_g[l, 3 * GROUP_WIDTH:].reshape(1, GROUP_WIDTH)
        od, st = _gla(u, None, latent=False, gate_w=gw, gate_b=gla_gate_b[l], mix_g_d=mg_d)
        od = _gla(u, od, latent=True, gate_w=gw, gate_b=gla_gate_b[l], mix_g_d=mg_d,
                  s0=_states_to_kernel(state_gla[:, l]))

        h = _merge(h, oa, ob, oc, od, u, mod, mix_gp, out_w, l)
        xs = [_ffn([h], mod, 6, ng, ffn_in, ffn_out, l, 1)] if l + 1 < DEPTH else None
        if xs is None:
            y_prompt, y_sample = _ffn([h], mod, 6, ng, ffn_in, ffn_out, l, 1, final_g=final_g)
        sts.append(_states_from_kernel(st))

    y_prompt = y_prompt.reshape(BATCH, SEQ, D_MODEL)
    y_sample = y_sample.reshape(DEC_BATCH, DEC_SEQ, D_MODEL)
    caches = [x.reshape(BATCH, DEPTH, SEQ, N_KV, HEAD_DIM) for x in (*cache_a, *cache_c)]
    return (y_prompt, y_sample, *caches, jnp.stack(sts, axis=1))
```

```python
import functools
import math

import numpy as np
import jax
import jax.numpy as jnp
from jax import lax
from jax.experimental import pallas as pl
from jax.experimental.pallas import tpu as pltpu

F32 = jnp.float32
BF16 = jnp.bfloat16

D_MODEL = 1024
BATCH = 32
SEQ = 256
DEPTH = 2
DEC_BATCH = 2
DEC_SEQ = 1024
PAST_LEN = 512
GRID_W = 64
HEAD_DIM = 64
GROUP_WIDTH = 256
N_HEADS = 4
N_KV = 2
KV_WIDTH = N_KV * HEAD_DIM
WINDOW = 128
HY_CH = 256
HY_ORDER = 2
HY_BANDS = 16
HY_EMB = 2 * HY_BANDS + 1
HY_FH = 64
HY_TARGET = 1e-2
HY_FAST = 0.3
HY_SLOW = 1.5
H_GLA = 4
DV_GLA = 64
DK_GLA = 32
GLA_RANK = 16
GLA_CHUNK = 64
GLA_NORM = 16.0
D_FF = 2816
N_MOD = 9
ROPE_THETA = 10000.0
EPS = 1e-6

N_CTX = BATCH * SEQ
N_LAT = DEC_BATCH * DEC_SEQ
N_TOK = N_CTX + N_LAT
MOD_ROWS = 8

COL_HY, COL_QA, COL_KA, COL_VA = 0, 768, 1024, 1152
COL_QC, COL_KC, COL_VC = 1280, 1536, 1664
COL_QD, COL_KD, COL_VD, COL_GD, COL_RD = 1792, 1920, 2048, 2304, 2336
PROJ_WIDTH = 2688
REF_QA, REF_KA, REF_HY, REF_QC, REF_KC, REF_END = 0, 256, 512, 1280, 1536, 2592
COL_RD_BLOCK = 384
HEAD_ORDER = (0, 2, 1, 3)
_HEAD_PERM = np.concatenate([np.arange(HEAD_DIM) + HEAD_DIM * h for h in HEAD_ORDER])

TM_FFN = 1024
TK_FFN = 256
FFN_VMEM_BYTES = 60 * 1024 * 1024
TM_PROJ = 1024
TQ_ATTN = 256
ROWS_GROUP = 256
NB_CTX = 4
NB_CTX_NORMED = 2
NB_HYENA = 4


def _bdot(a, b):
    return jnp.dot(a.astype(BF16), b.astype(BF16), preferred_element_type=F32)


def _bdot_nt(a, b):
    return lax.dot_general(a.astype(BF16), b.astype(BF16), (((1,), (1,)), ((), ())),
                           preferred_element_type=F32)


def _bdot_tn(a, b):
    return lax.dot_general(a.astype(BF16), b.astype(BF16), (((0,), (0,)), ((), ())),
                           preferred_element_type=F32)


def _fdot(a, b):
    a_hi, b_hi = a.astype(BF16), b.astype(BF16)
    a_lo = (a - a_hi.astype(F32)).astype(BF16)
    b_lo = (b - b_hi.astype(F32)).astype(BF16)
    dot = functools.partial(jnp.dot, preferred_element_type=F32)
    return dot(a_hi, b_hi) + (dot(a_hi, b_lo) + dot(a_lo, b_hi) + dot(a_lo, b_lo))


def _split_dot(m, x):
    hi = x.astype(BF16)
    lo = (x - hi.astype(F32)).astype(BF16)
    return jnp.dot(m, hi, preferred_element_type=F32) + jnp.dot(m, lo, preferred_element_type=F32)


def _rms(x, g):
    return x * lax.rsqrt(jnp.mean(x * x, axis=-1, keepdims=True) + EPS) * g


def _silu(x):
    return x * jax.nn.sigmoid(x)


def _mod_row(i, tm):
    n_ctx_tiles = N_CTX // tm
    per_batch = DEC_SEQ // tm
    return jnp.where(i < n_ctx_tiles, 0, 1 + (i - n_ctx_tiles) // per_batch)


def _mod_spec(tm, j, grid_rank):
    if grid_rank == 1:
        return pl.BlockSpec((None, None, 1, D_MODEL), lambda i: (_mod_row(i, tm), j, 0, 0))
    return pl.BlockSpec((None, None, 1, D_MODEL), lambda i, k: (_mod_row(i, tm), j, 0, 0))


def _any_spec():
    return pl.BlockSpec(memory_space=pl.ANY)


def _mod_kernel(cond_ref, w_ref, b_ref, o_ref):
    o_ref[...] = _bdot(_silu(cond_ref[...]), w_ref[...]) + b_ref[...]


def _modulation(cond, w_mod, b_mod):
    tn = 3 * D_MODEL
    return pl.pallas_call(
        _mod_kernel,
        grid=(DEPTH, N_MOD * D_MODEL // tn),
        in_specs=[
            pl.BlockSpec((MOD_ROWS, D_MODEL), lambda l, j: (0, 0)),
            pl.BlockSpec((None, D_MODEL, tn), lambda l, j: (l, 0, j)),
            pl.BlockSpec((None, 1, tn), lambda l, j: (l, 0, j)),
        ],
        out_specs=pl.BlockSpec((None, MOD_ROWS, tn), lambda l, j: (l, 0, j)),
        out_shape=jax.ShapeDtypeStruct((DEPTH, MOD_ROWS, N_MOD * D_MODEL), F32),
        name="modulation",
    )(cond, w_mod, b_mod.reshape(DEPTH, 1, N_MOD * D_MODEL))


def _ffn_kernel(split_input, final, *refs):
    is_ctx = pl.program_id(0) < N_CTX // TM_FFN
    if split_input:
        xa_ref, xb_ref, *refs = refs
        x = jnp.where(is_ctx, xa_ref[...], xb_ref[...])
    else:
        x_ref, *refs = refs
        x = x_ref[...]
    if final:
        sh_ref, sc_ref, gt_ref, ng_ref, wi_ref, wo_ref, fg_ref, yc_ref, yl_ref, xm_ref, act_ref = refs
    else:
        sh_ref, sc_ref, gt_ref, ng_ref, wi_ref, wo_ref, o_ref, xm_ref, act_ref = refs
    xn = _rms(x, ng_ref[...])
    xm_ref[...] = (xn * (1.0 + sc_ref[...]) + sh_ref[...]).astype(BF16)
    for c0 in range(0, D_FF, TK_FFN):
        c1 = min(c0 + TK_FFN, D_FF)
        xm = xm_ref[...]
        a = jnp.dot(xm, wi_ref[:, c0:c1], preferred_element_type=F32)
        b = jnp.dot(xm, wi_ref[:, D_FF + c0:D_FF + c1], preferred_element_type=F32)
        act_ref[:, c0:c1] = (_silu(a) * b).astype(BF16)
    y = jnp.dot(act_ref[...], wo_ref[...], preferred_element_type=F32)
    out = x + 0.5 * gt_ref[...] * y
    if not final:
        o_ref[...] = out
        return
    out = _rms(out, fg_ref[...])

    @pl.when(is_ctx)
    def _():
        yc_ref[...] = out

    @pl.when(jnp.logical_not(is_ctx))
    def _():
        yl_ref[...] = out


def _ffn(xs, mod, j0, ng, w_in, w_out, layer, which, final_g=None):
    tm = TM_FFN
    n_ctx_tiles = N_CTX // tm
    split_specs = [pl.BlockSpec((tm, D_MODEL), lambda i: (jnp.minimum(i, n_ctx_tiles - 1), 0)),
                   pl.BlockSpec((tm, D_MODEL), lambda i: (jnp.maximum(i - n_ctx_tiles, 0), 0))]
    whole_spec = pl.BlockSpec((tm, D_MODEL), lambda i: (i, 0))
    x_specs = split_specs if len(xs) == 2 else [whole_spec]
    resident = dict(pipeline_mode=pl.Buffered(1))
    final = final_g is not None
    extra_args, extra_specs = [], []
    if final:
        extra_args = [final_g.reshape(1, D_MODEL)]
        extra_specs = [pl.BlockSpec((1, D_MODEL), lambda i: (0, 0))]
        out_specs = tuple(split_specs)
        out_shape = (jax.ShapeDtypeStruct((N_CTX, D_MODEL), F32), jax.ShapeDtypeStruct((N_LAT, D_MODEL), F32))
    else:
        out_specs = whole_spec
        out_shape = jax.ShapeDtypeStruct((N_TOK, D_MODEL), F32)
    return pl.pallas_call(
        functools.partial(_ffn_kernel, len(xs) == 2, final),
        grid=(N_TOK // tm,),
        in_specs=x_specs + [
            _mod_spec(tm, j0, 1), _mod_spec(tm, j0 + 1, 1), _mod_spec(tm, j0 + 2, 1),
            pl.BlockSpec((None, None, 1, D_MODEL), lambda i: (layer, j0 // 3, 0, 0)),
            pl.BlockSpec((None, None, D_MODEL, 2 * D_FF), lambda i: (layer, which, 0, 0), **resident),
            pl.BlockSpec((None, None, D_FF, D_MODEL), lambda i: (layer, which, 0, 0), **resident),
        ] + extra_specs,
        out_specs=out_specs,
        out_shape=out_shape,
        scratch_shapes=[pltpu.VMEM((tm, D_MODEL), BF16), pltpu.VMEM((tm, D_FF), BF16)],
        compiler_params=pltpu.CompilerParams(dimension_semantics=("arbitrary",),
                                             vmem_limit_bytes=FFN_VMEM_BYTES),
        name="ffn",
    )(*xs, mod, mod, mod, ng, w_in, w_out, *extra_args)


def _proj_kernel(x_ref, sh_ref, sc_ref, ng_ref, w_ref, o_ref):
    xn = _rms(x_ref[...], ng_ref[...])
    xm = (xn * (1.0 + sc_ref[...]) + sh_ref[...]).astype(BF16)

    def put(col, w):
        o_ref[:, col:col + w.shape[1]] = jnp.dot(xm, w, preferred_element_type=F32)

    def queries(base):
        return jnp.concatenate([w_ref[:, base + HEAD_DIM * h:base + HEAD_DIM * (h + 1)] for h in HEAD_ORDER], axis=1)

    put(COL_HY, w_ref[:, REF_HY:REF_QC])
    put(COL_QA, queries(REF_QA))
    put(COL_KA, w_ref[:, REF_KA:REF_HY])
    put(COL_QC, queries(REF_QC))
    put(COL_KC, w_ref[:, REF_KC:COL_GD])
    put(COL_GD, w_ref[:, COL_GD:REF_END])
    o_ref[:, REF_END:PROJ_WIDTH] = jnp.zeros((x_ref.shape[0], PROJ_WIDTH - REF_END), F32)


def _in_proj(h, mod, ng, w, layer):
    tm = TM_PROJ
    return pl.pallas_call(
        _proj_kernel,
        grid=(N_TOK // tm,),
        in_specs=[
            pl.BlockSpec((tm, D_MODEL), lambda i: (i, 0)),
            _mod_spec(tm, 3, 1), _mod_spec(tm, 4, 1),
            pl.BlockSpec((None, None, 1, D_MODEL), lambda i: (layer, 1, 0, 0)),
            pl.BlockSpec((None, D_MODEL, PROJ_WIDTH), lambda i: (layer, 0, 0)),
        ],
        out_specs=pl.BlockSpec((tm, PROJ_WIDTH), lambda i: (i, 0)),
        out_shape=jax.ShapeDtypeStruct((N_TOK, PROJ_WIDTH), F32),
        compiler_params=pltpu.CompilerParams(dimension_semantics=("parallel",)),
        name="in_proj",
    )(h, mod, mod, ng, w)


def _rope_tables(length):
    rows = length // GRID_W
    r = np.repeat(np.arange(rows, dtype=np.float32), GRID_W)
    col = np.tile(np.arange(GRID_W, dtype=np.float32), rows)
    nf = HEAD_DIM // 4
    inv = (np.float32(ROPE_THETA) ** (-np.arange(nf, dtype=np.float32) / nf)).astype(np.float32)
    ang = np.concatenate([r[:, None] * inv, col[:, None] * inv], axis=-1).astype(np.float32)
    cos, sin = np.cos(ang).astype(np.float32), np.sin(ang).astype(np.float32)
    return np.tile(cos, (1, 4)), np.tile(np.concatenate([-sin, sin], axis=-1), (1, 2))


def _pair_lanes(rows):
    lane = lax.broadcasted_iota(jnp.int32, (rows, 2 * HEAD_DIM), 1)
    return lane < HEAD_DIM, (lane % HEAD_DIM) < HEAD_DIM // 2


def _rope_pair(x, cos, sin_signed, first_half):
    partner = jnp.where(first_half, pltpu.roll(x, 3 * HEAD_DIM // 2, 1), pltpu.roll(x, HEAD_DIM // 2, 1))
    return x * cos + partner * sin_signed


def _rms_pair(x, g, lo):
    x2 = x * x
    s_lo = jnp.sum(jnp.where(lo, x2, 0.0), axis=-1, keepdims=True)
    s_hi = jnp.sum(jnp.where(lo, 0.0, x2), axis=-1, keepdims=True)
    ms = jnp.where(lo, s_lo, s_hi) * (1.0 / HEAD_DIM)
    return x * lax.rsqrt(ms + EPS) * g


def _attn_core(q, k_bf, v_bf, lo, *, mask=None, ctx=None, sinks=None):
    tq = q.shape[0]
    pair = 2 * HEAD_DIM
    scale = HEAD_DIM ** -0.5
    qa, qb = q[:, :pair] * scale, q[:, pair:] * scale
    qs = jnp.concatenate([jnp.where(lo, qa, 0.0), jnp.where(lo, 0.0, qa),
                          jnp.where(lo, qb, 0.0), jnp.where(lo, 0.0, qb)], axis=0).astype(BF16)
    s = _bdot_nt(qs, k_bf)
    if ctx is not None:
        s_c = _bdot_nt(qs, ctx[0])
    ps, pcs, dens = [], [], []
    for j in range(N_HEADS):
        rows = slice(j * tq, (j + 1) * tq)
        sj = s[rows]
        if mask is not None:
            sj = jnp.where(mask, sj, -1e30)
        m = jnp.max(sj, axis=-1, keepdims=True)
        if ctx is not None:
            m = jnp.maximum(m, jnp.max(s_c[rows], axis=-1, keepdims=True))
        if sinks is not None:
            m = jnp.maximum(m, sinks[j])
        p = jnp.exp(sj - m)
        den = jnp.sum(p, axis=-1, keepdims=True)
        ps.append(p.astype(BF16))
        if ctx is not None:
            pc = jnp.exp(s_c[rows] - m)
            den = den + jnp.sum(pc, axis=-1, keepdims=True)
            pcs.append(pc.astype(BF16))
        if sinks is not None:
            den = den + jnp.exp(sinks[j] - m)
        dens.append(den)
    r = jnp.dot(jnp.concatenate(ps, axis=0), v_bf, preferred_element_type=F32)
    if ctx is not None:
        r = r + jnp.dot(jnp.concatenate(pcs, axis=0), ctx[1], preferred_element_type=F32)
    o = [r[j * tq:(j + 1) * tq] / dens[j] for j in range(N_HEADS)]
    return jnp.concatenate([jnp.where(lo, o[0], o[1]), jnp.where(lo, o[2], o[3])], axis=1)


def _attn_ctx_kernel(names, nb, *refs):
    r = dict(zip(names, refs))
    lo, _ = _pair_lanes(SEQ)
    sinks = [r['sink'][h] for h in HEAD_ORDER] if 'sink' in r else None
    for bb in range(nb):
        rows = slice(bb * SEQ, (bb + 1) * SEQ)
        q, k, v = r['q'][rows, :], r['k'][rows, :], r['v'][rows, :]
        if 'qkg' in r:
            gq, gk = r['qkg'][0:1, :], r['qkg'][1:2, :]
            q = jnp.concatenate([_rms_pair(q[:, :KV_WIDTH], gq, lo), _rms_pair(q[:, KV_WIDTH:], gq, lo)], axis=1)
            k = _rms_pair(k, gk, lo)
        r['k_out'][bb] = k
        r['v_out'][bb] = v
        r['o'][rows, :] = _attn_core(q, k.astype(BF16), v.astype(BF16), lo, sinks=sinks)


def _attn_lat_kernel(names, window, *refs):
    r = dict(zip(names, refs))
    tq = TQ_ATTN
    i = pl.program_id(1)
    lo, first_half = _pair_lanes(tq)
    qk_norm = 'qkg' in r

    @pl.when(i == 0)
    def _():
        lo_k, first_half_k = _pair_lanes(DEC_SEQ)
        k = r['k'][...]
        if qk_norm:
            k = _rms_pair(k, r['qkg'][1:2, :], lo_k)
        r['k_scr'][...] = _rope_pair(k, r['cos_k'][...], r['sin_k'][...], first_half_k).astype(BF16)

    q = r['q'][...]
    halves = []
    for c0 in (0, KV_WIDTH):
        x = q[:, c0:c0 + KV_WIDTH]
        if qk_norm:
            x = _rms_pair(x, r['qkg'][0:1, :], lo)
        halves.append(_rope_pair(x, r['cos_q'][...], r['sin_q'][...], first_half))
    q = jnp.concatenate(halves, axis=1)
    sinks = [r['sink'][h] for h in HEAD_ORDER] if 'sink' in r else None
    ctx = (r['kc'][...].astype(BF16), r['vc'][...].astype(BF16))
    if window:
        span = tq + 2 * WINDOW
        start = pl.multiple_of(jnp.clip(i * tq - WINDOW, 0, DEC_SEQ - span), WINDOW)
        qpos = i * tq + lax.broadcasted_iota(jnp.int32, (tq, span), 0)
        kpos = start + lax.broadcasted_iota(jnp.int32, (tq, span), 1)
        mask = jnp.abs(qpos - kpos) <= WINDOW
        k_bf = r['k_scr'][pl.ds(start, span), :]
        v_bf = r['v'][pl.ds(start, span), :].astype(BF16)
    else:
        mask = None
        k_bf = r['k_scr'][...]
        v_bf = r['v'][...].astype(BF16)
    r['o'][...] = _attn_core(q, k_bf, v_bf, lo, mask=mask, ctx=ctx, sinks=sinks)


def _attention_context(u, layer, cache_prev, *, col_q, col_k, col_v, sink=None, qkg2=None):
    nb = NB_CTX if qkg2 is None else NB_CTX_NORMED
    rows = nb * SEQ
    qb, kb, vb = col_q // GROUP_WIDTH, col_k // KV_WIDTH, col_v // KV_WIDTH
    names = ['q', 'k', 'v']
    args = [u, u, u]
    in_specs = [pl.BlockSpec((rows, GROUP_WIDTH), lambda b: (b, qb)),
                pl.BlockSpec((rows, KV_WIDTH), lambda b: (b, kb)),
                pl.BlockSpec((rows, KV_WIDTH), lambda b: (b, vb))]
    if sink is not None:
        names.append('sink'); args.append(sink)
        in_specs.append(pl.BlockSpec(memory_space=pltpu.SMEM))
    if qkg2 is not None:
        names.append('qkg'); args.append(qkg2)
        in_specs.append(pl.BlockSpec((2, KV_WIDTH), lambda b: (0, 0)))
    aliases = {}
    if cache_prev is not None:
        for j, prev in enumerate(cache_prev):
            aliases[len(args)] = 1 + j
            names.append(f'prev{j}'); args.append(prev); in_specs.append(_any_spec())
    names += ['o', 'k_out', 'v_out']
    cache_shape = jax.ShapeDtypeStruct((BATCH, DEPTH, SEQ, KV_WIDTH), F32)
    cache_spec = pl.BlockSpec((nb, None, SEQ, KV_WIDTH), lambda b: (b, layer, 0, 0))
    return pl.pallas_call(
        functools.partial(_attn_ctx_kernel, tuple(names), nb),
        grid=(BATCH // nb,), in_specs=in_specs,
        out_specs=(pl.BlockSpec((rows, GROUP_WIDTH), lambda b: (b, 0)), cache_spec, cache_spec),
        out_shape=(jax.ShapeDtypeStruct((N_TOK, GROUP_WIDTH), F32), cache_shape, cache_shape),
        input_output_aliases=aliases,
        compiler_params=pltpu.CompilerParams(dimension_semantics=("parallel",)),
        name="attn_context",
    )(*args)


def _attention_latent(u, prev, layer, *, col_q, col_k, col_v, cache_k, cache_v, window=False, sink=None,
                      qkg2=None):
    tq = TQ_ATTN
    nq = DEC_SEQ // tq
    row0_q, row0_k = N_CTX // tq, N_CTX // DEC_SEQ
    qb, kb, vb = col_q // GROUP_WIDTH, col_k // KV_WIDTH, col_v // KV_WIDTH
    cos, sin = _rope_tables(DEC_SEQ)
    const = lambda b, i: (0, 0)
    names = ['q', 'k', 'v', 'kc', 'vc', 'cos_q', 'sin_q', 'cos_k', 'sin_k']
    args = [u, u, u, cache_k, cache_v, cos, sin, cos, sin]
    in_specs = [
        pl.BlockSpec((tq, GROUP_WIDTH), lambda b, i: (row0_q + b * nq + i, qb)),
        pl.BlockSpec((DEC_SEQ, KV_WIDTH), lambda b, i: (row0_k + b, kb)),
        pl.BlockSpec((DEC_SEQ, KV_WIDTH), lambda b, i: (row0_k + b, vb)),
        pl.BlockSpec((None, None, PAST_LEN, KV_WIDTH), lambda b, i: (b, layer, 0, 0)),
        pl.BlockSpec((None, None, PAST_LEN, KV_WIDTH), lambda b, i: (b, layer, 0, 0)),
        pl.BlockSpec((tq, KV_WIDTH), lambda b, i: (i, 0)),
        pl.BlockSpec((tq, KV_WIDTH), lambda b, i: (i, 0)),
        pl.BlockSpec((DEC_SEQ, KV_WIDTH), const),
        pl.BlockSpec((DEC_SEQ, KV_WIDTH), const),
    ]
    if sink is not None:
        names.append('sink'); args.append(sink)
        in_specs.append(pl.BlockSpec(memory_space=pltpu.SMEM))
    if qkg2 is not None:
        names.append('qkg'); args.append(qkg2)
        in_specs.append(pl.BlockSpec((2, KV_WIDTH), const))
    aliases = {len(args): 0}
    names.append('prev'); args.append(prev); in_specs.append(_any_spec())
    names += ['o', 'k_scr']
    return pl.pallas_call(
        functools.partial(_attn_lat_kernel, tuple(names), window),
        grid=(DEC_BATCH, nq), in_specs=in_specs,
        out_specs=pl.BlockSpec((tq, GROUP_WIDTH), lambda b, i: (row0_q + b * nq + i, 0)),
        out_shape=jax.ShapeDtypeStruct((N_TOK, GROUP_WIDTH), F32),
        scratch_shapes=[pltpu.VMEM((DEC_SEQ, KV_WIDTH), BF16)],
        input_output_aliases=aliases,
        compiler_params=pltpu.CompilerParams(dimension_semantics=("parallel", "arbitrary")),
        name="attn_latent",
    )(*args)


def _dft_matrices(length):
    n = length
    k = np.arange(n, dtype=np.int64)[:, None]
    s = np.arange(n, dtype=np.int64)[None, :]
    ang = np.pi * ((k * s) % (2 * n)).astype(np.float64) / n
    fwd_cos = np.cos(ang)
    fwd_sin = -np.sin(ang)
    fwd_sin[0, :] = 1.0 - 2.0 * (np.arange(n) % 2)
    fwd = np.concatenate([fwd_cos, fwd_sin], axis=0)
    wk = np.full((n,), 2.0)
    wk[0] = 1.0
    inv_cos = (np.cos(ang) * wk[:, None]).T / (2 * n)
    inv_sin = (-2.0 * np.sin(ang)).T / (2 * n)
    inv_sin[:, 0] = (1.0 - 2.0 * (np.arange(n) % 2)) / (2 * n)
    inv = np.concatenate([inv_cos, inv_sin], axis=1)
    return fwd.astype(np.float32), inv.astype(np.float32)


def _filter_features(length):
    t = np.linspace(0.0, 1.0, length, dtype=np.float32)[:, None]
    w = (np.float32(2.0 * math.pi / length) * np.arange(length, dtype=np.float32))[:, None]
    bands = np.linspace(1e-4, HY_BANDS - 1, HY_BANDS, dtype=np.float32)[None, :]
    z = np.concatenate([t, np.cos(bands * w), -np.sin(bands * w)], axis=-1).astype(np.float32)
    zp = np.zeros((length, 128), np.float32)
    zp[:, :HY_EMB] = z
    deltas = np.linspace(math.log(HY_TARGET) / HY_SLOW, math.log(HY_TARGET) / HY_FAST, HY_CH, dtype=np.float32)
    decay = np.exp(-t * np.abs(deltas)).astype(np.float32)
    return zp, decay


def _filter_kernel(z_ref, w1_ref, b1_ref, w2_ref, b2_ref, w3_ref, fr_ref, dec_ref, f_ref, kr_ref, ki_ref):
    n = z_ref.shape[0]
    h = jnp.sin(fr_ref[0:1, :] * (_fdot(z_ref[...], w1_ref[...]) + b1_ref[...]))
    h = jnp.sin(fr_ref[1:2, :] * (_fdot(h, w2_ref[...]) + b2_ref[...]))
    h = _fdot(h, w3_ref[...])
    dec = dec_ref[...]
    half = HY_ORDER * HY_CH
    pos = jnp.concatenate([h[:, j * HY_CH:(j + 1) * HY_CH] * dec for j in range(HY_ORDER)], axis=-1)
    neg = jnp.concatenate([h[:, half + j * HY_CH:half + (j + 1) * HY_CH] * dec for j in range(HY_ORDER)], axis=-1)
    row = lax.broadcasted_iota(jnp.int32, (n, half), 0)
    neg = jnp.where(row == 0, 0.0, neg)
    fwd = f_ref[...]
    a = jnp.dot(fwd, (pos + neg).astype(BF16), preferred_element_type=F32)
    b = jnp.dot(fwd, (pos - neg).astype(BF16), preferred_element_type=F32)
    kr_ref[...] = a[:n]
    ki_ref[...] = jnp.where(row == 0, a[n:], b[n:])


def _hyena_filters(length, fwd, w1p, b1, w2, b2, w3, freq):
    zp, decay = _filter_features(length)
    half = HY_ORDER * HY_CH
    return pl.pallas_call(
        _filter_kernel,
        out_shape=(jax.ShapeDtypeStruct((length, half), F32), jax.ShapeDtypeStruct((length, half), F32)),
        name="hyena_filters",
    )(zp, w1p, b1.reshape(1, HY_FH), w2, b2.reshape(1, HY_FH), w3, freq, decay, fwd)


def _hyena_kernel(z_ref, cw_ref, cb_ref, f_ref, g_ref, kr_ref, ki_ref, bias_ref, o_ref, *, n):
    nb = z_ref.shape[0] // n
    row = lax.broadcasted_iota(jnp.int32, (n, 3 * HY_CH), 0)
    row0 = lax.broadcasted_iota(jnp.int32, (n, HY_CH), 0) == 0
    vs, x1s, x2s = [], [], []
    for bb in range(nb):
        z = z_ref[bb * n:(bb + 1) * n, :]
        prev = jnp.where(row == 0, 0.0, pltpu.roll(z, 1, 0))
        nxt = jnp.where(row == n - 1, 0.0, pltpu.roll(z, n - 1, 0))
        z = prev * cw_ref[0:1, :] + z * cw_ref[1:2, :] + nxt * cw_ref[2:3, :] + cb_ref[...]
        vs.append(z[:, :HY_CH])
        x1s.append(z[:, HY_CH:2 * HY_CH])
        x2s.append(z[:, 2 * HY_CH:])

    def long_conv(xs, order):
        cols = slice(order * HY_CH, (order + 1) * HY_CH)
        kr, ki = kr_ref[:, cols], ki_ref[:, cols]
        spec = jnp.dot(f_ref[...], jnp.concatenate([x.astype(BF16) for x in xs], axis=1),
                       preferred_element_type=F32)
        prods = []
        for bb in range(nb):
            ur, ui = spec[:n, bb * HY_CH:(bb + 1) * HY_CH], spec[n:, bb * HY_CH:(bb + 1) * HY_CH]
            yr = ur * kr - jnp.where(row0, 0.0, ui * ki)
            yi = jnp.where(row0, ui * ki, ur * ki + ui * kr)
            prods.append(jnp.concatenate([yr, yi], axis=0).astype(BF16))
        y = jnp.dot(g_ref[...], jnp.concatenate(prods, axis=1), preferred_element_type=F32)
        return [y[:, bb * HY_CH:(bb + 1) * HY_CH] + xs[bb] * bias_ref[order:order + 1, :] for bb in range(nb)]

    ys = long_conv(vs, 0)
    ys = long_conv([x1s[bb] * ys[bb] for bb in range(nb)], 1)
    for bb in range(nb):
        o_ref[bb * n:(bb + 1) * n, :] = x2s[bb] * ys[bb]


def _hyena(u, prev, *, latent, fwd, inv, kr, ki, conv_w, conv_b, bias):
    n = DEC_SEQ if latent else SEQ
    nb = DEC_BATCH if latent else NB_HYENA
    steps = 1 if latent else BATCH // nb
    row0 = N_CTX // (nb * n) if latent else 0
    half = HY_ORDER * HY_CH
    const = lambda b: (0, 0)
    args = [u, conv_w, conv_b.reshape(1, 3 * HY_CH), fwd, inv, kr, ki, bias]
    in_specs = [
        pl.BlockSpec((nb * n, 3 * HY_CH), lambda b: (row0 + b, COL_HY // (3 * HY_CH))),
        pl.BlockSpec((3, 3 * HY_CH), const),
        pl.BlockSpec((1, 3 * HY_CH), const),
        pl.BlockSpec((2 * n, n), const),
        pl.BlockSpec((n, 2 * n), const),
        pl.BlockSpec((n, half), const),
        pl.BlockSpec((n, half), const),
        pl.BlockSpec((HY_ORDER, HY_CH), const),
    ]
    aliases = {}
    if latent:
        aliases = {len(args): 0}
        args.append(prev)
        in_specs.append(_any_spec())

    def body(*refs):
        if latent:
            refs = refs[:len(args) - 1] + refs[len(args):]
        _hyena_kernel(*refs, n=n)

    return pl.pallas_call(
        body, grid=(steps,), in_specs=in_specs,
        out_specs=pl.BlockSpec((nb * n, HY_CH), lambda b: (row0 + b, 0)),
        out_shape=jax.ShapeDtypeStruct((N_TOK, HY_CH), F32),
        input_output_aliases=aliases,
        compiler_params=pltpu.CompilerParams(dimension_semantics=("parallel",)),
        name="hyena_latent" if latent else "hyena_context",
    )(*args)


def _gla_masks():
    t = np.arange(ROWS_GROUP)[:, None]
    s = np.arange(ROWS_GROUP)[None, :]
    same = (t // GLA_CHUNK) == (s // GLA_CHUNK)
    return np.stack([same & (s <= t), same & (s >= t), same]).astype(np.float32)


def _gla_group(r, rows, z, st):
    c = GLA_CHUNK
    hk = H_GLA * DK_GLA
    n_chunks = ROWS_GROUP // c
    logits = _bdot(r['gl'][rows, :], r['gw'][z]) + r['gb'][z:z + 1, :]
    g = (jnp.minimum(logits, 0.0) - jnp.log1p(jnp.exp(-jnp.abs(logits)))) / GLA_NORM
    causal = r['masks'][z]
    b = _split_dot(causal.astype(BF16), g)
    tot = _split_dot(r['masks'][2].astype(BF16), g)
    k = r['k'][rows, :]
    q_t = r['q'][rows, :] * jnp.exp(b) * (DK_GLA ** -0.5)
    k_t = (k * jnp.exp(-b)).astype(BF16)
    k_e = (k * jnp.exp(tot - b)).astype(BF16)
    dec = jnp.exp(tot)
    v = r['v'][rows, :].astype(BF16)

    klane = lax.broadcasted_iota(jnp.int32, (c, hk), 1) // DK_GLA
    vlane = lax.broadcasted_iota(jnp.int32, (c, GROUP_WIDTH), 1) // DV_GLA
    causal_c = jnp.concatenate([causal[0:c, 0:c]] * H_GLA, axis=1) != 0.0
    bd = (lax.broadcasted_iota(jnp.int32, (GROUP_WIDTH, hk), 0) // DV_GLA
          == lax.broadcasted_iota(jnp.int32, (GROUP_WIDTH, hk), 1) // DK_GLA)
    q_bf = q_t.astype(BF16)
    zero_k = jnp.zeros((c, hk), BF16)
    zero_v = jnp.zeros((c, GROUP_WIDTH), BF16)
    o_intra = [None] * n_chunks
    for ci in range(n_chunks):
        cr = slice(ci * c, (ci + 1) * c)
        k_stack = jnp.concatenate([jnp.where(klane == h, k_t[cr], zero_k) for h in range(H_GLA)], axis=0)
        v_diag = jnp.concatenate([jnp.where(vlane == h, v[cr], zero_v) for h in range(H_GLA)], axis=0)
        s = lax.dot_general(q_bf[cr], k_stack, (((1,), (1,)), ((), ())), preferred_element_type=F32)
        p = jnp.where(causal_c, s, 0.0).astype(BF16)
        o_intra[ci] = jnp.dot(p, v_diag, preferred_element_type=F32)

    o = [None] * n_chunks
    for step in range(n_chunks):
        ci = step if z == 0 else n_chunks - 1 - step
        cr = slice(ci * c, (ci + 1) * c)
        o[ci] = o_intra[ci] + _bdot_nt(q_bf[cr], st)
        st = st * dec[ci * c:ci * c + 1, :] + jnp.where(bd, _bdot_tn(v[cr], k_e[cr]), 0.0)
    return jnp.concatenate(o, axis=0), st


def _head_rms(o, g):
    hb = (lax.broadcasted_iota(jnp.int32, (GROUP_WIDTH, GROUP_WIDTH), 0) // DV_GLA
          == lax.broadcasted_iota(jnp.int32, (GROUP_WIDTH, GROUP_WIDTH), 1) // DV_GLA)
    ms = _split_dot_rhs(o * o, hb.astype(BF16)) * (1.0 / DV_GLA)
    return o * lax.rsqrt(ms + EPS) * g


def _split_dot_rhs(x, m):
    hi = x.astype(BF16)
    lo = (x - hi.astype(F32)).astype(BF16)
    return jnp.dot(hi, m, preferred_element_type=F32) + jnp.dot(lo, m, preferred_element_type=F32)


def _expand_state(s):
    hk = H_GLA * DK_GLA
    bd = (lax.broadcasted_iota(jnp.int32, (GROUP_WIDTH, hk), 0) // DV_GLA
          == lax.broadcasted_iota(jnp.int32, (GROUP_WIDTH, hk), 1) // DK_GLA)
    return jnp.where(bd, jnp.concatenate([s] * H_GLA, axis=0), 0.0)


def _compact_state(st):
    out = st[0:DV_GLA]
    for h in range(1, H_GLA):
        out = out + st[h * DV_GLA:(h + 1) * DV_GLA]
    return out


def _gla_ctx_kernel(names, *refs):
    r = dict(zip(names, refs))
    hk = H_GLA * DK_GLA
    for bb in range(NB_CTX):
        rows = slice(bb * SEQ, (bb + 1) * SEQ)
        zero = jnp.zeros((GROUP_WIDTH, hk), F32)
        o_f, st_f = _gla_group(r, rows, 0, zero)
        o_b, st_b = _gla_group(r, rows, 1, zero)
        r['o'][rows, :] = _head_rms(o_f + o_b, r['mg'][...])
        r['sf'][bb, 0] = _compact_state(st_f)
        r['sf'][bb, 1] = _compact_state(st_b)


def _gla_lat_kernel(names, *refs):
    r = dict(zip(names, refs))
    n_groups = DEC_SEQ // ROWS_GROUP

    def run(z):
        r['st'][...] = _expand_state(r['s0'][z])

        def body(step, carry):
            gi = step if z == 0 else n_groups - 1 - step
            rows = pl.ds(pl.multiple_of(gi * ROWS_GROUP, ROWS_GROUP), ROWS_GROUP)
            o, st = _gla_group(r, rows, z, r['st'][...])
            r['st'][...] = st
            if z == 0:
                r['o'][rows, :] = o
            else:
                r['o'][rows, :] = _head_rms(r['o'][rows, :] + o, r['mg'][...])
            return carry

        lax.fori_loop(0, n_groups, body, 0)

    run(0)
    run(1)


def _gla(u, prev, *, latent, gate_w, gate_b, mix_g_d, s0=None):
    n = DEC_SEQ if latent else NB_CTX * SEQ
    steps = DEC_BATCH if latent else BATCH // NB_CTX
    row0 = N_CTX // n if latent else 0
    hk = H_GLA * DK_GLA
    const2 = lambda b: (0, 0)
    names = ['q', 'k', 'v', 'gl', 'gw', 'gb', 'mg', 'masks']
    args = [u, u, u, u, gate_w, gate_b, mix_g_d, _gla_masks()]
    in_specs = [
        pl.BlockSpec((n, hk), lambda b: (row0 + b, COL_QD // hk)),
        pl.BlockSpec((n, hk), lambda b: (row0 + b, COL_KD // hk)),
        pl.BlockSpec((n, GROUP_WIDTH), lambda b: (row0 + b, COL_VD // GROUP_WIDTH)),
        pl.BlockSpec((n, 128), lambda b: (row0 + b, COL_GD // 128)),
        pl.BlockSpec((2, 128, hk), lambda b: (0, 0, 0)),
        pl.BlockSpec((2, hk), const2),
        pl.BlockSpec((1, GROUP_WIDTH), const2),
        pl.BlockSpec((3, ROWS_GROUP, ROWS_GROUP), lambda b: (0, 0, 0)),
    ]
    out_shape = jax.ShapeDtypeStruct((N_TOK, GROUP_WIDTH), F32)
    out_specs = pl.BlockSpec((n, GROUP_WIDTH), lambda b: (row0 + b, 0))
    aliases, scratch = {}, []
    if latent:
        names.append('s0'); args.append(s0)
        in_specs.append(pl.BlockSpec((None, 2, DV_GLA, hk), lambda b: (b, 0, 0, 0)))
        aliases = {len(args): 0}
        names.append('prev'); args.append(prev); in_specs.append(_any_spec())
        names += ['o', 'st']
        scratch = [pltpu.VMEM((GROUP_WIDTH, hk), F32)]
        body = functools.partial(_gla_lat_kernel, tuple(names))
    else:
        names += ['o', 'sf']
        out_shape = (out_shape, jax.ShapeDtypeStruct((BATCH, 2, DV_GLA, hk), F32))
        out_specs = (out_specs, pl.BlockSpec((NB_CTX, 2, DV_GLA, hk), lambda b: (b, 0, 0, 0)))
        body = functools.partial(_gla_ctx_kernel, tuple(names))
    return pl.pallas_call(
        body, grid=(steps,), in_specs=in_specs, out_specs=out_specs, out_shape=out_shape,
        scratch_shapes=scratch, input_output_aliases=aliases,
        compiler_params=pltpu.CompilerParams(dimension_semantics=("parallel",)),
        name="gla_latent" if latent else "gla_context",
    )(*args)


def _merge_kernel(h_ref, oa_ref, ob_ref, oc_ref, od_ref, rd_ref, gt_ref, mg_ref, wo_ref, o_ref):
    gw = GROUP_WIDTH

    def gain(g, order):
        return jnp.concatenate([mg_ref[:, g * gw + h * HEAD_DIM:g * gw + (h + 1) * HEAD_DIM] for h in order], axis=1)

    def weight(g, order):
        return jnp.concatenate([wo_ref[g * gw + h * HEAD_DIM:g * gw + (h + 1) * HEAD_DIM, :] for h in order], axis=0)

    natural = range(N_HEADS)
    ys = [
        _rms(oa_ref[...], gain(0, HEAD_ORDER)),
        _rms(ob_ref[...], gain(1, natural)),
        _rms(oc_ref[...], gain(2, HEAD_ORDER)),
        od_ref[...] * _silu(rd_ref[:, COL_RD - COL_GD:COL_RD - COL_GD + gw]),
    ]
    orders = [HEAD_ORDER, natural, HEAD_ORDER, natural]
    mix = jnp.dot(ys[0].astype(BF16), weight(0, orders[0]), preferred_element_type=F32)
    for j in range(1, 4):
        mix = mix + jnp.dot(ys[j].astype(BF16), weight(j, orders[j]), preferred_element_type=F32)
    o_ref[...] = h_ref[...] + gt_ref[...] * mix


def _merge(h, oa, ob, oc, od, u, mod, mix_g, w_out, layer):
    tm = TM_PROJ
    grp = pl.BlockSpec((tm, GROUP_WIDTH), lambda i: (i, 0))
    return pl.pallas_call(
        _merge_kernel,
        grid=(N_TOK // tm,),
        in_specs=[
            pl.BlockSpec((tm, D_MODEL), lambda i: (i, 0)),
            grp, grp, grp, grp,
            pl.BlockSpec((tm, COL_RD_BLOCK), lambda i: (i, COL_GD // COL_RD_BLOCK)),
            _mod_spec(tm, 5, 1),
            pl.BlockSpec((None, 1, D_MODEL), lambda i: (layer, 0, 0)),
            pl.BlockSpec((None, D_MODEL, D_MODEL), lambda i: (layer, 0, 0)),
        ],
        out_specs=pl.BlockSpec((tm, D_MODEL), lambda i: (i, 0)),
        out_shape=jax.ShapeDtypeStruct((N_TOK, D_MODEL), F32),
        compiler_params=pltpu.CompilerParams(dimension_semantics=("parallel",)),
        name="merge",
    )(h, oa, ob, oc, od, u, mod, mix_g, w_out)


def _gate_weights(gate_w):
    out = jnp.zeros((2, 128, H_GLA * DK_GLA), gate_w.dtype)
    out = out.at[0, 0:GLA_RANK].set(gate_w[0])
    return out.at[1, GLA_RANK:2 * GLA_RANK].set(gate_w[1])


def _states_to_kernel(st):
    b = st.shape[0]
    return st.transpose(0, 1, 4, 2, 3).reshape(b, 2, DV_GLA, H_GLA * DK_GLA)


def _states_from_kernel(st):
    b = st.shape[0]
    return st.reshape(b, 2, DV_GLA, H_GLA, DK_GLA).transpose(0, 1, 3, 4, 2)


def kernel(x_prompt, x_sample, cache_swa_k, cache_swa_v, cache_gqa_k, cache_gqa_v, state_gla, c, c_ctx, w_mod, b_mod, norm_g, ffn_w_in, ffn_w_out, w_in, w_out, mix_g, swa_sink, qk_norm_g, hy_conv_w, hy_conv_b, hy_w1, hy_b1, hy_w2, hy_b2, hy_w3, hy_freq, hy_bias, gla_gate_w, gla_gate_b, final_g):
    cond = jnp.zeros((MOD_ROWS, D_MODEL), F32).at[0].set(c_ctx).at[1:1 + DEC_BATCH].set(c)
    mod_all = _modulation(cond, w_mod, b_mod).reshape(DEPTH, MOD_ROWS, N_MOD, 1, D_MODEL)

    dft = {n: tuple(jnp.asarray(m).astype(BF16) for m in _dft_matrices(n)) for n in (SEQ, DEC_SEQ)}

    xs = [x_prompt.reshape(N_CTX, D_MODEL), x_sample.reshape(N_LAT, D_MODEL)]
    ffn_in = ffn_w_in.astype(BF16)
    ffn_out = ffn_w_out.astype(BF16)
    ng = norm_g.reshape(DEPTH, 3, 1, D_MODEL)
    proj_w = jnp.pad(w_in, ((0, 0), (0, 0), (0, PROJ_WIDTH - REF_END))).astype(BF16)
    out_w = w_out.astype(BF16)
    mix_gp = mix_g.reshape(DEPTH, 1, D_MODEL)
    cache_shape = (DEC_BATCH, DEPTH, PAST_LEN, KV_WIDTH)
    ck_a, cv_a = cache_swa_k.reshape(cache_shape), cache_swa_v.reshape(cache_shape)
    ck_c, cv_c = cache_gqa_k.reshape(cache_shape), cache_gqa_v.reshape(cache_shape)
    cache_a = cache_c = None
    sts = []
    for l in range(DEPTH):
        mod = mod_all[l]
        h = _ffn(xs, mod, 0, ng, ffn_in, ffn_out, l, 0)
        u = _in_proj(h, mod, ng, proj_w, l)

        qkg2 = jnp.tile(qk_norm_g[l], (1, N_KV))

        oa, *cache_a = _attention_context(u, l, cache_a, col_q=COL_QA, col_k=COL_KA, col_v=COL_VA,
                                          sink=swa_sink[l])
        oa = _attention_latent(u, oa, l, col_q=COL_QA, col_k=COL_KA, col_v=COL_VA, cache_k=ck_a, cache_v=cv_a,
                               window=True, sink=swa_sink[l])
        oc, *cache_c = _attention_context(u, l, cache_c, col_q=COL_QC, col_k=COL_KC, col_v=COL_VC, qkg2=qkg2)
        oc = _attention_latent(u, oc, l, col_q=COL_QC, col_k=COL_KC, col_v=COL_VC, cache_k=ck_c, cache_v=cv_c,
                               qkg2=qkg2)

        w1p = jnp.zeros((128, HY_FH), F32).at[:HY_EMB].set(hy_w1[l])
        ob = None
        for latent, n in ((False, SEQ), (True, DEC_SEQ)):
            fwd, inv = dft[n]
            kr, ki = _hyena_filters(n, fwd, w1p, hy_b1[l], hy_w2[l], hy_b2[l], hy_w3[l], hy_freq[l])
            ob = _hyena(u, ob, latent=latent, fwd=fwd, inv=inv, kr=kr, ki=ki, conv_w=hy_conv_w[l],
                        conv_b=hy_conv_b[l], bias=hy_bias[l])

        gw = _gate_weights(gla_gate_w[l]).astype(BF16)
        mg_d = mix_g[l, 3 * GROUP_WIDTH:].reshape(1, GROUP_WIDTH)
        od, st = _gla(u, None, latent=False, gate_w=gw, gate_b=gla_gate_b[l], mix_g_d=mg_d)
        od = _gla(u, od, latent=True, gate_w=gw, gate_b=gla_gate_b[l], mix_g_d=mg_d,
                  s0=_states_to_kernel(state_gla[:, l]))

        h = _merge(h, oa, ob, oc, od, u, mod, mix_gp, out_w, l)
        xs = [_ffn([h], mod, 6, ng, ffn_in, ffn_out, l, 1)] if l + 1 < DEPTH else None
        if xs is None:
            y_prompt, y_sample = _ffn([h], mod, 6, ng, ffn_in, ffn_out, l, 1, final_g=final_g)
        sts.append(_states_from_kernel(st))

    y_prompt = y_prompt.reshape(BATCH, SEQ, D_MODEL)
    y_sample = y_sample.reshape(DEC_BATCH, DEC_SEQ, D_MODEL)
    caches = [x.reshape(BATCH, DEPTH, SEQ, N_KV, HEAD_DIM) for x in (*cache_a, *cache_c)]
    return (y_prompt, y_sample, *caches, jnp.stack(sts, axis=1))
```

```python
import functools
import math

import numpy as np
import jax
import jax.numpy as jnp
from jax import lax
from jax.experimental import pallas as pl
from jax.experimental.pallas import tpu as pltpu

F32 = jnp.float32
BF16 = jnp.bfloat16

D_MODEL = 1024
BATCH = 32
SEQ = 256
DEPTH = 2
DEC_BATCH = 2
DEC_SEQ = 1024
PAST_LEN = 512
GRID_W = 64
HEAD_DIM = 64
GROUP_WIDTH = 256
N_HEADS = 4
N_KV = 2
KV_WIDTH = N_KV * HEAD_DIM
WINDOW = 128
HY_CH = 256
HY_ORDER = 2
HY_BANDS = 16
HY_EMB = 2 * HY_BANDS + 1
HY_FH = 64
HY_TARGET = 1e-2
HY_FAST = 0.3
HY_SLOW = 1.5
H_GLA = 4
DV_GLA = 64
DK_GLA = 32
GLA_RANK = 16
GLA_CHUNK = 64
GLA_NORM = 16.0
D_FF = 2816
N_MOD = 9
ROPE_THETA = 10000.0
EPS = 1e-6

N_CTX = BATCH * SEQ
N_LAT = DEC_BATCH * DEC_SEQ
N_TOK = N_CTX + N_LAT
MOD_ROWS = 8

COL_HY, COL_QA, COL_KA, COL_VA = 0, 768, 1024, 1152
COL_QC, COL_KC, COL_VC = 1280, 1536, 1664
COL_QD, COL_KD, COL_VD, COL_GD, COL_RD = 1792, 1920, 2048, 2304, 2336
PROJ_WIDTH = 2688
REF_QA, REF_KA, REF_HY, REF_QC, REF_KC, REF_END = 0, 256, 512, 1280, 1536, 2592
COL_RD_BLOCK = 384
HEAD_ORDER = (0, 2, 1, 3)
_HEAD_PERM = np.concatenate([np.arange(HEAD_DIM) + HEAD_DIM * h for h in HEAD_ORDER])

TM_FFN = 1024
TK_FFN = 256
FFN_VMEM_BYTES = 60 * 1024 * 1024
TM_PROJ = 1024
TQ_ATTN = 256
ROWS_GROUP = 256
NB_CTX = 4
NB_CTX_NORMED = 2
NB_HYENA = 4


def _bdot(a, b):
    return jnp.dot(a.astype(BF16), b.astype(BF16), preferred_element_type=F32)


def _bdot_nt(a, b):
    return lax.dot_general(a.astype(BF16), b.astype(BF16), (((1,), (1,)), ((), ())),
                           preferred_element_type=F32)


def _bdot_tn(a, b):
    return lax.dot_general(a.astype(BF16), b.astype(BF16), (((0,), (0,)), ((), ())),
                           preferred_element_type=F32)


def _fdot(a, b):
    a_hi, b_hi = a.astype(BF16), b.astype(BF16)
    a_lo = (a - a_hi.astype(F32)).astype(BF16)
    b_lo = (b - b_hi.astype(F32)).astype(BF16)
    dot = functools.partial(jnp.dot, preferred_element_type=F32)
    return dot(a_hi, b_hi) + (dot(a_hi, b_lo) + dot(a_lo, b_hi) + dot(a_lo, b_lo))


def _split_dot(m, x):
    hi = x.astype(BF16)
    lo = (x - hi.astype(F32)).astype(BF16)
    return jnp.dot(m, hi, preferred_element_type=F32) + jnp.dot(m, lo, preferred_element_type=F32)


def _rms(x, g):
    return x * lax.rsqrt(jnp.mean(x * x, axis=-1, keepdims=True) + EPS) * g


def _silu(x):
    return x * jax.nn.sigmoid(x)


def _mod_row(i, tm):
    n_ctx_tiles = N_CTX // tm
    per_batch = DEC_SEQ // tm
    return jnp.where(i < n_ctx_tiles, 0, 1 + (i - n_ctx_tiles) // per_batch)


def _mod_spec(tm, j, grid_rank):
    if grid_rank == 1:
        return pl.BlockSpec((None, None, 1, D_MODEL), lambda i: (_mod_row(i, tm), j, 0, 0))
    return pl.BlockSpec((None, None, 1, D_MODEL), lambda i, k: (_mod_row(i, tm), j, 0, 0))


def _any_spec():
    return pl.BlockSpec(memory_space=pl.ANY)


def _mod_kernel(cond_ref, w_ref, b_ref, o_ref):
    o_ref[...] = _bdot(_silu(cond_ref[...]), w_ref[...]) + b_ref[...]


def _modulation(cond, w_mod, b_mod):
    tn = D_MODEL
    return pl.pallas_call(
        _mod_kernel,
        grid=(DEPTH, N_MOD * D_MODEL // tn),
        in_specs=[
            pl.BlockSpec((MOD_ROWS, D_MODEL), lambda l, j: (0, 0)),
            pl.BlockSpec((None, D_MODEL, tn), lambda l, j: (l, 0, j)),
            pl.BlockSpec((None, 1, tn), lambda l, j: (l, 0, j)),
        ],
        out_specs=pl.BlockSpec((None, MOD_ROWS, tn), lambda l, j: (l, 0, j)),
        out_shape=jax.ShapeDtypeStruct((DEPTH, MOD_ROWS, N_MOD * D_MODEL), F32),
        name="modulation",
    )(cond, w_mod, b_mod.reshape(DEPTH, 1, N_MOD * D_MODEL))


def _ffn_kernel(split_input, final, *refs):
    is_ctx = pl.program_id(0) < N_CTX // TM_FFN
    if split_input:
        xa_ref, xb_ref, *refs = refs
        x = jnp.where(is_ctx, xa_ref[...], xb_ref[...])
    else:
        x_ref, *refs = refs
        x = x_ref[...]
    if final:
        sh_ref, sc_ref, gt_ref, ng_ref, wi_ref, wo_ref, fg_ref, yc_ref, yl_ref, xm_ref, act_ref = refs
    else:
        sh_ref, sc_ref, gt_ref, ng_ref, wi_ref, wo_ref, o_ref, xm_ref, act_ref = refs
    xn = _rms(x, ng_ref[...])
    xm_ref[...] = (xn * (1.0 + sc_ref[...]) + sh_ref[...]).astype(BF16)
    for c0 in range(0, D_FF, TK_FFN):
        c1 = min(c0 + TK_FFN, D_FF)
        xm = xm_ref[...]
        a = jnp.dot(xm, wi_ref[:, c0:c1], preferred_element_type=F32)
        b = jnp.dot(xm, wi_ref[:, D_FF + c0:D_FF + c1], preferred_element_type=F32)
        act_ref[:, c0:c1] = (_silu(a) * b).astype(BF16)
    y = jnp.dot(act_ref[...], wo_ref[...], preferred_element_type=F32)
    out = x + 0.5 * gt_ref[...] * y
    if not final:
        o_ref[...] = out
        return
    out = _rms(out, fg_ref[...])

    @pl.when(is_ctx)
    def _():
        yc_ref[...] = out

    @pl.when(jnp.logical_not(is_ctx))
    def _():
        yl_ref[...] = out


def _ffn(xs, mod, j0, ng, w_in, w_out, layer, which, final_g=None):
    tm = TM_FFN
    n_ctx_tiles = N_CTX // tm
    split_specs = [pl.BlockSpec((tm, D_MODEL), lambda i: (jnp.minimum(i, n_ctx_tiles - 1), 0)),
                   pl.BlockSpec((tm, D_MODEL), lambda i: (jnp.maximum(i - n_ctx_tiles, 0), 0))]
    whole_spec = pl.BlockSpec((tm, D_MODEL), lambda i: (i, 0))
    x_specs = split_specs if len(xs) == 2 else [whole_spec]
    resident = dict(pipeline_mode=pl.Buffered(1))
    final = final_g is not None
    extra_args, extra_specs = [], []
    if final:
        extra_args = [final_g.reshape(1, D_MODEL)]
        extra_specs = [pl.BlockSpec((1, D_MODEL), lambda i: (0, 0))]
        out_specs = tuple(split_specs)
        out_shape = (jax.ShapeDtypeStruct((N_CTX, D_MODEL), F32), jax.ShapeDtypeStruct((N_LAT, D_MODEL), F32))
    else:
        out_specs = whole_spec
        out_shape = jax.ShapeDtypeStruct((N_TOK, D_MODEL), F32)
    return pl.pallas_call(
        functools.partial(_ffn_kernel, len(xs) == 2, final),
        grid=(N_TOK // tm,),
        in_specs=x_specs + [
            _mod_spec(tm, j0, 1), _mod_spec(tm, j0 + 1, 1), _mod_spec(tm, j0 + 2, 1),
            pl.BlockSpec((None, None, 1, D_MODEL), lambda i: (layer, j0 // 3, 0, 0)),
            pl.BlockSpec((None, None, D_MODEL, 2 * D_FF), lambda i: (layer, which, 0, 0), **resident),
            pl.BlockSpec((None, None, D_FF, D_MODEL), lambda i: (layer, which, 0, 0), **resident),
        ] + extra_specs,
        out_specs=out_specs,
        out_shape=out_shape,
        scratch_shapes=[pltpu.VMEM((tm, D_MODEL), BF16), pltpu.VMEM((tm, D_FF), BF16)],
        compiler_params=pltpu.CompilerParams(dimension_semantics=("arbitrary",),
                                             vmem_limit_bytes=FFN_VMEM_BYTES),
        name="ffn",
    )(*xs, mod, mod, mod, ng, w_in, w_out, *extra_args)


def _proj_kernel(x_ref, sh_ref, sc_ref, ng_ref, w_ref, o_ref):
    xn = _rms(x_ref[...], ng_ref[...])
    xm = (xn * (1.0 + sc_ref[...]) + sh_ref[...]).astype(BF16)

    def put(col, w):
        o_ref[:, col:col + w.shape[1]] = jnp.dot(xm, w, preferred_element_type=F32)

    def queries(base):
        return jnp.concatenate([w_ref[:, base + HEAD_DIM * h:base + HEAD_DIM * (h + 1)] for h in HEAD_ORDER], axis=1)

    put(COL_HY, w_ref[:, REF_HY:REF_QC])
    put(COL_QA, queries(REF_QA))
    put(COL_KA, w_ref[:, REF_KA:REF_HY])
    put(COL_QC, queries(REF_QC))
    put(COL_KC, w_ref[:, REF_KC:COL_GD])
    put(COL_GD, w_ref[:, COL_GD:REF_END])
    o_ref[:, REF_END:PROJ_WIDTH] = jnp.zeros((x_ref.shape[0], PROJ_WIDTH - REF_END), F32)


def _in_proj(h, mod, ng, w, layer):
    tm = TM_PROJ
    return pl.pallas_call(
        _proj_kernel,
        grid=(N_TOK // tm,),
        in_specs=[
            pl.BlockSpec((tm, D_MODEL), lambda i: (i, 0)),
            _mod_spec(tm, 3, 1), _mod_spec(tm, 4, 1),
            pl.BlockSpec((None, None, 1, D_MODEL), lambda i: (layer, 1, 0, 0)),
            pl.BlockSpec((None, D_MODEL, PROJ_WIDTH), lambda i: (layer, 0, 0)),
        ],
        out_specs=pl.BlockSpec((tm, PROJ_WIDTH), lambda i: (i, 0)),
        out_shape=jax.ShapeDtypeStruct((N_TOK, PROJ_WIDTH), F32),
        compiler_params=pltpu.CompilerParams(dimension_semantics=("parallel",)),
        name="in_proj",
    )(h, mod, mod, ng, w)


def _rope_tables(length):
    rows = length // GRID_W
    r = np.repeat(np.arange(rows, dtype=np.float32), GRID_W)
    col = np.tile(np.arange(GRID_W, dtype=np.float32), rows)
    nf = HEAD_DIM // 4
    inv = (np.float32(ROPE_THETA) ** (-np.arange(nf, dtype=np.float32) / nf)).astype(np.float32)
    ang = np.concatenate([r[:, None] * inv, col[:, None] * inv], axis=-1).astype(np.float32)
    cos, sin = np.cos(ang).astype(np.float32), np.sin(ang).astype(np.float32)
    return np.tile(cos, (1, 4)), np.tile(np.concatenate([-sin, sin], axis=-1), (1, 2))


def _pair_lanes(rows):
    lane = lax.broadcasted_iota(jnp.int32, (rows, 2 * HEAD_DIM), 1)
    return lane < HEAD_DIM, (lane % HEAD_DIM) < HEAD_DIM // 2


def _rope_pair(x, cos, sin_signed, first_half):
    partner = jnp.where(first_half, pltpu.roll(x, 3 * HEAD_DIM // 2, 1), pltpu.roll(x, HEAD_DIM // 2, 1))
    return x * cos + partner * sin_signed


def _rms_pair(x, g, lo):
    x2 = x * x
    s_lo = jnp.sum(jnp.where(lo, x2, 0.0), axis=-1, keepdims=True)
    s_hi = jnp.sum(jnp.where(lo, 0.0, x2), axis=-1, keepdims=True)
    ms = jnp.where(lo, s_lo, s_hi) * (1.0 / HEAD_DIM)
    return x * lax.rsqrt(ms + EPS) * g


def _attn_core(q, k_bf, v_bf, lo, *, mask=None, ctx=None, sinks=None):
    tq = q.shape[0]
    pair = 2 * HEAD_DIM
    scale = HEAD_DIM ** -0.5
    qa, qb = q[:, :pair] * scale, q[:, pair:] * scale
    qs = jnp.concatenate([jnp.where(lo, qa, 0.0), jnp.where(lo, 0.0, qa),
                          jnp.where(lo, qb, 0.0), jnp.where(lo, 0.0, qb)], axis=0).astype(BF16)
    s = _bdot_nt(qs, k_bf)
    if ctx is not None:
        s_c = _bdot_nt(qs, ctx[0])
    ps, pcs, dens = [], [], []
    for j in range(N_HEADS):
        rows = slice(j * tq, (j + 1) * tq)
        sj = s[rows]
        if mask is not None:
            sj = jnp.where(mask, sj, -1e30)
        m = jnp.max(sj, axis=-1, keepdims=True)
        if ctx is not None:
            m = jnp.maximum(m, jnp.max(s_c[rows], axis=-1, keepdims=True))
        if sinks is not None:
            m = jnp.maximum(m, sinks[j])
        p = jnp.exp(sj - m)
        den = jnp.sum(p, axis=-1, keepdims=True)
        ps.append(p.astype(BF16))
        if ctx is not None:
            pc = jnp.exp(s_c[rows] - m)
            den = den + jnp.sum(pc, axis=-1, keepdims=True)
            pcs.append(pc.astype(BF16))
        if sinks is not None:
            den = den + jnp.exp(sinks[j] - m)
        dens.append(den)
    r = jnp.dot(jnp.concatenate(ps, axis=0), v_bf, preferred_element_type=F32)
    if ctx is not None:
        r = r + jnp.dot(jnp.concatenate(pcs, axis=0), ctx[1], preferred_element_type=F32)
    o = [r[j * tq:(j + 1) * tq] / dens[j] for j in range(N_HEADS)]
    return jnp.concatenate([jnp.where(lo, o[0], o[1]), jnp.where(lo, o[2], o[3])], axis=1)


def _attn_ctx_kernel(names, nb, *refs):
    r = dict(zip(names, refs))
    lo, _ = _pair_lanes(SEQ)
    sinks = [r['sink'][h] for h in HEAD_ORDER] if 'sink' in r else None
    for bb in range(nb):
        rows = slice(bb * SEQ, (bb + 1) * SEQ)
        q, k, v = r['q'][rows, :], r['k'][rows, :], r['v'][rows, :]
        if 'qkg' in r:
            gq, gk = r['qkg'][0:1, :], r['qkg'][1:2, :]
            q = jnp.concatenate([_rms_pair(q[:, :KV_WIDTH], gq, lo), _rms_pair(q[:, KV_WIDTH:], gq, lo)], axis=1)
            k = _rms_pair(k, gk, lo)
        r['k_out'][bb] = k
        r['v_out'][bb] = v
        r['o'][rows, :] = _attn_core(q, k.astype(BF16), v.astype(BF16), lo, sinks=sinks)


def _attn_lat_kernel(names, window, *refs):
    r = dict(zip(names, refs))
    tq = TQ_ATTN
    i = pl.program_id(1)
    lo, first_half = _pair_lanes(tq)
    qk_norm = 'qkg' in r

    @pl.when(i == 0)
    def _():
        lo_k, first_half_k = _pair_lanes(DEC_SEQ)
        k = r['k'][...]
        if qk_norm:
            k = _rms_pair(k, r['qkg'][1:2, :], lo_k)
        r['k_scr'][...] = _rope_pair(k, r['cos_k'][...], r['sin_k'][...], first_half_k).astype(BF16)

    q = r['q'][...]
    halves = []
    for c0 in (0, KV_WIDTH):
        x = q[:, c0:c0 + KV_WIDTH]
        if qk_norm:
            x = _rms_pair(x, r['qkg'][0:1, :], lo)
        halves.append(_rope_pair(x, r['cos_q'][...], r['sin_q'][...], first_half))
    q = jnp.concatenate(halves, axis=1)
    sinks = [r['sink'][h] for h in HEAD_ORDER] if 'sink' in r else None
    ctx = (r['kc'][...].astype(BF16), r['vc'][...].astype(BF16))
    if window:
        span = tq + 2 * WINDOW
        start = pl.multiple_of(jnp.clip(i * tq - WINDOW, 0, DEC_SEQ - span), WINDOW)
        qpos = i * tq + lax.broadcasted_iota(jnp.int32, (tq, span), 0)
        kpos = start + lax.broadcasted_iota(jnp.int32, (tq, span), 1)
        mask = jnp.abs(qpos - kpos) <= WINDOW
        k_bf = r['k_scr'][pl.ds(start, span), :]
        v_bf = r['v'][pl.ds(start, span), :].astype(BF16)
    else:
        mask = None
        k_bf = r['k_scr'][...]
        v_bf = r['v'][...].astype(BF16)
    r['o'][...] = _attn_core(q, k_bf, v_bf, lo, mask=mask, ctx=ctx, sinks=sinks)


def _attention_context(u, layer, cache_prev, *, col_q, col_k, col_v, sink=None, qkg2=None):
    nb = NB_CTX if qkg2 is None else NB_CTX_NORMED
    rows = nb * SEQ
    qb, kb, vb = col_q // GROUP_WIDTH, col_k // KV_WIDTH, col_v // KV_WIDTH
    names = ['q', 'k', 'v']
    args = [u, u, u]
    in_specs = [pl.BlockSpec((rows, GROUP_WIDTH), lambda b: (b, qb)),
                pl.BlockSpec((rows, KV_WIDTH), lambda b: (b, kb)),
                pl.BlockSpec((rows, KV_WIDTH), lambda b: (b, vb))]
    if sink is not None:
        names.append('sink'); args.append(sink)
        in_specs.append(pl.BlockSpec(memory_space=pltpu.SMEM))
    if qkg2 is not None:
        names.append('qkg'); args.append(qkg2)
        in_specs.append(pl.BlockSpec((2, KV_WIDTH), lambda b: (0, 0)))
    aliases = {}
    if cache_prev is not None:
        for j, prev in enumerate(cache_prev):
            aliases[len(args)] = 1 + j
            names.append(f'prev{j}'); args.append(prev); in_specs.append(_any_spec())
    names += ['o', 'k_out', 'v_out']
    cache_shape = jax.ShapeDtypeStruct((BATCH, DEPTH, SEQ, KV_WIDTH), F32)
    cache_spec = pl.BlockSpec((nb, None, SEQ, KV_WIDTH), lambda b: (b, layer, 0, 0))
    return pl.pallas_call(
        functools.partial(_attn_ctx_kernel, tuple(names), nb),
        grid=(BATCH // nb,), in_specs=in_specs,
        out_specs=(pl.BlockSpec((rows, GROUP_WIDTH), lambda b: (b, 0)), cache_spec, cache_spec),
        out_shape=(jax.ShapeDtypeStruct((N_TOK, GROUP_WIDTH), F32), cache_shape, cache_shape),
        input_output_aliases=aliases,
        compiler_params=pltpu.CompilerParams(dimension_semantics=("parallel",)),
        name="attn_context",
    )(*args)


def _attention_latent(u, prev, layer, *, col_q, col_k, col_v, cache_k, cache_v, window=False, sink=None,
                      qkg2=None):
    tq = TQ_ATTN
    nq = DEC_SEQ // tq
    row0_q, row0_k = N_CTX // tq, N_CTX // DEC_SEQ
    qb, kb, vb = col_q // GROUP_WIDTH, col_k // KV_WIDTH, col_v // KV_WIDTH
    cos, sin = _rope_tables(DEC_SEQ)
    const = lambda b, i: (0, 0)
    names = ['q', 'k', 'v', 'kc', 'vc', 'cos_q', 'sin_q', 'cos_k', 'sin_k']
    args = [u, u, u, cache_k, cache_v, cos, sin, cos, sin]
    in_specs = [
        pl.BlockSpec((tq, GROUP_WIDTH), lambda b, i: (row0_q + b * nq + i, qb)),
        pl.BlockSpec((DEC_SEQ, KV_WIDTH), lambda b, i: (row0_k + b, kb)),
        pl.BlockSpec((DEC_SEQ, KV_WIDTH), lambda b, i: (row0_k + b, vb)),
        pl.BlockSpec((None, None, PAST_LEN, KV_WIDTH), lambda b, i: (b, layer, 0, 0)),
        pl.BlockSpec((None, None, PAST_LEN, KV_WIDTH), lambda b, i: (b, layer, 0, 0)),
        pl.BlockSpec((tq, KV_WIDTH), lambda b, i: (i, 0)),
        pl.BlockSpec((tq, KV_WIDTH), lambda b, i: (i, 0)),
        pl.BlockSpec((DEC_SEQ, KV_WIDTH), const),
        pl.BlockSpec((DEC_SEQ, KV_WIDTH), const),
    ]
    if sink is not None:
        names.append('sink'); args.append(sink)
        in_specs.append(pl.BlockSpec(memory_space=pltpu.SMEM))
    if qkg2 is not None:
        names.append('qkg'); args.append(qkg2)
        in_specs.append(pl.BlockSpec((2, KV_WIDTH), const))
    aliases = {len(args): 0}
    names.append('prev'); args.append(prev); in_specs.append(_any_spec())
    names += ['o', 'k_scr']
    return pl.pallas_call(
        functools.partial(_attn_lat_kernel, tuple(names), window),
        grid=(DEC_BATCH, nq), in_specs=in_specs,
        out_specs=pl.BlockSpec((tq, GROUP_WIDTH), lambda b, i: (row0_q + b * nq + i, 0)),
        out_shape=jax.ShapeDtypeStruct((N_TOK, GROUP_WIDTH), F32),
        scratch_shapes=[pltpu.VMEM((DEC_SEQ, KV_WIDTH), BF16)],
        input_output_aliases=aliases,
        compiler_params=pltpu.CompilerParams(dimension_semantics=("parallel", "arbitrary")),
        name="attn_latent",
    )(*args)


def _dft_matrices(length):
    n = length
    k = np.arange(n, dtype=np.int64)[:, None]
    s = np.arange(n, dtype=np.int64)[None, :]
    ang = np.pi * ((k * s) % (2 * n)).astype(np.float64) / n
    fwd_cos = np.cos(ang)
    fwd_sin = -np.sin(ang)
    fwd_sin[0, :] = 1.0 - 2.0 * (np.arange(n) % 2)
    fwd = np.concatenate([fwd_cos, fwd_sin], axis=0)
    wk = np.full((n,), 2.0)
    wk[0] = 1.0
    inv_cos = (np.cos(ang) * wk[:, None]).T / (2 * n)
    inv_sin = (-2.0 * np.sin(ang)).T / (2 * n)
    inv_sin[:, 0] = (1.0 - 2.0 * (np.arange(n) % 2)) / (2 * n)
    inv = np.concatenate([inv_cos, inv_sin], axis=1)
    return fwd.astype(np.float32), inv.astype(np.float32)


def _filter_features(length):
    t = np.linspace(0.0, 1.0, length, dtype=np.float32)[:, None]
    w = (np.float32(2.0 * math.pi / length) * np.arange(length, dtype=np.float32))[:, None]
    bands = np.linspace(1e-4, HY_BANDS - 1, HY_BANDS, dtype=np.float32)[None, :]
    z = np.concatenate([t, np.cos(bands * w), -np.sin(bands * w)], axis=-1).astype(np.float32)
    zp = np.zeros((length, 128), np.float32)
    zp[:, :HY_EMB] = z
    deltas = np.linspace(math.log(HY_TARGET) / HY_SLOW, math.log(HY_TARGET) / HY_FAST, HY_CH, dtype=np.float32)
    decay = np.exp(-t * np.abs(deltas)).astype(np.float32)
    return zp, decay


def _filter_kernel(z_ref, w1_ref, b1_ref, w2_ref, b2_ref, w3_ref, fr_ref, dec_ref, f_ref, kr_ref, ki_ref):
    n = z_ref.shape[0]
    h = jnp.sin(fr_ref[0:1, :] * (_fdot(z_ref[...], w1_ref[...]) + b1_ref[...]))
    h = jnp.sin(fr_ref[1:2, :] * (_fdot(h, w2_ref[...]) + b2_ref[...]))
    h = _fdot(h, w3_ref[...])
    dec = dec_ref[...]
    half = HY_ORDER * HY_CH
    pos = jnp.concatenate([h[:, j * HY_CH:(j + 1) * HY_CH] * dec for j in range(HY_ORDER)], axis=-1)
    neg = jnp.concatenate([h[:, half + j * HY_CH:half + (j + 1) * HY_CH] * dec for j in range(HY_ORDER)], axis=-1)
    row = lax.broadcasted_iota(jnp.int32, (n, half), 0)
    neg = jnp.where(row == 0, 0.0, neg)
    fwd = f_ref[...]
    a = jnp.dot(fwd, (pos + neg).astype(BF16), preferred_element_type=F32)
    b = jnp.dot(fwd, (pos - neg).astype(BF16), preferred_element_type=F32)
    kr_ref[...] = a[:n]
    ki_ref[...] = jnp.where(row == 0, a[n:], b[n:])


def _hyena_filters(length, fwd, w1p, b1, w2, b2, w3, freq):
    zp, decay = _filter_features(length)
    half = HY_ORDER * HY_CH
    return pl.pallas_call(
        _filter_kernel,
        out_shape=(jax.ShapeDtypeStruct((length, half), F32), jax.ShapeDtypeStruct((length, half), F32)),
        name="hyena_filters",
    )(zp, w1p, b1.reshape(1, HY_FH), w2, b2.reshape(1, HY_FH), w3, freq, decay, fwd)


def _hyena_kernel(z_ref, cw_ref, cb_ref, f_ref, g_ref, kr_ref, ki_ref, bias_ref, o_ref, *, n):
    nb = z_ref.shape[0] // n
    row = lax.broadcasted_iota(jnp.int32, (n, 3 * HY_CH), 0)
    row0 = lax.broadcasted_iota(jnp.int32, (n, HY_CH), 0) == 0
    vs, x1s, x2s = [], [], []
    for bb in range(nb):
        z = z_ref[bb * n:(bb + 1) * n, :]
        prev = jnp.where(row == 0, 0.0, pltpu.roll(z, 1, 0))
        nxt = jnp.where(row == n - 1, 0.0, pltpu.roll(z, n - 1, 0))
        z = prev * cw_ref[0:1, :] + z * cw_ref[1:2, :] + nxt * cw_ref[2:3, :] + cb_ref[...]
        vs.append(z[:, :HY_CH])
        x1s.append(z[:, HY_CH:2 * HY_CH])
        x2s.append(z[:, 2 * HY_CH:])

    def long_conv(xs, order):
        cols = slice(order * HY_CH, (order + 1) * HY_CH)
        kr, ki = kr_ref[:, cols], ki_ref[:, cols]
        spec = jnp.dot(f_ref[...], jnp.concatenate([x.astype(BF16) for x in xs], axis=1),
                       preferred_element_type=F32)
        prods = []
        for bb in range(nb):
            ur, ui = spec[:n, bb * HY_CH:(bb + 1) * HY_CH], spec[n:, bb * HY_CH:(bb + 1) * HY_CH]
            yr = ur * kr - jnp.where(row0, 0.0, ui * ki)
            yi = jnp.where(row0, ui * ki, ur * ki + ui * kr)
            prods.append(jnp.concatenate([yr, yi], axis=0).astype(BF16))
        y = jnp.dot(g_ref[...], jnp.concatenate(prods, axis=1), preferred_element_type=F32)
        return [y[:, bb * HY_CH:(bb + 1) * HY_CH] + xs[bb] * bias_ref[order:order + 1, :] for bb in range(nb)]

    ys = long_conv(vs, 0)
    ys = long_conv([x1s[bb] * ys[bb] for bb in range(nb)], 1)
    for bb in range(nb):
        o_ref[bb * n:(bb + 1) * n, :] = x2s[bb] * ys[bb]


def _hyena(u, prev, *, latent, fwd, inv, kr, ki, conv_w, conv_b, bias):
    n = DEC_SEQ if latent else SEQ
    nb = DEC_BATCH if latent else NB_HYENA
    steps = 1 if latent else BATCH // nb
    row0 = N_CTX // (nb * n) if latent else 0
    half = HY_ORDER * HY_CH
    const = lambda b: (0, 0)
    args = [u, conv_w, conv_b.reshape(1, 3 * HY_CH), fwd, inv, kr, ki, bias]
    in_specs = [
        pl.BlockSpec((nb * n, 3 * HY_CH), lambda b: (row0 + b, COL_HY // (3 * HY_CH))),
        pl.BlockSpec((3, 3 * HY_CH), const),
        pl.BlockSpec((1, 3 * HY_CH), const),
        pl.BlockSpec((2 * n, n), const),
        pl.BlockSpec((n, 2 * n), const),
        pl.BlockSpec((n, half), const),
        pl.BlockSpec((n, half), const),
        pl.BlockSpec((HY_ORDER, HY_CH), const),
    ]
    aliases = {}
    if latent:
        aliases = {len(args): 0}
        args.append(prev)
        in_specs.append(_any_spec())

    def body(*refs):
        if latent:
            refs = refs[:len(args) - 1] + refs[len(args):]
        _hyena_kernel(*refs, n=n)

    return pl.pallas_call(
        body, grid=(steps,), in_specs=in_specs,
        out_specs=pl.BlockSpec((nb * n, HY_CH), lambda b: (row0 + b, 0)),
        out_shape=jax.ShapeDtypeStruct((N_TOK, HY_CH), F32),
        input_output_aliases=aliases,
        compiler_params=pltpu.CompilerParams(dimension_semantics=("parallel",)),
        name="hyena_latent" if latent else "hyena_context",
    )(*args)


def _gla_masks():
    t = np.arange(ROWS_GROUP)[:, None]
    s = np.arange(ROWS_GROUP)[None, :]
    same = (t // GLA_CHUNK) == (s // GLA_CHUNK)
    return np.stack([same & (s <= t), same & (s >= t), same]).astype(np.float32)


def _gla_group(r, rows, z, st):
    c = GLA_CHUNK
    hk = H_GLA * DK_GLA
    n_chunks = ROWS_GROUP // c
    logits = _bdot(r['gl'][rows, :], r['gw'][z]) + r['gb'][z:z + 1, :]
    g = (jnp.minimum(logits, 0.0) - jnp.log1p(jnp.exp(-jnp.abs(logits)))) / GLA_NORM
    causal = r['masks'][z]
    b = _split_dot(causal.astype(BF16), g)
    tot = _split_dot(r['masks'][2].astype(BF16), g)
    k = r['k'][rows, :]
    q_t = r['q'][rows, :] * jnp.exp(b) * (DK_GLA ** -0.5)
    k_t = (k * jnp.exp(-b)).astype(BF16)
    k_e = (k * jnp.exp(tot - b)).astype(BF16)
    dec = jnp.exp(tot)
    v = r['v'][rows, :].astype(BF16)

    klane = lax.broadcasted_iota(jnp.int32, (c, hk), 1) // DK_GLA
    vlane = lax.broadcasted_iota(jnp.int32, (c, GROUP_WIDTH), 1) // DV_GLA
    causal_c = jnp.concatenate([causal[0:c, 0:c]] * H_GLA, axis=1) != 0.0
    bd = (lax.broadcasted_iota(jnp.int32, (GROUP_WIDTH, hk), 0) // DV_GLA
          == lax.broadcasted_iota(jnp.int32, (GROUP_WIDTH, hk), 1) // DK_GLA)
    q_bf = q_t.astype(BF16)
    zero_k = jnp.zeros((c, hk), BF16)
    zero_v = jnp.zeros((c, GROUP_WIDTH), BF16)
    o_intra = [None] * n_chunks
    for ci in range(n_chunks):
        cr = slice(ci * c, (ci + 1) * c)
        k_stack = jnp.concatenate([jnp.where(klane == h, k_t[cr], zero_k) for h in range(H_GLA)], axis=0)
        v_diag = jnp.concatenate([jnp.where(vlane == h, v[cr], zero_v) for h in range(H_GLA)], axis=0)
        s = lax.dot_general(q_bf[cr], k_stack, (((1,), (1,)), ((), ())), preferred_element_type=F32)
        p = jnp.where(causal_c, s, 0.0).astype(BF16)
        o_intra[ci] = jnp.dot(p, v_diag, preferred_element_type=F32)

    o = [None] * n_chunks
    for step in range(n_chunks):
        ci = step if z == 0 else n_chunks - 1 - step
        cr = slice(ci * c, (ci + 1) * c)
        o[ci] = o_intra[ci] + _bdot_nt(q_bf[cr], st)
        st = st * dec[ci * c:ci * c + 1, :] + jnp.where(bd, _bdot_tn(v[cr], k_e[cr]), 0.0)
    return jnp.concatenate(o, axis=0), st


def _head_rms(o, g):
    hb = (lax.broadcasted_iota(jnp.int32, (GROUP_WIDTH, GROUP_WIDTH), 0) // DV_GLA
          == lax.broadcasted_iota(jnp.int32, (GROUP_WIDTH, GROUP_WIDTH), 1) // DV_GLA)
    ms = _split_dot_rhs(o * o, hb.astype(BF16)) * (1.0 / DV_GLA)
    return o * lax.rsqrt(ms + EPS) * g


def _split_dot_rhs(x, m):
    hi = x.astype(BF16)
    lo = (x - hi.astype(F32)).astype(BF16)
    return jnp.dot(hi, m, preferred_element_type=F32) + jnp.dot(lo, m, preferred_element_type=F32)


def _expand_state(s):
    hk = H_GLA * DK_GLA
    bd = (lax.broadcasted_iota(jnp.int32, (GROUP_WIDTH, hk), 0) // DV_GLA
          == lax.broadcasted_iota(jnp.int32, (GROUP_WIDTH, hk), 1) // DK_GLA)
    return jnp.where(bd, jnp.concatenate([s] * H_GLA, axis=0), 0.0)


def _compact_state(st):
    out = st[0:DV_GLA]
    for h in range(1, H_GLA):
        out = out + st[h * DV_GLA:(h + 1) * DV_GLA]
    return out


def _gla_ctx_kernel(names, *refs):
    r = dict(zip(names, refs))
    hk = H_GLA * DK_GLA
    for bb in range(NB_CTX):
        rows = slice(bb * SEQ, (bb + 1) * SEQ)
        zero = jnp.zeros((GROUP_WIDTH, hk), F32)
        o_f, st_f = _gla_group(r, rows, 0, zero)
        o_b, st_b = _gla_group(r, rows, 1, zero)
        r['o'][rows, :] = _head_rms(o_f + o_b, r['mg'][...])
        r['sf'][bb, 0] = _compact_state(st_f)
        r['sf'][bb, 1] = _compact_state(st_b)


def _gla_lat_kernel(names, *refs):
    r = dict(zip(names, refs))
    n_groups = DEC_SEQ // ROWS_GROUP

    def run(z):
        r['st'][...] = _expand_state(r['s0'][z])

        def body(step, carry):
            gi = step if z == 0 else n_groups - 1 - step
            rows = pl.ds(pl.multiple_of(gi * ROWS_GROUP, ROWS_GROUP), ROWS_GROUP)
            o, st = _gla_group(r, rows, z, r['st'][...])
            r['st'][...] = st
            if z == 0:
                r['o'][rows, :] = o
            else:
                r['o'][rows, :] = _head_rms(r['o'][rows, :] + o, r['mg'][...])
            return carry

        lax.fori_loop(0, n_groups, body, 0)

    run(0)
    run(1)


def _gla(u, prev, *, latent, gate_w, gate_b, mix_g_d, s0=None):
    n = DEC_SEQ if latent else NB_CTX * SEQ
    steps = DEC_BATCH if latent else BATCH // NB_CTX
    row0 = N_CTX // n if latent else 0
    hk = H_GLA * DK_GLA
    const2 = lambda b: (0, 0)
    names = ['q', 'k', 'v', 'gl', 'gw', 'gb', 'mg', 'masks']
    args = [u, u, u, u, gate_w, gate_b, mix_g_d, _gla_masks()]
    in_specs = [
        pl.BlockSpec((n, hk), lambda b: (row0 + b, COL_QD // hk)),
        pl.BlockSpec((n, hk), lambda b: (row0 + b, COL_KD // hk)),
        pl.BlockSpec((n, GROUP_WIDTH), lambda b: (row0 + b, COL_VD // GROUP_WIDTH)),
        pl.BlockSpec((n, 128), lambda b: (row0 + b, COL_GD // 128)),
        pl.BlockSpec((2, 128, hk), lambda b: (0, 0, 0)),
        pl.BlockSpec((2, hk), const2),
        pl.BlockSpec((1, GROUP_WIDTH), const2),
        pl.BlockSpec((3, ROWS_GROUP, ROWS_GROUP), lambda b: (0, 0, 0)),
    ]
    out_shape = jax.ShapeDtypeStruct((N_TOK, GROUP_WIDTH), F32)
    out_specs = pl.BlockSpec((n, GROUP_WIDTH), lambda b: (row0 + b, 0))
    aliases, scratch = {}, []
    if latent:
        names.append('s0'); args.append(s0)
        in_specs.append(pl.BlockSpec((None, 2, DV_GLA, hk), lambda b: (b, 0, 0, 0)))
        aliases = {len(args): 0}
        names.append('prev'); args.append(prev); in_specs.append(_any_spec())
        names += ['o', 'st']
        scratch = [pltpu.VMEM((GROUP_WIDTH, hk), F32)]
        body = functools.partial(_gla_lat_kernel, tuple(names))
    else:
        names += ['o', 'sf']
        out_shape = (out_shape, jax.ShapeDtypeStruct((BATCH, 2, DV_GLA, hk), F32))
        out_specs = (out_specs, pl.BlockSpec((NB_CTX, 2, DV_GLA, hk), lambda b: (b, 0, 0, 0)))
        body = functools.partial(_gla_ctx_kernel, tuple(names))
    return pl.pallas_call(
        body, grid=(steps,), in_specs=in_specs, out_specs=out_specs, out_shape=out_shape,
        scratch_shapes=scratch, input_output_aliases=aliases,
        compiler_params=pltpu.CompilerParams(dimension_semantics=("parallel",)),
        name="gla_latent" if latent else "gla_context",
    )(*args)


def _merge_kernel(h_ref, oa_ref, ob_ref, oc_ref, od_ref, rd_ref, gt_ref, mg_ref, wo_ref, o_ref):
    gw = GROUP_WIDTH

    def gain(g, order):
        return jnp.concatenate([mg_ref[:, g * gw + h * HEAD_DIM:g * gw + (h + 1) * HEAD_DIM] for h in order], axis=1)

    def weight(g, order):
        return jnp.concatenate([wo_ref[g * gw + h * HEAD_DIM:g * gw + (h + 1) * HEAD_DIM, :] for h in order], axis=0)

    natural = range(N_HEADS)
    ys = [
        _rms(oa_ref[...], gain(0, HEAD_ORDER)),
        _rms(ob_ref[...], gain(1, natural)),
        _rms(oc_ref[...], gain(2, HEAD_ORDER)),
        od_ref[...] * _silu(rd_ref[:, COL_RD - COL_GD:COL_RD - COL_GD + gw]),
    ]
    orders = [HEAD_ORDER, natural, HEAD_ORDER, natural]
    mix = jnp.dot(ys[0].astype(BF16), weight(0, orders[0]), preferred_element_type=F32)
    for j in range(1, 4):
        mix = mix + jnp.dot(ys[j].astype(BF16), weight(j, orders[j]), preferred_element_type=F32)
    o_ref[...] = h_ref[...] + gt_ref[...] * mix


def _merge(h, oa, ob, oc, od, u, mod, mix_g, w_out, layer):
    tm = TM_PROJ
    grp = pl.BlockSpec((tm, GROUP_WIDTH), lambda i: (i, 0))
    return pl.pallas_call(
        _merge_kernel,
        grid=(N_TOK // tm,),
        in_specs=[
            pl.BlockSpec((tm, D_MODEL), lambda i: (i, 0)),
            grp, grp, grp, grp,
            pl.BlockSpec((tm, COL_RD_BLOCK), lambda i: (i, COL_GD // COL_RD_BLOCK)),
            _mod_spec(tm, 5, 1),
            pl.BlockSpec((None, 1, D_MODEL), lambda i: (layer, 0, 0)),
            pl.BlockSpec((None, D_MODEL, D_MODEL), lambda i: (layer, 0, 0)),
        ],
        out_specs=pl.BlockSpec((tm, D_MODEL), lambda i: (i, 0)),
        out_shape=jax.ShapeDtypeStruct((N_TOK, D_MODEL), F32),
        compiler_params=pltpu.CompilerParams(dimension_semantics=("parallel",)),
        name="merge",
    )(h, oa, ob, oc, od, u, mod, mix_g, w_out)


def _gate_weights(gate_w):
    out = jnp.zeros((2, 128, H_GLA * DK_GLA), gate_w.dtype)
    out = out.at[0, 0:GLA_RANK].set(gate_w[0])
    return out.at[1, GLA_RANK:2 * GLA_RANK].set(gate_w[1])


def _states_to_kernel(st):
    b = st.shape[0]
    return st.transpose(0, 1, 4, 2, 3).reshape(b, 2, DV_GLA, H_GLA * DK_GLA)


def _states_from_kernel(st):
    b = st.shape[0]
    return st.reshape(b, 2, DV_GLA, H_GLA, DK_GLA).transpose(0, 1, 3, 4, 2)


def kernel(x_prompt, x_sample, cache_swa_k, cache_swa_v, cache_gqa_k, cache_gqa_v, state_gla, c, c_ctx, w_mod, b_mod, norm_g, ffn_w_in, ffn_w_out, w_in, w_out, mix_g, swa_sink, qk_norm_g, hy_conv_w, hy_conv_b, hy_w1, hy_b1, hy_w2, hy_b2, hy_w3, hy_freq, hy_bias, gla_gate_w, gla_gate_b, final_g):
    cond = jnp.zeros((MOD_ROWS, D_MODEL), F32).at[0].set(c_ctx).at[1:1 + DEC_BATCH].set(c)
    mod_all = _modulation(cond, w_mod, b_mod).reshape(DEPTH, MOD_ROWS, N_MOD, 1, D_MODEL)

    dft = {n: tuple(jnp.asarray(m).astype(BF16) for m in _dft_matrices(n)) for n in (SEQ, DEC_SEQ)}

    xs = [x_prompt.reshape(N_CTX, D_MODEL), x_sample.reshape(N_LAT, D_MODEL)]
    ffn_in = ffn_w_in.astype(BF16)
    ffn_out = ffn_w_out.astype(BF16)
    ng = norm_g.reshape(DEPTH, 3, 1, D_MODEL)
    proj_w = jnp.pad(w_in, ((0, 0), (0, 0), (0, PROJ_WIDTH - REF_END))).astype(BF16)
    out_w = w_out.astype(BF16)
    mix_gp = mix_g.reshape(DEPTH, 1, D_MODEL)
    cache_shape = (DEC_BATCH, DEPTH, PAST_LEN, KV_WIDTH)
    ck_a, cv_a = cache_swa_k.reshape(cache_shape), cache_swa_v.reshape(cache_shape)
    ck_c, cv_c = cache_gqa_k.reshape(cache_shape), cache_gqa_v.reshape(cache_shape)
    cache_a = cache_c = None
    sts = []
    for l in range(DEPTH):
        mod = mod_all[l]
        h = _ffn(xs, mod, 0, ng, ffn_in, ffn_out, l, 0)
        u = _in_proj(h, mod, ng, proj_w, l)

        qkg2 = jnp.tile(qk_norm_g[l], (1, N_KV))

        oa, *cache_a = _attention_context(u, l, cache_a, col_q=COL_QA, col_k=COL_KA, col_v=COL_VA,
                                          sink=swa_sink[l])
        oa = _attention_latent(u, oa, l, col_q=COL_QA, col_k=COL_KA, col_v=COL_VA, cache_k=ck_a, cache_v=cv_a,
                               window=True, sink=swa_sink[l])
        oc, *cache_c = _attention_context(u, l, cache_c, col_q=COL_QC, col_k=COL_KC, col_v=COL_VC, qkg2=qkg2)
        oc = _attention_latent(u, oc, l, col_q=COL_QC, col_k=COL_KC, col_v=COL_VC, cache_k=ck_c, cache_v=cv_c,
                               qkg2=qkg2)

        w1p = jnp.zeros((128, HY_FH), F32).at[:HY_EMB].set(hy_w1[l])
        ob = None
        for latent, n in ((False, SEQ), (True, DEC_SEQ)):
            fwd, inv = dft[n]
            kr, ki = _hyena_filters(n, fwd, w1p, hy_b1[l], hy_w2[l], hy_b2[l], hy_w3[l], hy_freq[l])
            ob = _hyena(u, ob, latent=latent, fwd=fwd, inv=inv, kr=kr, ki=ki, conv_w=hy_conv_w[l],
                        conv_b=hy_conv_b[l], bias=hy_bias[l])

        gw = _gate_weights(gla_gate_w[l]).astype(BF16)
        mg_d = mix_g[l, 3 * GROUP_WIDTH:].reshape(1, GROUP_WIDTH)
        od, st = _gla(u, None, latent=False, gate_w=gw, gate_b=gla_gate_b[l], mix_g_d=mg_d)
        od = _gla(u, od, latent=True, gate_w=gw, gate_b=gla_gate_b[l], mix_g_d=mg_d,
                  s0=_states_to_kernel(state_gla[:, l]))

        h = _merge(h, oa, ob, oc, od, u, mod, mix_gp, out_w, l)
        xs = [_ffn([h], mod, 6, ng, ffn_in, ffn_out, l, 1)] if l + 1 < DEPTH else None
        if xs is None:
            y_prompt, y_sample = _ffn([h], mod, 6, ng, ffn_in, ffn_out, l, 1, final_g=final_g)
        sts.append(_states_from_kernel(st))

    y_prompt = y_prompt.reshape(BATCH, SEQ, D_MODEL)
    y_sample = y_sample.reshape(DEC_BATCH, DEC_SEQ, D_MODEL)
    caches = [x.reshape(BATCH, DEPTH, SEQ, N_KV, HEAD_DIM) for x in (*cache_a, *cache_c)]
    return (y_prompt, y_sample, *caches, jnp.stack(sts, axis=1))
```

```python
import functools
import math

import numpy as np
import jax
import jax.numpy as jnp
from jax import lax
from jax.experimental import pallas as pl
from jax.experimental.pallas import tpu as pltpu

F32 = jnp.float32
BF16 = jnp.bfloat16

D_MODEL = 1024
BATCH = 32
SEQ = 256
DEPTH = 2
DEC_BATCH = 2
DEC_SEQ = 1024
PAST_LEN = 512
GRID_W = 64
HEAD_DIM = 64
GROUP_WIDTH = 256
N_HEADS = 4
N_KV = 2
KV_WIDTH = N_KV * HEAD_DIM
WINDOW = 128
HY_CH = 256
HY_ORDER = 2
HY_BANDS = 16
HY_EMB = 2 * HY_BANDS + 1
HY_FH = 64
HY_TARGET = 1e-2
HY_FAST = 0.3
HY_SLOW = 1.5
H_GLA = 4
DV_GLA = 64
DK_GLA = 32
GLA_RANK = 16
GLA_CHUNK = 64
GLA_NORM = 16.0
D_FF = 2816
N_MOD = 9
ROPE_THETA = 10000.0
EPS = 1e-6

N_CTX = BATCH * SEQ
N_LAT = DEC_BATCH * DEC_SEQ
N_TOK = N_CTX + N_LAT
MOD_ROWS = 8

COL_HY, COL_QA, COL_KA, COL_VA = 0, 768, 1024, 1152
COL_QC, COL_KC, COL_VC = 1280, 1536, 1664
COL_QD, COL_KD, COL_VD, COL_GD, COL_RD = 1792, 1920, 2048, 2304, 2336
PROJ_WIDTH = 2688
REF_QA, REF_KA, REF_HY, REF_QC, REF_KC, REF_END = 0, 256, 512, 1280, 1536, 2592
COL_RD_BLOCK = 384
HEAD_ORDER = (0, 2, 1, 3)
_HEAD_PERM = np.concatenate([np.arange(HEAD_DIM) + HEAD_DIM * h for h in HEAD_ORDER])

TM_FFN = 1024
TK_FFN = 256
FFN_VMEM_BYTES = 60 * 1024 * 1024
TM_PROJ = 1024
TQ_ATTN = 256
ROWS_GROUP = 256
NB_CTX = 4
NB_CTX_NORMED = 2
NB_HYENA = 4


def _bdot(a, b):
    return jnp.dot(a.astype(BF16), b.astype(BF16), preferred_element_type=F32)


def _bdot_nt(a, b):
    return lax.dot_general(a.astype(BF16), b.astype(BF16), (((1,), (1,)), ((), ())),
                           preferred_element_type=F32)


def _bdot_tn(a, b):
    return lax.dot_general(a.astype(BF16), b.astype(BF16), (((0,), (0,)), ((), ())),
                           preferred_element_type=F32)


def _fdot(a, b):
    a_hi, b_hi = a.astype(BF16), b.astype(BF16)
    a_lo = (a - a_hi.astype(F32)).astype(BF16)
    b_lo = (b - b_hi.astype(F32)).astype(BF16)
    dot = functools.partial(jnp.dot, preferred_element_type=F32)
    return dot(a_hi, b_hi) + (dot(a_hi, b_lo) + dot(a_lo, b_hi) + dot(a_lo, b_lo))


def _split_dot(m, x):
    hi = x.astype(BF16)
    lo = (x - hi.astype(F32)).astype(BF16)
    return jnp.dot(m, hi, preferred_element_type=F32) + jnp.dot(m, lo, preferred_element_type=F32)


def _rms(x, g):
    return x * lax.rsqrt(jnp.mean(x * x, axis=-1, keepdims=True) + EPS) * g


def _silu(x):
    return x * jax.nn.sigmoid(x)


def _mod_row(i, tm):
    n_ctx_tiles = N_CTX // tm
    per_batch = DEC_SEQ // tm
    return jnp.where(i < n_ctx_tiles, 0, 1 + (i - n_ctx_tiles) // per_batch)


def _mod_spec(tm, j, grid_rank):
    if grid_rank == 1:
        return pl.BlockSpec((None, None, 1, D_MODEL), lambda i: (_mod_row(i, tm), j, 0, 0))
    return pl.BlockSpec((None, None, 1, D_MODEL), lambda i, k: (_mod_row(i, tm), j, 0, 0))


def _any_spec():
    return pl.BlockSpec(memory_space=pl.ANY)


def _mod_kernel(cond_ref, w_ref, b_ref, o_ref):
    o_ref[...] = _bdot(_silu(cond_ref[...]), w_ref[...]) + b_ref[...]


def _modulation(cond, w_mod, b_mod):
    tn = 3 * D_MODEL
    return pl.pallas_call(
        _mod_kernel,
        grid=(DEPTH, N_MOD * D_MODEL // tn),
        in_specs=[
            pl.BlockSpec((MOD_ROWS, D_MODEL), lambda l, j: (0, 0)),
            pl.BlockSpec((None, D_MODEL, tn), lambda l, j: (l, 0, j)),
            pl.BlockSpec((None, 1, tn), lambda l, j: (l, 0, j)),
        ],
        out_specs=pl.BlockSpec((None, MOD_ROWS, tn), lambda l, j: (l, 0, j)),
        out_shape=jax.ShapeDtypeStruct((DEPTH, MOD_ROWS, N_MOD * D_MODEL), F32),
        name="modulation",
    )(cond, w_mod, b_mod.reshape(DEPTH, 1, N_MOD * D_MODEL))


def _ffn_kernel(split_input, final, *refs):
    is_ctx = pl.program_id(0) < N_CTX // TM_FFN
    if split_input:
        xa_ref, xb_ref, *refs = refs
        x = jnp.where(is_ctx, xa_ref[...], xb_ref[...])
    else:
        x_ref, *refs = refs
        x = x_ref[...]
    if final:
        sh_ref, sc_ref, gt_ref, ng_ref, wi_ref, wo_ref, fg_ref, yc_ref, yl_ref, xm_ref, act_ref = refs
    else:
        sh_ref, sc_ref, gt_ref, ng_ref, wi_ref, wo_ref, o_ref, xm_ref, act_ref = refs
    xn = _rms(x, ng_ref[...])
    xm_ref[...] = (xn * (1.0 + sc_ref[...]) + sh_ref[...]).astype(BF16)
    for c0 in range(0, D_FF, TK_FFN):
        c1 = min(c0 + TK_FFN, D_FF)
        xm = xm_ref[...]
        a = jnp.dot(xm, wi_ref[:, c0:c1], preferred_element_type=F32)
        b = jnp.dot(xm, wi_ref[:, D_FF + c0:D_FF + c1], preferred_element_type=F32)
        act_ref[:, c0:c1] = (_silu(a) * b).astype(BF16)
    y = jnp.dot(act_ref[...], wo_ref[...], preferred_element_type=F32)
    out = x + 0.5 * gt_ref[...] * y
    if not final:
        o_ref[...] = out
        return
    out = _rms(out, fg_ref[...])

    @pl.when(is_ctx)
    def _():
        yc_ref[...] = out

    @pl.when(jnp.logical_not(is_ctx))
    def _():
        yl_ref[...] = out


def _ffn_lookahead_kernel(x_ref, xn_ref, sh_ref, sc_ref, gt_ref, shn_ref, scn_ref, ng_ref, wi_ref, wo_ref,
                          o_ref, xm_ref, act_ref):
    i = pl.program_id(0)
    slot = i % 2

    def prepare(x, sh, sc):
        return (_rms(x, ng_ref[...]) * (1.0 + sc) + sh).astype(BF16)

    @pl.when(i == 0)
    def _():
        xm_ref[0] = prepare(x_ref[...], sh_ref[...], sc_ref[...])

    for c0 in range(0, D_FF, TK_FFN):
        c1 = min(c0 + TK_FFN, D_FF)
        xm = xm_ref[slot]
        a = jnp.dot(xm, wi_ref[:, c0:c1], preferred_element_type=F32)
        b = jnp.dot(xm, wi_ref[:, D_FF + c0:D_FF + c1], preferred_element_type=F32)
        act_ref[:, c0:c1] = (_silu(a) * b).astype(BF16)
    xm_ref[1 - slot] = prepare(xn_ref[...], shn_ref[...], scn_ref[...])
    y = jnp.dot(act_ref[...], wo_ref[...], preferred_element_type=F32)
    o_ref[...] = x_ref[...] + 0.5 * gt_ref[...] * y


def _ffn_lookahead(h, mod, j0, ng, w_in, w_out, layer, which):
    tm = TM_FFN
    n_tiles = N_TOK // tm
    nxt = lambda i: jnp.minimum(i + 1, n_tiles - 1)
    mod_next = lambda j: pl.BlockSpec((None, None, 1, D_MODEL), lambda i: (_mod_row(nxt(i), tm), j, 0, 0))
    resident = dict(pipeline_mode=pl.Buffered(1))
    return pl.pallas_call(
        _ffn_lookahead_kernel,
        grid=(n_tiles,),
        in_specs=[
            pl.BlockSpec((tm, D_MODEL), lambda i: (i, 0)),
            pl.BlockSpec((tm, D_MODEL), lambda i: (nxt(i), 0)),
            _mod_spec(tm, j0, 1), _mod_spec(tm, j0 + 1, 1), _mod_spec(tm, j0 + 2, 1),
            mod_next(j0), mod_next(j0 + 1),
            pl.BlockSpec((None, None, 1, D_MODEL), lambda i: (layer, j0 // 3, 0, 0)),
            pl.BlockSpec((None, None, D_MODEL, 2 * D_FF), lambda i: (layer, which, 0, 0), **resident),
            pl.BlockSpec((None, None, D_FF, D_MODEL), lambda i: (layer, which, 0, 0), **resident),
        ],
        out_specs=pl.BlockSpec((tm, D_MODEL), lambda i: (i, 0)),
        out_shape=jax.ShapeDtypeStruct((N_TOK, D_MODEL), F32),
        scratch_shapes=[pltpu.VMEM((2, tm, D_MODEL), BF16), pltpu.VMEM((tm, D_FF), BF16)],
        compiler_params=pltpu.CompilerParams(dimension_semantics=("arbitrary",),
                                             vmem_limit_bytes=FFN_VMEM_BYTES),
        name="ffn_lookahead",
    )(h, h, mod, mod, mod, mod, mod, ng, w_in, w_out)


def _ffn(xs, mod, j0, ng, w_in, w_out, layer, which, final_g=None):
    tm = TM_FFN
    n_ctx_tiles = N_CTX // tm
    split_specs = [pl.BlockSpec((tm, D_MODEL), lambda i: (jnp.minimum(i, n_ctx_tiles - 1), 0)),
                   pl.BlockSpec((tm, D_MODEL), lambda i: (jnp.maximum(i - n_ctx_tiles, 0), 0))]
    whole_spec = pl.BlockSpec((tm, D_MODEL), lambda i: (i, 0))
    x_specs = split_specs if len(xs) == 2 else [whole_spec]
    resident = dict(pipeline_mode=pl.Buffered(1))
    final = final_g is not None
    extra_args, extra_specs = [], []
    if final:
        extra_args = [final_g.reshape(1, D_MODEL)]
        extra_specs = [pl.BlockSpec((1, D_MODEL), lambda i: (0, 0))]
        out_specs = tuple(split_specs)
        out_shape = (jax.ShapeDtypeStruct((N_CTX, D_MODEL), F32), jax.ShapeDtypeStruct((N_LAT, D_MODEL), F32))
    else:
        out_specs = whole_spec
        out_shape = jax.ShapeDtypeStruct((N_TOK, D_MODEL), F32)
    return pl.pallas_call(
        functools.partial(_ffn_kernel, len(xs) == 2, final),
        grid=(N_TOK // tm,),
        in_specs=x_specs + [
            _mod_spec(tm, j0, 1), _mod_spec(tm, j0 + 1, 1), _mod_spec(tm, j0 + 2, 1),
            pl.BlockSpec((None, None, 1, D_MODEL), lambda i: (layer, j0 // 3, 0, 0)),
            pl.BlockSpec((None, None, D_MODEL, 2 * D_FF), lambda i: (layer, which, 0, 0), **resident),
            pl.BlockSpec((None, None, D_FF, D_MODEL), lambda i: (layer, which, 0, 0), **resident),
        ] + extra_specs,
        out_specs=out_specs,
        out_shape=out_shape,
        scratch_shapes=[pltpu.VMEM((tm, D_MODEL), BF16), pltpu.VMEM((tm, D_FF), BF16)],
        compiler_params=pltpu.CompilerParams(dimension_semantics=("arbitrary",),
                                             vmem_limit_bytes=FFN_VMEM_BYTES),
        name="ffn",
    )(*xs, mod, mod, mod, ng, w_in, w_out, *extra_args)


def _proj_kernel(x_ref, sh_ref, sc_ref, ng_ref, w_ref, o_ref):
    xn = _rms(x_ref[...], ng_ref[...])
    xm = (xn * (1.0 + sc_ref[...]) + sh_ref[...]).astype(BF16)

    def put(col, w):
        o_ref[:, col:col + w.shape[1]] = jnp.dot(xm, w, preferred_element_type=F32)

    def queries(base):
        return jnp.concatenate([w_ref[:, base + HEAD_DIM * h:base + HEAD_DIM * (h + 1)] for h in HEAD_ORDER], axis=1)

    put(COL_HY, w_ref[:, REF_HY:REF_QC])
    put(COL_QA, queries(REF_QA))
    put(COL_KA, w_ref[:, REF_KA:REF_HY])
    put(COL_QC, queries(REF_QC))
    put(COL_KC, w_ref[:, REF_KC:COL_GD])
    put(COL_GD, w_ref[:, COL_GD:REF_END])
    o_ref[:, REF_END:PROJ_WIDTH] = jnp.zeros((x_ref.shape[0], PROJ_WIDTH - REF_END), F32)


def _in_proj(h, mod, ng, w, layer):
    tm = TM_PROJ
    return pl.pallas_call(
        _proj_kernel,
        grid=(N_TOK // tm,),
        in_specs=[
            pl.BlockSpec((tm, D_MODEL), lambda i: (i, 0)),
            _mod_spec(tm, 3, 1), _mod_spec(tm, 4, 1),
            pl.BlockSpec((None, None, 1, D_MODEL), lambda i: (layer, 1, 0, 0)),
            pl.BlockSpec((None, D_MODEL, PROJ_WIDTH), lambda i: (layer, 0, 0)),
        ],
        out_specs=pl.BlockSpec((tm, PROJ_WIDTH), lambda i: (i, 0)),
        out_shape=jax.ShapeDtypeStruct((N_TOK, PROJ_WIDTH), F32),
        compiler_params=pltpu.CompilerParams(dimension_semantics=("parallel",)),
        name="in_proj",
    )(h, mod, mod, ng, w)


def _rope_tables(length):
    rows = length // GRID_W
    r = np.repeat(np.arange(rows, dtype=np.float32), GRID_W)
    col = np.tile(np.arange(GRID_W, dtype=np.float32), rows)
    nf = HEAD_DIM // 4
    inv = (np.float32(ROPE_THETA) ** (-np.arange(nf, dtype=np.float32) / nf)).astype(np.float32)
    ang = np.concatenate([r[:, None] * inv, col[:, None] * inv], axis=-1).astype(np.float32)
    cos, sin = np.cos(ang).astype(np.float32), np.sin(ang).astype(np.float32)
    return np.tile(cos, (1, 4)), np.tile(np.concatenate([-sin, sin], axis=-1), (1, 2))


def _pair_lanes(rows):
    lane = lax.broadcasted_iota(jnp.int32, (rows, 2 * HEAD_DIM), 1)
    return lane < HEAD_DIM, (lane % HEAD_DIM) < HEAD_DIM // 2


def _rope_pair(x, cos, sin_signed, first_half):
    partner = jnp.where(first_half, pltpu.roll(x, 3 * HEAD_DIM // 2, 1), pltpu.roll(x, HEAD_DIM // 2, 1))
    return x * cos + partner * sin_signed


def _rms_pair(x, g, lo):
    x2 = x * x
    s_lo = jnp.sum(jnp.where(lo, x2, 0.0), axis=-1, keepdims=True)
    s_hi = jnp.sum(jnp.where(lo, 0.0, x2), axis=-1, keepdims=True)
    ms = jnp.where(lo, s_lo, s_hi) * (1.0 / HEAD_DIM)
    return x * lax.rsqrt(ms + EPS) * g


def _attn_core(q, k_bf, v_bf, lo, *, mask=None, ctx=None, sinks=None):
    tq = q.shape[0]
    pair = 2 * HEAD_DIM
    scale = HEAD_DIM ** -0.5
    qa, qb = q[:, :pair] * scale, q[:, pair:] * scale
    qs = jnp.concatenate([jnp.where(lo, qa, 0.0), jnp.where(lo, 0.0, qa),
                          jnp.where(lo, qb, 0.0), jnp.where(lo, 0.0, qb)], axis=0).astype(BF16)
    s = _bdot_nt(qs, k_bf)
    if ctx is not None:
        s_c = _bdot_nt(qs, ctx[0])
    ps, pcs, dens = [], [], []
    for j in range(N_HEADS):
        rows = slice(j * tq, (j + 1) * tq)
        sj = s[rows]
        if mask is not None:
            sj = jnp.where(mask, sj, -1e30)
        m = jnp.max(sj, axis=-1, keepdims=True)
        if ctx is not None:
            m = jnp.maximum(m, jnp.max(s_c[rows], axis=-1, keepdims=True))
        if sinks is not None:
            m = jnp.maximum(m, sinks[j])
        p = jnp.exp(sj - m)
        den = jnp.sum(p, axis=-1, keepdims=True)
        ps.append(p.astype(BF16))
        if ctx is not None:
            pc = jnp.exp(s_c[rows] - m)
            den = den + jnp.sum(pc, axis=-1, keepdims=True)
            pcs.append(pc.astype(BF16))
        if sinks is not None:
            den = den + jnp.exp(sinks[j] - m)
        dens.append(den)
    r = jnp.dot(jnp.concatenate(ps, axis=0), v_bf, preferred_element_type=F32)
    if ctx is not None:
        r = r + jnp.dot(jnp.concatenate(pcs, axis=0), ctx[1], preferred_element_type=F32)
    o = [r[j * tq:(j + 1) * tq] / dens[j] for j in range(N_HEADS)]
    return jnp.concatenate([jnp.where(lo, o[0], o[1]), jnp.where(lo, o[2], o[3])], axis=1)


def _attn_ctx_kernel(names, nb, *refs):
    r = dict(zip(names, refs))
    lo, _ = _pair_lanes(SEQ)
    sinks = [r['sink'][h] for h in HEAD_ORDER] if 'sink' in r else None
    for bb in range(nb):
        rows = slice(bb * SEQ, (bb + 1) * SEQ)
        q, k, v = r['q'][rows, :], r['k'][rows, :], r['v'][rows, :]
        if 'qkg' in r:
            gq, gk = r['qkg'][0:1, :], r['qkg'][1:2, :]
            q = jnp.concatenate([_rms_pair(q[:, :KV_WIDTH], gq, lo), _rms_pair(q[:, KV_WIDTH:], gq, lo)], axis=1)
            k = _rms_pair(k, gk, lo)
        r['k_out'][bb] = k
        r['v_out'][bb] = v
        r['o'][rows, :] = _attn_core(q, k.astype(BF16), v.astype(BF16), lo, sinks=sinks)


def _attn_lat_kernel(names, window, *refs):
    r = dict(zip(names, refs))
    tq = TQ_ATTN
    i = pl.program_id(1)
    lo, first_half = _pair_lanes(tq)
    qk_norm = 'qkg' in r

    @pl.when(i == 0)
    def _():
        lo_k, first_half_k = _pair_lanes(DEC_SEQ)
        k = r['k'][...]
        if qk_norm:
            k = _rms_pair(k, r['qkg'][1:2, :], lo_k)
        r['k_scr'][...] = _rope_pair(k, r['cos_k'][...], r['sin_k'][...], first_half_k).astype(BF16)

    q = r['q'][...]
    halves = []
    for c0 in (0, KV_WIDTH):
        x = q[:, c0:c0 + KV_WIDTH]
        if qk_norm:
            x = _rms_pair(x, r['qkg'][0:1, :], lo)
        halves.append(_rope_pair(x, r['cos_q'][...], r['sin_q'][...], first_half))
    q = jnp.concatenate(halves, axis=1)
    sinks = [r['sink'][h] for h in HEAD_ORDER] if 'sink' in r else None
    ctx = (r['kc'][...].astype(BF16), r['vc'][...].astype(BF16))
    if window:
        span = tq + 2 * WINDOW
        start = pl.multiple_of(jnp.clip(i * tq - WINDOW, 0, DEC_SEQ - span), WINDOW)
        qpos = i * tq + lax.broadcasted_iota(jnp.int32, (tq, span), 0)
        kpos = start + lax.broadcasted_iota(jnp.int32, (tq, span), 1)
        mask = jnp.abs(qpos - kpos) <= WINDOW
        k_bf = r['k_scr'][pl.ds(start, span), :]
        v_bf = r['v'][pl.ds(start, span), :].astype(BF16)
    else:
        mask = None
        k_bf = r['k_scr'][...]
        v_bf = r['v'][...].astype(BF16)
    r['o'][...] = _attn_core(q, k_bf, v_bf, lo, mask=mask, ctx=ctx, sinks=sinks)


def _attention_context(u, layer, cache_prev, *, col_q, col_k, col_v, sink=None, qkg2=None):
    nb = NB_CTX if qkg2 is None else NB_CTX_NORMED
    rows = nb * SEQ
    qb, kb, vb = col_q // GROUP_WIDTH, col_k // KV_WIDTH, col_v // KV_WIDTH
    names = ['q', 'k', 'v']
    args = [u, u, u]
    in_specs = [pl.BlockSpec((rows, GROUP_WIDTH), lambda b: (b, qb)),
                pl.BlockSpec((rows, KV_WIDTH), lambda b: (b, kb)),
                pl.BlockSpec((rows, KV_WIDTH), lambda b: (b, vb))]
    if sink is not None:
        names.append('sink'); args.append(sink)
        in_specs.append(pl.BlockSpec(memory_space=pltpu.SMEM))
    if qkg2 is not None:
        names.append('qkg'); args.append(qkg2)
        in_specs.append(pl.BlockSpec((2, KV_WIDTH), lambda b: (0, 0)))
    aliases = {}
    if cache_prev is not None:
        for j, prev in enumerate(cache_prev):
            aliases[len(args)] = 1 + j
            names.append(f'prev{j}'); args.append(prev); in_specs.append(_any_spec())
    names += ['o', 'k_out', 'v_out']
    cache_shape = jax.ShapeDtypeStruct((BATCH, DEPTH, SEQ, KV_WIDTH), F32)
    cache_spec = pl.BlockSpec((nb, None, SEQ, KV_WIDTH), lambda b: (b, layer, 0, 0))
    return pl.pallas_call(
        functools.partial(_attn_ctx_kernel, tuple(names), nb),
        grid=(BATCH // nb,), in_specs=in_specs,
        out_specs=(pl.BlockSpec((rows, GROUP_WIDTH), lambda b: (b, 0)), cache_spec, cache_spec),
        out_shape=(jax.ShapeDtypeStruct((N_TOK, GROUP_WIDTH), F32), cache_shape, cache_shape),
        input_output_aliases=aliases,
        compiler_params=pltpu.CompilerParams(dimension_semantics=("parallel",)),
        name="attn_context",
    )(*args)


def _attention_latent(u, prev, layer, *, col_q, col_k, col_v, cache_k, cache_v, window=False, sink=None,
                      qkg2=None):
    tq = TQ_ATTN
    nq = DEC_SEQ // tq
    row0_q, row0_k = N_CTX // tq, N_CTX // DEC_SEQ
    qb, kb, vb = col_q // GROUP_WIDTH, col_k // KV_WIDTH, col_v // KV_WIDTH
    cos, sin = _rope_tables(DEC_SEQ)
    const = lambda b, i: (0, 0)
    names = ['q', 'k', 'v', 'kc', 'vc', 'cos_q', 'sin_q', 'cos_k', 'sin_k']
    args = [u, u, u, cache_k, cache_v, cos, sin, cos, sin]
    in_specs = [
        pl.BlockSpec((tq, GROUP_WIDTH), lambda b, i: (row0_q + b * nq + i, qb)),
        pl.BlockSpec((DEC_SEQ, KV_WIDTH), lambda b, i: (row0_k + b, kb)),
        pl.BlockSpec((DEC_SEQ, KV_WIDTH), lambda b, i: (row0_k + b, vb)),
        pl.BlockSpec((None, None, PAST_LEN, KV_WIDTH), lambda b, i: (b, layer, 0, 0)),
        pl.BlockSpec((None, None, PAST_LEN, KV_WIDTH), lambda b, i: (b, layer, 0, 0)),
        pl.BlockSpec((tq, KV_WIDTH), lambda b, i: (i, 0)),
        pl.BlockSpec((tq, KV_WIDTH), lambda b, i: (i, 0)),
        pl.BlockSpec((DEC_SEQ, KV_WIDTH), const),
        pl.BlockSpec((DEC_SEQ, KV_WIDTH), const),
    ]
    if sink is not None:
        names.append('sink'); args.append(sink)
        in_specs.append(pl.BlockSpec(memory_space=pltpu.SMEM))
    if qkg2 is not None:
        names.append('qkg'); args.append(qkg2)
        in_specs.append(pl.BlockSpec((2, KV_WIDTH), const))
    aliases = {len(args): 0}
    names.append('prev'); args.append(prev); in_specs.append(_any_spec())
    names += ['o', 'k_scr']
    return pl.pallas_call(
        functools.partial(_attn_lat_kernel, tuple(names), window),
        grid=(DEC_BATCH, nq), in_specs=in_specs,
        out_specs=pl.BlockSpec((tq, GROUP_WIDTH), lambda b, i: (row0_q + b * nq + i, 0)),
        out_shape=jax.ShapeDtypeStruct((N_TOK, GROUP_WIDTH), F32),
        scratch_shapes=[pltpu.VMEM((DEC_SEQ, KV_WIDTH), BF16)],
        input_output_aliases=aliases,
        compiler_params=pltpu.CompilerParams(dimension_semantics=("parallel", "arbitrary")),
        name="attn_latent",
    )(*args)


def _dft_matrices(length):
    n = length
    k = np.arange(n, dtype=np.int64)[:, None]
    s = np.arange(n, dtype=np.int64)[None, :]
    ang = np.pi * ((k * s) % (2 * n)).astype(np.float64) / n
    fwd_cos = np.cos(ang)
    fwd_sin = -np.sin(ang)
    fwd_sin[0, :] = 1.0 - 2.0 * (np.arange(n) % 2)
    fwd = np.concatenate([fwd_cos, fwd_sin], axis=0)
    wk = np.full((n,), 2.0)
    wk[0] = 1.0
    inv_cos = (np.cos(ang) * wk[:, None]).T / (2 * n)
    inv_sin = (-2.0 * np.sin(ang)).T / (2 * n)
    inv_sin[:, 0] = (1.0 - 2.0 * (np.arange(n) % 2)) / (2 * n)
    inv = np.concatenate([inv_cos, inv_sin], axis=1)
    return fwd.astype(np.float32), inv.astype(np.float32)


def _filter_features(length):
    t = np.linspace(0.0, 1.0, length, dtype=np.float32)[:, None]
    w = (np.float32(2.0 * math.pi / length) * np.arange(length, dtype=np.float32))[:, None]
    bands = np.linspace(1e-4, HY_BANDS - 1, HY_BANDS, dtype=np.float32)[None, :]
    z = np.concatenate([t, np.cos(bands * w), -np.sin(bands * w)], axis=-1).astype(np.float32)
    zp = np.zeros((length, 128), np.float32)
    zp[:, :HY_EMB] = z
    deltas = np.linspace(math.log(HY_TARGET) / HY_SLOW, math.log(HY_TARGET) / HY_FAST, HY_CH, dtype=np.float32)
    decay = np.exp(-t * np.abs(deltas)).astype(np.float32)
    return zp, decay


def _filter_kernel(z_ref, w1_ref, b1_ref, w2_ref, b2_ref, w3_ref, fr_ref, dec_ref, f_ref, kr_ref, ki_ref):
    n = z_ref.shape[0]
    h = jnp.sin(fr_ref[0:1, :] * (_fdot(z_ref[...], w1_ref[...]) + b1_ref[...]))
    h = jnp.sin(fr_ref[1:2, :] * (_fdot(h, w2_ref[...]) + b2_ref[...]))
    h = _fdot(h, w3_ref[...])
    dec = dec_ref[...]
    half = HY_ORDER * HY_CH
    pos = jnp.concatenate([h[:, j * HY_CH:(j + 1) * HY_CH] * dec for j in range(HY_ORDER)], axis=-1)
    neg = jnp.concatenate([h[:, half + j * HY_CH:half + (j + 1) * HY_CH] * dec for j in range(HY_ORDER)], axis=-1)
    row = lax.broadcasted_iota(jnp.int32, (n, half), 0)
    neg = jnp.where(row == 0, 0.0, neg)
    fwd = f_ref[...]
    a = jnp.dot(fwd, (pos + neg).astype(BF16), preferred_element_type=F32)
    b = jnp.dot(fwd, (pos - neg).astype(BF16), preferred_element_type=F32)
    kr_ref[...] = a[:n]
    ki_ref[...] = jnp.where(row == 0, a[n:], b[n:])


def _hyena_filters(length, fwd, w1p, b1, w2, b2, w3, freq):
    zp, decay = _filter_features(length)
    half = HY_ORDER * HY_CH
    return pl.pallas_call(
        _filter_kernel,
        out_shape=(jax.ShapeDtypeStruct((length, half), F32), jax.ShapeDtypeStruct((length, half), F32)),
        name="hyena_filters",
    )(zp, w1p, b1.reshape(1, HY_FH), w2, b2.reshape(1, HY_FH), w3, freq, decay, fwd)


def _hyena_kernel(z_ref, cw_ref, cb_ref, f_ref, g_ref, kr_ref, ki_ref, bias_ref, o_ref, *, n):
    nb = z_ref.shape[0] // n
    row = lax.broadcasted_iota(jnp.int32, (n, 3 * HY_CH), 0)
    row0 = lax.broadcasted_iota(jnp.int32, (n, HY_CH), 0) == 0
    vs, x1s, x2s = [], [], []
    for bb in range(nb):
        z = z_ref[bb * n:(bb + 1) * n, :]
        prev = jnp.where(row == 0, 0.0, pltpu.roll(z, 1, 0))
        nxt = jnp.where(row == n - 1, 0.0, pltpu.roll(z, n - 1, 0))
        z = prev * cw_ref[0:1, :] + z * cw_ref[1:2, :] + nxt * cw_ref[2:3, :] + cb_ref[...]
        vs.append(z[:, :HY_CH])
        x1s.append(z[:, HY_CH:2 * HY_CH])
        x2s.append(z[:, 2 * HY_CH:])

    def long_conv(xs, order):
        cols = slice(order * HY_CH, (order + 1) * HY_CH)
        kr, ki = kr_ref[:, cols], ki_ref[:, cols]
        spec = jnp.dot(f_ref[...], jnp.concatenate([x.astype(BF16) for x in xs], axis=1),
                       preferred_element_type=F32)
        prods = []
        for bb in range(nb):
            ur, ui = spec[:n, bb * HY_CH:(bb + 1) * HY_CH], spec[n:, bb * HY_CH:(bb + 1) * HY_CH]
            yr = ur * kr - jnp.where(row0, 0.0, ui * ki)
            yi = jnp.where(row0, ui * ki, ur * ki + ui * kr)
            prods.append(jnp.concatenate([yr, yi], axis=0).astype(BF16))
        y = jnp.dot(g_ref[...], jnp.concatenate(prods, axis=1), preferred_element_type=F32)
        return [y[:, bb * HY_CH:(bb + 1) * HY_CH] + xs[bb] * bias_ref[order:order + 1, :] for bb in range(nb)]

    ys = long_conv(vs, 0)
    ys = long_conv([x1s[bb] * ys[bb] for bb in range(nb)], 1)
    for bb in range(nb):
        o_ref[bb * n:(bb + 1) * n, :] = x2s[bb] * ys[bb]


def _hyena(u, prev, *, latent, fwd, inv, kr, ki, conv_w, conv_b, bias):
    n = DEC_SEQ if latent else SEQ
    nb = DEC_BATCH if latent else NB_HYENA
    steps = 1 if latent else BATCH // nb
    row0 = N_CTX // (nb * n) if latent else 0
    half = HY_ORDER * HY_CH
    const = lambda b: (0, 0)
    args = [u, conv_w, conv_b.reshape(1, 3 * HY_CH), fwd, inv, kr, ki, bias]
    in_specs = [
        pl.BlockSpec((nb * n, 3 * HY_CH), lambda b: (row0 + b, COL_HY // (3 * HY_CH))),
        pl.BlockSpec((3, 3 * HY_CH), const),
        pl.BlockSpec((1, 3 * HY_CH), const),
        pl.BlockSpec((2 * n, n), const),
        pl.BlockSpec((n, 2 * n), const),
        pl.BlockSpec((n, half), const),
        pl.BlockSpec((n, half), const),
        pl.BlockSpec((HY_ORDER, HY_CH), const),
    ]
    aliases = {}
    if latent:
        aliases = {len(args): 0}
        args.append(prev)
        in_specs.append(_any_spec())

    def body(*refs):
        if latent:
            refs = refs[:len(args) - 1] + refs[len(args):]
        _hyena_kernel(*refs, n=n)

    return pl.pallas_call(
        body, grid=(steps,), in_specs=in_specs,
        out_specs=pl.BlockSpec((nb * n, HY_CH), lambda b: (row0 + b, 0)),
        out_shape=jax.ShapeDtypeStruct((N_TOK, HY_CH), F32),
        input_output_aliases=aliases,
        compiler_params=pltpu.CompilerParams(dimension_semantics=("parallel",)),
        name="hyena_latent" if latent else "hyena_context",
    )(*args)


def _gla_masks():
    t = np.arange(ROWS_GROUP)[:, None]
    s = np.arange(ROWS_GROUP)[None, :]
    same = (t // GLA_CHUNK) == (s // GLA_CHUNK)
    return np.stack([same & (s <= t), same & (s >= t), same]).astype(np.float32)


def _gla_group(r, rows, z, st):
    c = GLA_CHUNK
    hk = H_GLA * DK_GLA
    n_chunks = ROWS_GROUP // c
    logits = _bdot(r['gl'][rows, :], r['gw'][z]) + r['gb'][z:z + 1, :]
    g = (jnp.minimum(logits, 0.0) - jnp.log1p(jnp.exp(-jnp.abs(logits)))) / GLA_NORM
    causal = r['masks'][z]
    b = _split_dot(causal.astype(BF16), g)
    tot = _split_dot(r['masks'][2].astype(BF16), g)
    k = r['k'][rows, :]
    q_t = r['q'][rows, :] * jnp.exp(b) * (DK_GLA ** -0.5)
    k_t = (k * jnp.exp(-b)).astype(BF16)
    k_e = (k * jnp.exp(tot - b)).astype(BF16)
    dec = jnp.exp(tot)
    v = r['v'][rows, :].astype(BF16)

    klane = lax.broadcasted_iota(jnp.int32, (c, hk), 1) // DK_GLA
    vlane = lax.broadcasted_iota(jnp.int32, (c, GROUP_WIDTH), 1) // DV_GLA
    causal_c = jnp.concatenate([causal[0:c, 0:c]] * H_GLA, axis=1) != 0.0
    bd = (lax.broadcasted_iota(jnp.int32, (GROUP_WIDTH, hk), 0) // DV_GLA
          == lax.broadcasted_iota(jnp.int32, (GROUP_WIDTH, hk), 1) // DK_GLA)
    q_bf = q_t.astype(BF16)
    zero_k = jnp.zeros((c, hk), BF16)
    zero_v = jnp.zeros((c, GROUP_WIDTH), BF16)
    o_intra = [None] * n_chunks
    for ci in range(n_chunks):
        cr = slice(ci * c, (ci + 1) * c)
        k_stack = jnp.concatenate([jnp.where(klane == h, k_t[cr], zero_k) for h in range(H_GLA)], axis=0)
        v_diag = jnp.concatenate([jnp.where(vlane == h, v[cr], zero_v) for h in range(H_GLA)], axis=0)
        s = lax.dot_general(q_bf[cr], k_stack, (((1,), (1,)), ((), ())), preferred_element_type=F32)
        p = jnp.where(causal_c, s, 0.0).astype(BF16)
        o_intra[ci] = jnp.dot(p, v_diag, preferred_element_type=F32)

    o = [None] * n_chunks
    for step in range(n_chunks):
        ci = step if z == 0 else n_chunks - 1 - step
        cr = slice(ci * c, (ci + 1) * c)
        o[ci] = o_intra[ci] + _bdot_nt(q_bf[cr], st)
        st = st * dec[ci * c:ci * c + 1, :] + jnp.where(bd, _bdot_tn(v[cr], k_e[cr]), 0.0)
    return jnp.concatenate(o, axis=0), st


def _head_rms(o, g):
    hb = (lax.broadcasted_iota(jnp.int32, (GROUP_WIDTH, GROUP_WIDTH), 0) // DV_GLA
          == lax.broadcasted_iota(jnp.int32, (GROUP_WIDTH, GROUP_WIDTH), 1) // DV_GLA)
    ms = _split_dot_rhs(o * o, hb.astype(BF16)) * (1.0 / DV_GLA)
    return o * lax.rsqrt(ms + EPS) * g


def _split_dot_rhs(x, m):
    hi = x.astype(BF16)
    lo = (x - hi.astype(F32)).astype(BF16)
    return jnp.dot(hi, m, preferred_element_type=F32) + jnp.dot(lo, m, preferred_element_type=F32)


def _expand_state(s):
    hk = H_GLA * DK_GLA
    bd = (lax.broadcasted_iota(jnp.int32, (GROUP_WIDTH, hk), 0) // DV_GLA
          == lax.broadcasted_iota(jnp.int32, (GROUP_WIDTH, hk), 1) // DK_GLA)
    return jnp.where(bd, jnp.concatenate([s] * H_GLA, axis=0), 0.0)


def _compact_state(st):
    out = st[0:DV_GLA]
    for h in range(1, H_GLA):
        out = out + st[h * DV_GLA:(h + 1) * DV_GLA]
    return out


def _gla_ctx_kernel(names, *refs):
    r = dict(zip(names, refs))
    hk = H_GLA * DK_GLA
    for bb in range(NB_CTX):
        rows = slice(bb * SEQ, (bb + 1) * SEQ)
        zero = jnp.zeros((GROUP_WIDTH, hk), F32)
        o_f, st_f = _gla_group(r, rows, 0, zero)
        o_b, st_b = _gla_group(r, rows, 1, zero)
        r['o'][rows, :] = _head_rms(o_f + o_b, r['mg'][...])
        r['sf'][bb, 0] = _compact_state(st_f)
        r['sf'][bb, 1] = _compact_state(st_b)


def _gla_lat_kernel(names, *refs):
    r = dict(zip(names, refs))
    n_groups = DEC_SEQ // ROWS_GROUP

    def run(z):
        r['st'][...] = _expand_state(r['s0'][z])

        def body(step, carry):
            gi = step if z == 0 else n_groups - 1 - step
            rows = pl.ds(pl.multiple_of(gi * ROWS_GROUP, ROWS_GROUP), ROWS_GROUP)
            o, st = _gla_group(r, rows, z, r['st'][...])
            r['st'][...] = st
            if z == 0:
                r['o'][rows, :] = o
            else:
                r['o'][rows, :] = _head_rms(r['o'][rows, :] + o, r['mg'][...])
            return carry

        lax.fori_loop(0, n_groups, body, 0)

    run(0)
    run(1)


def _gla(u, prev, *, latent, gate_w, gate_b, mix_g_d, s0=None):
    n = DEC_SEQ if latent else NB_CTX * SEQ
    steps = DEC_BATCH if latent else BATCH // NB_CTX
    row0 = N_CTX // n if latent else 0
    hk = H_GLA * DK_GLA
    const2 = lambda b: (0, 0)
    names = ['q', 'k', 'v', 'gl', 'gw', 'gb', 'mg', 'masks']
    args = [u, u, u, u, gate_w, gate_b, mix_g_d, _gla_masks()]
    in_specs = [
        pl.BlockSpec((n, hk), lambda b: (row0 + b, COL_QD // hk)),
        pl.BlockSpec((n, hk), lambda b: (row0 + b, COL_KD // hk)),
        pl.BlockSpec((n, GROUP_WIDTH), lambda b: (row0 + b, COL_VD // GROUP_WIDTH)),
        pl.BlockSpec((n, 128), lambda b: (row0 + b, COL_GD // 128)),
        pl.BlockSpec((2, 128, hk), lambda b: (0, 0, 0)),
        pl.BlockSpec((2, hk), const2),
        pl.BlockSpec((1, GROUP_WIDTH), const2),
        pl.BlockSpec((3, ROWS_GROUP, ROWS_GROUP), lambda b: (0, 0, 0)),
    ]
    out_shape = jax.ShapeDtypeStruct((N_TOK, GROUP_WIDTH), F32)
    out_specs = pl.BlockSpec((n, GROUP_WIDTH), lambda b: (row0 + b, 0))
    aliases, scratch = {}, []
    if latent:
        names.append('s0'); args.append(s0)
        in_specs.append(pl.BlockSpec((None, 2, DV_GLA, hk), lambda b: (b, 0, 0, 0)))
        aliases = {len(args): 0}
        names.append('prev'); args.append(prev); in_specs.append(_any_spec())
        names += ['o', 'st']
        scratch = [pltpu.VMEM((GROUP_WIDTH, hk), F32)]
        body = functools.partial(_gla_lat_kernel, tuple(names))
    else:
        names += ['o', 'sf']
        out_shape = (out_shape, jax.ShapeDtypeStruct((BATCH, 2, DV_GLA, hk), F32))
        out_specs = (out_specs, pl.BlockSpec((NB_CTX, 2, DV_GLA, hk), lambda b: (b, 0, 0, 0)))
        body = functools.partial(_gla_ctx_kernel, tuple(names))
    return pl.pallas_call(
        body, grid=(steps,), in_specs=in_specs, out_specs=out_specs, out_shape=out_shape,
        scratch_shapes=scratch, input_output_aliases=aliases,
        compiler_params=pltpu.CompilerParams(dimension_semantics=("parallel",)),
        name="gla_latent" if latent else "gla_context",
    )(*args)


def _merge_kernel(h_ref, oa_ref, ob_ref, oc_ref, od_ref, rd_ref, gt_ref, mg_ref, wo_ref, o_ref):
    gw = GROUP_WIDTH

    def gain(g, order):
        return jnp.concatenate([mg_ref[:, g * gw + h * HEAD_DIM:g * gw + (h + 1) * HEAD_DIM] for h in order], axis=1)

    def weight(g, order):
        return jnp.concatenate([wo_ref[g * gw + h * HEAD_DIM:g * gw + (h + 1) * HEAD_DIM, :] for h in order], axis=0)

    natural = range(N_HEADS)
    ys = [
        _rms(oa_ref[...], gain(0, HEAD_ORDER)),
        _rms(ob_ref[...], gain(1, natural)),
        _rms(oc_ref[...], gain(2, HEAD_ORDER)),
        od_ref[...] * _silu(rd_ref[:, COL_RD - COL_GD:COL_RD - COL_GD + gw]),
    ]
    orders = [HEAD_ORDER, natural, HEAD_ORDER, natural]
    mix = jnp.dot(ys[0].astype(BF16), weight(0, orders[0]), preferred_element_type=F32)
    for j in range(1, 4):
        mix = mix + jnp.dot(ys[j].astype(BF16), weight(j, orders[j]), preferred_element_type=F32)
    o_ref[...] = h_ref[...] + gt_ref[...] * mix


def _merge(h, oa, ob, oc, od, u, mod, mix_g, w_out, layer):
    tm = TM_PROJ
    grp = pl.BlockSpec((tm, GROUP_WIDTH), lambda i: (i, 0))
    return pl.pallas_call(
        _merge_kernel,
        grid=(N_TOK // tm,),
        in_specs=[
            pl.BlockSpec((tm, D_MODEL), lambda i: (i, 0)),
            grp, grp, grp, grp,
            pl.BlockSpec((tm, COL_RD_BLOCK), lambda i: (i, COL_GD // COL_RD_BLOCK)),
            _mod_spec(tm, 5, 1),
            pl.BlockSpec((None, 1, D_MODEL), lambda i: (layer, 0, 0)),
            pl.BlockSpec((None, D_MODEL, D_MODEL), lambda i: (layer, 0, 0)),
        ],
        out_specs=pl.BlockSpec((tm, D_MODEL), lambda i: (i, 0)),
        out_shape=jax.ShapeDtypeStruct((N_TOK, D_MODEL), F32),
        compiler_params=pltpu.CompilerParams(dimension_semantics=("parallel",)),
        name="merge",
    )(h, oa, ob, oc, od, u, mod, mix_g, w_out)


def _gate_weights(gate_w):
    out = jnp.zeros((2, 128, H_GLA * DK_GLA), gate_w.dtype)
    out = out.at[0, 0:GLA_RANK].set(gate_w[0])
    return out.at[1, GLA_RANK:2 * GLA_RANK].set(gate_w[1])


def _states_to_kernel(st):
    b = st.shape[0]
    return st.transpose(0, 1, 4, 2, 3).reshape(b, 2, DV_GLA, H_GLA * DK_GLA)


def _states_from_kernel(st):
    b = st.shape[0]
    return st.reshape(b, 2, DV_GLA, H_GLA, DK_GLA).transpose(0, 1, 3, 4, 2)


def kernel(x_prompt, x_sample, cache_swa_k, cache_swa_v, cache_gqa_k, cache_gqa_v, state_gla, c, c_ctx, w_mod, b_mod, norm_g, ffn_w_in, ffn_w_out, w_in, w_out, mix_g, swa_sink, qk_norm_g, hy_conv_w, hy_conv_b, hy_w1, hy_b1, hy_w2, hy_b2, hy_w3, hy_freq, hy_bias, gla_gate_w, gla_gate_b, final_g):
    cond = jnp.zeros((MOD_ROWS, D_MODEL), F32).at[0].set(c_ctx).at[1:1 + DEC_BATCH].set(c)
    mod_all = _modulation(cond, w_mod, b_mod).reshape(DEPTH, MOD_ROWS, N_MOD, 1, D_MODEL)

    dft = {n: tuple(jnp.asarray(m).astype(BF16) for m in _dft_matrices(n)) for n in (SEQ, DEC_SEQ)}

    xs = [x_prompt.reshape(N_CTX, D_MODEL), x_sample.reshape(N_LAT, D_MODEL)]
    ffn_in = ffn_w_in.astype(BF16)
    ffn_out = ffn_w_out.astype(BF16)
    ng = norm_g.reshape(DEPTH, 3, 1, D_MODEL)
    proj_w = jnp.pad(w_in, ((0, 0), (0, 0), (0, PROJ_WIDTH - REF_END))).astype(BF16)
    out_w = w_out.astype(BF16)
    mix_gp = mix_g.reshape(DEPTH, 1, D_MODEL)
    cache_shape = (DEC_BATCH, DEPTH, PAST_LEN, KV_WIDTH)
    ck_a, cv_a = cache_swa_k.reshape(cache_shape), cache_swa_v.reshape(cache_shape)
    ck_c, cv_c = cache_gqa_k.reshape(cache_shape), cache_gqa_v.reshape(cache_shape)
    cache_a = cache_c = None
    sts = []
    for l in range(DEPTH):
        mod = mod_all[l]
        if len(xs) == 2:
            h = _ffn(xs, mod, 0, ng, ffn_in, ffn_out, l, 0)
        else:
            h = _ffn_lookahead(xs[0], mod, 0, ng, ffn_in, ffn_out, l, 0)
        u = _in_proj(h, mod, ng, proj_w, l)

        qkg2 = jnp.tile(qk_norm_g[l], (1, N_KV))

        oa, *cache_a = _attention_context(u, l, cache_a, col_q=COL_QA, col_k=COL_KA, col_v=COL_VA,
                                          sink=swa_sink[l])
        oa = _attention_latent(u, oa, l, col_q=COL_QA, col_k=COL_KA, col_v=COL_VA, cache_k=ck_a, cache_v=cv_a,
                               window=True, sink=swa_sink[l])
        oc, *cache_c = _attention_context(u, l, cache_c, col_q=COL_QC, col_k=COL_KC, col_v=COL_VC, qkg2=qkg2)
        oc = _attention_latent(u, oc, l, col_q=COL_QC, col_k=COL_KC, col_v=COL_VC, cache_k=ck_c, cache_v=cv_c,
                               qkg2=qkg2)

        w1p = jnp.zeros((128, HY_FH), F32).at[:HY_EMB].set(hy_w1[l])
        ob = None
        for latent, n in ((False, SEQ), (True, DEC_SEQ)):
            fwd, inv = dft[n]
            kr, ki = _hyena_filters(n, fwd, w1p, hy_b1[l], hy_w2[l], hy_b2[l], hy_w3[l], hy_freq[l])
            ob = _hyena(u, ob, latent=latent, fwd=fwd, inv=inv, kr=kr, ki=ki, conv_w=hy_conv_w[l],
                        conv_b=hy_conv_b[l], bias=hy_bias[l])

        gw = _gate_weights(gla_gate_w[l]).astype(BF16)
        mg_d = mix_g[l, 3 * GROUP_WIDTH:].reshape(1, GROUP_WIDTH)
        od, st = _gla(u, None, latent=False, gate_w=gw, gate_b=gla_gate_b[l], mix_g_d=mg_d)
        od = _gla(u, od, latent=True, gate_w=gw, gate_b=gla_gate_b[l], mix_g_d=mg_d,
                  s0=_states_to_kernel(state_gla[:, l]))

        h = _merge(h, oa, ob, oc, od, u, mod, mix_gp, out_w, l)
        xs = [_ffn_lookahead(h, mod, 6, ng, ffn_in, ffn_out, l, 1)] if l + 1 < DEPTH else None
        if xs is None:
            y_prompt, y_sample = _ffn([h], mod, 6, ng, ffn_in, ffn_out, l, 1, final_g=final_g)
        sts.append(_states_from_kernel(st))

    y_prompt = y_prompt.reshape(BATCH, SEQ, D_MODEL)
    y_sample = y_sample.reshape(DEC_BATCH, DEC_SEQ, D_MODEL)
    caches = [x.reshape(BATCH, DEPTH, SEQ, N_KV, HEAD_DIM) for x in (*cache_a, *cache_c)]
    return (y_prompt, y_sample, *caches, jnp.stack(sts, axis=1))
```
